```python
import jax, jax.numpy as jnp
from jax import lax
import numpy as np

D_MODEL = 1024
BATCH = 16
SEQ = 2048
DEPTH = 2
DEC_BATCH = 32
DEC_SEQ = 32
PAST_LEN = 4096

CHUNK = 64

N_A_LAYERS = DEPTH // 2
N_B_LAYERS = DEPTH - N_A_LAYERS
N_DENSE = (DEPTH + 1) // 2
N_MOE = DEPTH // 2

POOL_WINDOWS = (2, 4, 8, 16)
N_POOL_GROUPS = len(POOL_WINDOWS)
POOL_GROUP = D_MODEL // N_POOL_GROUPS
POOL_HIST = max(POOL_WINDOWS) - 1

N_HEADS = 16
HEAD_DIM = D_MODEL // N_HEADS
KV_WIDTH = N_HEADS * HEAD_DIM
Q_BLOCK = 128

D_FF = ((8 * D_MODEL // 3 + 127) // 128) * 128
N_EXPERTS = 8
TOP_K = 2
EPS = 1e-6

kernel_name = "yoco_pool_fox_moe_stream_step"


def rms_norm(x, g):
    xf = x.astype(jnp.float32)
    y = xf * lax.rsqrt(jnp.mean(xf * xf, axis=-1, keepdims=True) + EPS)
    return (y * g.astype(jnp.float32)).astype(x.dtype)


def pool_mixer(u, hist, start_pos, w_pool, pool_scale):
    b, n, d = u.shape
    full = jnp.concatenate([hist.astype(u.dtype), u], axis=1)
    full_f = full.astype(jnp.float32)
    csum = jnp.concatenate([jnp.zeros((b, 1, d), jnp.float32), jnp.cumsum(full_f, axis=1)], axis=1)
    end = csum[:, POOL_HIST + 1:]
    n_avail = start_pos + jnp.arange(n, dtype=jnp.int32) + 1
    groups = []
    for g, w in enumerate(POOL_WINDOWS):
        lo, hi = g * POOL_GROUP, (g + 1) * POOL_GROUP
        s0 = POOL_HIST + 1 - w
        win_sum = end[:, :, lo:hi] - csum[:, s0:s0 + n, lo:hi]
        cnt = jnp.minimum(n_avail, w).astype(jnp.float32)[None, :, None]
        groups.append(win_sum / cnt)
    pooled = jnp.stack(groups, axis=2)
    diff = pooled - full_f[:, POOL_HIST:].reshape(b, n, N_POOL_GROUPS, POOL_GROUP)
    y = jnp.einsum('bngc,gce->bnge', diff.astype(u.dtype), w_pool).reshape(b, n, d)
    return y * pool_scale, full[:, -POOL_HIST:]


def swiglu(x, w_gu, w_down):
    gate, up = jnp.split(x @ w_gu, 2, axis=-1)
    return (jax.nn.silu(gate) * up) @ w_down


def moe_swiglu(x, w_router, w_gu, w_down):
    logits = (x @ w_router).astype(jnp.float32)
    top_v, top_i = lax.top_k(logits, TOP_K)
    top_w = jax.nn.softmax(top_v, axis=-1)
    gates = jnp.sum(jax.nn.one_hot(top_i, N_EXPERTS, dtype=jnp.float32) * top_w[..., None], axis=-2)
    out = jnp.zeros(x.shape, jnp.float32)
    for e in range(N_EXPERTS):
        out = out + gates[..., e:e + 1] * swiglu(x, w_gu[e], w_down[e]).astype(jnp.float32)
    return out.astype(x.dtype)


def shared_kv(h, g_kv, w_kvf, b_f, g_k):
    b, n, _ = h.shape
    proj = rms_norm(h, g_kv) @ w_kvf
    k = rms_norm(proj[..., :KV_WIDTH].reshape(b, n, N_HEADS, HEAD_DIM), g_k)
    v = proj[..., KV_WIDTH:2 * KV_WIDTH].reshape(b, n, N_HEADS, HEAD_DIM)
    logf = jax.nn.log_sigmoid((proj[..., 2 * KV_WIDTH:] + b_f).astype(jnp.float32))
    return k, v, logf


def fox_attend(q, c_q, q_pos, k, v, c_k, k_pos):
    s = jnp.einsum('bqhd,bkhd->bhqk', q, k).astype(jnp.float32) * (HEAD_DIM ** -0.5)
    s = s + jnp.transpose(c_q, (0, 2, 1))[:, :, :, None] - jnp.transpose(c_k, (0, 2, 1))[:, :, None, :]
    s = jnp.where((k_pos[None, :] <= q_pos[:, None])[None, None], s, -jnp.inf)
    p = jax.nn.softmax(s, axis=-1).astype(v.dtype)
    return jnp.einsum('bhqk,bkhd->bqhd', p, v)


def fox_prompt(q, k, v, logf):
    b, s = q.shape[:2]
    c = jnp.cumsum(logf.astype(jnp.float32), axis=1)
    nb = s // Q_BLOCK
    pos = jnp.arange(s, dtype=jnp.int32)
    qb = jnp.moveaxis(q.reshape(b, nb, Q_BLOCK, N_HEADS, HEAD_DIM), 1, 0)
    cb = jnp.moveaxis(c.reshape(b, nb, Q_BLOCK, N_HEADS), 1, 0)
    pb = pos.reshape(nb, Q_BLOCK)
    out = lax.map(lambda a: fox_attend(a[0], a[1], a[2], k, v, c, pos), (qb, cb, pb))
    return jnp.moveaxis(out, 0, 1).reshape(b, s, KV_WIDTH)


def fox_sample(q, k_new, v_new, logf_new, cache_k, cache_v, cache_logf):
    b, n = q.shape[:2]
    p_len = cache_k.shape[1]
    k = jnp.concatenate([cache_k.astype(k_new.dtype), k_new], axis=1)
    v = jnp.concatenate([cache_v.astype(v_new.dtype), v_new], axis=1)
    c = jnp.cumsum(jnp.concatenate([cache_logf.astype(jnp.float32), logf_new], axis=1), axis=1)
    k_pos = jnp.arange(p_len + n, dtype=jnp.int32)
    q_pos = p_len + jnp.arange(n, dtype=jnp.int32)
    return fox_attend(q, c[:, p_len:], q_pos, k, v, c, k_pos).reshape(b, n, KV_WIDTH)


def run_trunk(x, pool_hist, kv_cache, params):
    (g_pool_norm, w_pool, pool_scale, g_kv_norm, w_kvf, b_f, g_k, g_attn_norm, w_q, g_q, w_o,
     g_ffn_norm, w_ffn_gu, w_ffn_down, w_router, w_moe_gu, w_moe_down) = params
    b, n, _ = x.shape
    start_pos = 0 if kv_cache is None else kv_cache[0].shape[1]
    h = x
    pool_states = []
    k = v = logf = None
    for layer in range(DEPTH):
        if layer < N_A_LAYERS:
            hist = jnp.zeros((b, POOL_HIST, D_MODEL), x.dtype) if pool_hist is None else pool_hist[layer]
            y, st = pool_mixer(rms_norm(h, g_pool_norm[layer]), hist, start_pos, w_pool[layer], pool_scale[layer])
            pool_states.append(st)
            h = h + y
        else:
            if layer == N_A_LAYERS:
                k, v, logf = shared_kv(h, g_kv_norm, w_kvf, b_f, g_k)
            i = layer - N_A_LAYERS
            q = rms_norm((rms_norm(h, g_attn_norm[i]) @ w_q[i]).reshape(b, n, N_HEADS, HEAD_DIM), g_q[i])
            if kv_cache is None:
                o = fox_prompt(q, k, v, logf)
            else:
                o = fox_sample(q, k, v, logf, *kv_cache)
            h = h + o @ w_o[i]
        u = rms_norm(h, g_ffn_norm[layer])
        if layer % 2 == 0:
            h = h + swiglu(u, w_ffn_gu[layer // 2], w_ffn_down[layer // 2])
        else:
            h = h + moe_swiglu(u, w_router[layer // 2], w_moe_gu[layer // 2], w_moe_down[layer // 2])
    return h, jnp.stack(pool_states, axis=0), k, v, logf


def setup_inputs(seed: int = 0) -> dict:
    key = jax.random.key(seed)
    ks = jax.random.split(key, 24)

    def nrm(i, shape, scale):
        return jax.random.normal(ks[i], shape, jnp.float32) * scale

    def gain(i, shape):
        return 1.0 + 0.05 * jax.random.normal(ks[i], shape, jnp.float32)

    d, f = D_MODEL, D_FF
    return {
        "x_prompt": nrm(0, (BATCH, SEQ, d), 1.0),
        "x_sample": nrm(1, (DEC_BATCH, DEC_SEQ, d), 1.0),
        "state_pool": nrm(2, (N_A_LAYERS, DEC_BATCH, POOL_HIST, d), 1.0),
        "cache_k": nrm(3, (DEC_BATCH, PAST_LEN, N_HEADS, HEAD_DIM), 1.0),
        "cache_v": nrm(4, (DEC_BATCH, PAST_LEN, N_HEADS, HEAD_DIM), 1.0),
        "cache_logf": jax.nn.log_sigmoid(3.5 + 1.5 * jax.random.normal(ks[5], (DEC_BATCH, PAST_LEN, N_HEADS), jnp.float32)),
        "g_pool_norm": gain(6, (N_A_LAYERS, d)),
        "w_pool": nrm(7, (N_A_LAYERS, N_POOL_GROUPS, POOL_GROUP, POOL_GROUP), POOL_GROUP ** -0.5),
        "pool_scale": gain(8, (N_A_LAYERS, d)),
        "g_kv_norm": gain(9, (d,)),
        "w_kvf": nrm(10, (d, 2 * KV_WIDTH + N_HEADS), d ** -0.5),
        "b_f": jax.random.uniform(ks[11], (N_HEADS,), dtype=jnp.float32, minval=1.0, maxval=6.0),
        "g_k": gain(12, (HEAD_DIM,)),
        "g_attn_norm": gain(13, (N_B_LAYERS, d)),
        "w_q": nrm(14, (N_B_LAYERS, d, KV_WIDTH), d ** -0.5),
        "g_q": gain(15, (N_B_LAYERS, HEAD_DIM)),
        "w_o": nrm(16, (N_B_LAYERS, KV_WIDTH, d), KV_WIDTH ** -0.5),
        "g_ffn_norm": gain(17, (DEPTH, d)),
        "w_ffn_gu": nrm(18, (N_DENSE, d, 2 * f), d ** -0.5),
        "w_ffn_down": nrm(19, (N_DENSE, f, d), f ** -0.5),
        "w_router": nrm(20, (N_MOE, d, N_EXPERTS), d ** -0.5),
        "w_moe_gu": nrm(21, (N_MOE, N_EXPERTS, d, 2 * f), d ** -0.5),
        "w_moe_down": nrm(22, (N_MOE, N_EXPERTS, f, d), f ** -0.5),
    }


def reference(x_prompt, x_sample, state_pool, cache_k, cache_v, cache_logf,
              g_pool_norm, w_pool, pool_scale, g_kv_norm, w_kvf, b_f, g_k,
              g_attn_norm, w_q, g_q, w_o, g_ffn_norm, w_ffn_gu, w_ffn_down,
              w_router, w_moe_gu, w_moe_down):
    params = (g_pool_norm, w_pool, pool_scale, g_kv_norm, w_kvf, b_f, g_k, g_attn_norm, w_q, g_q, w_o,
              g_ffn_norm, w_ffn_gu, w_ffn_down, w_router, w_moe_gu, w_moe_down)
    y_prompt, pool_prompt, k_prompt, v_prompt, logf_prompt = run_trunk(x_prompt, None, None, params)
    y_sample, pool_sample, k_sample, v_sample, logf_sample = run_trunk(
        x_sample, state_pool, (cache_k, cache_v, cache_logf), params)
    return (y_prompt, y_sample, pool_prompt, k_prompt, v_prompt, logf_prompt,
            pool_sample, k_sample, v_sample, logf_sample)
```

```python
import functools
import math

import jax
import jax.numpy as jnp
from jax import lax
from jax.experimental import pallas as pl
from jax.experimental.pallas import tpu as pltpu

EPS = 1e-6
POOL_WINDOWS = (2, 4, 8, 16)
HALO = max(POOL_WINDOWS)
POOL_HIST = HALO - 1
TOP_K = 2
LANES = 128
MXU_DIM = 256
VMEM_LIMIT = 56 * 1024 * 1024
NEG_INF = float("-inf")

BF16 = jnp.bfloat16
F32 = jnp.float32


def _params(semantics, vmem=None):
    return pltpu.CompilerParams(dimension_semantics=semantics, vmem_limit_bytes=vmem)


def _dot(a, b):
    return jnp.dot(a, b, preferred_element_type=F32)


def _dot_nt(a, b):
    return lax.dot_general(a, b, (((1,), (1,)), ((), ())), preferred_element_type=F32)


def _rms_base(x):
    return x * lax.rsqrt(jnp.mean(x * x, axis=-1, keepdims=True) + EPS)


def _split2(x):
    hi = x.astype(BF16)
    lo = (x - hi.astype(F32)).astype(BF16)
    return hi, lo


def _split3(x):
    hi = x.astype(BF16)
    r = x - hi.astype(F32)
    mid = r.astype(BF16)
    lo = (r - mid.astype(F32)).astype(BF16)
    return hi, mid, lo


def _ffn_chunks(f):
    out, s = [], 0
    while s < f:
        n = min(4 * MXU_DIM, f - s)
        out.append((s, n))
        s += n
    return out


def _pool_kernel(x_ref, halo_ref, g_ref, w_ref, scale_ref, h_ref, st_ref, *,
                 tm, start_pos, halo_is_normed):
    i = pl.program_id(1)
    x = x_ref[...]
    g = g_ref[...]
    u = _rms_base(x) * g
    if halo_is_normed:
        uh = halo_ref[...]
    else:
        uh = _rms_base(halo_ref[...]) * g
        uh = jnp.where(i > 0, uh, 0.0)
    ext = jnp.concatenate([uh, u], axis=0)

    row = lax.broadcasted_iota(jnp.int32, (tm, 1), 0)
    avail = start_pos + i * tm + row + 1
    group = x.shape[1] // len(POOL_WINDOWS)
    ys = []
    for gi, w in enumerate(POOL_WINDOWS):
        lo, hi = gi * group, (gi + 1) * group
        s = ext[:, lo:hi]
        step = 1
        while step < w:
            s = s + pltpu.roll(s, step, 0)
            step *= 2
        inv_cnt = 1.0 / jnp.minimum(avail, w).astype(F32)
        pooled = s[HALO:, :] * inv_cnt
        diff = pooled - u[:, lo:hi]
        ys.append(_dot(diff.astype(BF16), w_ref[gi]))
    y = jnp.concatenate(ys, axis=1) * scale_ref[...]
    h_ref[...] = x + y

    @pl.when(i == pl.num_programs(1) - 1)
    def _():
        st_ref[...] = u[tm - HALO:, :]


def _pool_layer(x, hist, g, w_pool, pool_scale, start_pos):
    b, n, d = x.shape
    tm = min(n, 256)
    assert n % tm == 0 and tm % HALO == 0 and n >= HALO
    nt = n // tm
    if hist is None:
        halo_arr = x
        halo_spec = pl.BlockSpec(
            (None, HALO, d), lambda bi, i: (bi, jnp.maximum(i * (tm // HALO) - 1, 0), 0))
    else:
        assert nt == 1
        halo_arr = jnp.pad(hist, ((0, 0), (HALO - POOL_HIST, 0), (0, 0)))
        halo_spec = pl.BlockSpec((None, HALO, d), lambda bi, i: (bi, 0, 0))
    ng = len(POOL_WINDOWS)
    h, st = pl.pallas_call(
        functools.partial(_pool_kernel, tm=tm, start_pos=start_pos,
                          halo_is_normed=hist is not None),
        grid=(b, nt),
        in_specs=[
            pl.BlockSpec((None, tm, d), lambda bi, i: (bi, i, 0)),
            halo_spec,
            pl.BlockSpec((1, d), lambda bi, i: (0, 0)),
            pl.BlockSpec((ng, d // ng, d // ng), lambda bi, i: (0, 0, 0)),
            pl.BlockSpec((1, d), lambda bi, i: (0, 0)),
        ],
        out_specs=[
            pl.BlockSpec((None, tm, d), lambda bi, i: (bi, i, 0)),
            pl.BlockSpec((None, HALO, d), lambda bi, i: (bi, 0, 0)),
        ],
        out_shape=[jax.ShapeDtypeStruct((b, n, d), F32),
                   jax.ShapeDtypeStruct((b, HALO, d), F32)],
        compiler_params=_params(("parallel", "arbitrary")),
        name="pool_mixer",
    )(x, halo_arr, g.reshape(1, d), w_pool.astype(BF16), pool_scale.reshape(1, d))
    return h, st[:, HALO - POOL_HIST:, :]


def _swiglu(xb, wgu_ref, wd_ref, f):
    acc = None
    for s, n in _ffn_chunks(f):
        gate = _dot(xb, wgu_ref[:, s:s + n])
        up = _dot(xb, wgu_ref[:, f + s:f + s + n])
        act = (gate * jax.nn.sigmoid(gate)) * up
        part = _dot(act.astype(BF16), wd_ref[s:s + n, :])
        acc = part if acc is None else acc + part
    return acc


def _dense_ffn_kernel(h_ref, g_ref, wgu_ref, wd_ref, o_ref, *, f):
    h = h_ref[...]
    u = (_rms_base(h) * g_ref[...]).astype(BF16)
    o_ref[...] = h + _swiglu(u, wgu_ref, wd_ref, f)


def _dense_ffn(h, g, w_gu, w_down):
    t, d = h.shape
    f = w_down.shape[0]
    tm = min(t, 512)
    assert t % tm == 0
    return pl.pallas_call(
        functools.partial(_dense_ffn_kernel, f=f),
        grid=(t // tm,),
        in_specs=[
            pl.BlockSpec((tm, d), lambda i: (i, 0)),
            pl.BlockSpec((1, d), lambda i: (0, 0)),
            pl.BlockSpec((d, 2 * f), lambda i: (0, 0)),
            pl.BlockSpec((f, d), lambda i: (0, 0)),
        ],
        out_specs=pl.BlockSpec((tm, d), lambda i: (i, 0)),
        out_shape=jax.ShapeDtypeStruct((t, d), F32),
        compiler_params=_params(("parallel",), VMEM_LIMIT),
        name="dense_ffn",
    )(h, g.reshape(1, d), w_gu.astype(BF16), w_down.astype(BF16))


def _head_norm(x, bd_ref, g):
    sq_hi, sq_lo = _split2(x * x)
    n = bd_ref.shape[0]
    ms = jnp.concatenate(
        [_dot(sq_hi[:, c:c + n], bd_ref[...]) + _dot(sq_lo[:, c:c + n], bd_ref[...])
         for c in range(0, x.shape[1], n)], axis=1)
    return x * lax.rsqrt(ms + EPS) * g


def _log_sigmoid(z):
    return jnp.minimum(z, 0.0) - jnp.log1p(jnp.exp(-jnp.abs(z)))


def _qkv_kernel(h_ref, gkv_ref, gat_ref, wk_ref, wv_ref, wf_ref, wq_ref, bf_ref,
                gk_ref, gq_ref, bd_ref,
                k_ref, v_ref, lf_ref, kb_ref, vb_ref, qb_ref, *, n_heads, q_scale):
    base = _rms_base(h_ref[...])
    a_kv = (base * gkv_ref[...]).astype(BF16)
    a_q = (base * gat_ref[...]).astype(BF16)
    k = _head_norm(_dot(a_kv, wk_ref[...]), bd_ref, gk_ref[...])
    v = _dot(a_kv, wv_ref[...])
    z = _dot(a_kv, wf_ref[...])[:, :n_heads] + bf_ref[...]
    q = _head_norm(_dot(a_q, wq_ref[...]), bd_ref, gq_ref[...])
    k_ref[...] = k
    v_ref[...] = v
    lf_ref[...] = _log_sigmoid(z)
    kb_ref[...] = k.astype(BF16)
    vb_ref[...] = v.astype(BF16)
    qb_ref[...] = (q * q_scale).astype(BF16)


def _qkv_proj(h, g_kv, g_attn, w_kvf, b_f, g_k, w_q, g_q, n_heads):
    t, d = h.shape
    kvw = w_q.shape[1]
    hd = kvw // n_heads
    tm = min(t, 256)
    assert t % tm == 0 and MXU_DIM % hd == 0 and kvw % MXU_DIM == 0
    wk = w_kvf[:, :kvw].astype(BF16)
    wv = w_kvf[:, kvw:2 * kvw].astype(BF16)
    wf = jnp.pad(w_kvf[:, 2 * kvw:], ((0, 0), (0, LANES - n_heads))).astype(BF16)
    head = jnp.arange(MXU_DIM) // hd
    bd = ((head[:, None] == head[None, :]).astype(F32) / hd).astype(BF16)
    row = lambda i: (i, 0)
    fix = lambda i: (0, 0)
    return pl.pallas_call(
        functools.partial(_qkv_kernel, n_heads=n_heads, q_scale=hd ** -0.5),
        grid=(t // tm,),
        in_specs=[
            pl.BlockSpec((tm, d), row),
            pl.BlockSpec((1, d), fix), pl.BlockSpec((1, d), fix),
            pl.BlockSpec((d, kvw), fix), pl.BlockSpec((d, kvw), fix),
            pl.BlockSpec((d, LANES), fix), pl.BlockSpec((d, kvw), fix),
            pl.BlockSpec((1, n_heads), fix),
            pl.BlockSpec((1, kvw), fix), pl.BlockSpec((1, kvw), fix),
            pl.BlockSpec((MXU_DIM, MXU_DIM), fix),
        ],
        out_specs=[
            pl.BlockSpec((tm, kvw), row), pl.BlockSpec((tm, kvw), row),
            pl.BlockSpec((tm, n_heads), row),
            pl.BlockSpec((tm, kvw), row), pl.BlockSpec((tm, kvw), row),
            pl.BlockSpec((tm, kvw), row),
        ],
        out_shape=[
            jax.ShapeDtypeStruct((t, kvw), F32), jax.ShapeDtypeStruct((t, kvw), F32),
            jax.ShapeDtypeStruct((t, n_heads), F32),
            jax.ShapeDtypeStruct((t, kvw), BF16), jax.ShapeDtypeStruct((t, kvw), BF16),
            jax.ShapeDtypeStruct((t, kvw), BF16),
        ],
        compiler_params=_params(("parallel",), VMEM_LIMIT),
        name="qkv_proj",
    )(h, g_kv.reshape(1, d), g_attn.reshape(1, d), wk, wv, wf, w_q.astype(BF16),
      b_f.reshape(1, n_heads), jnp.tile(g_k, n_heads).reshape(1, kvw),
      jnp.tile(g_q, n_heads).reshape(1, kvw), bd)


def _cumsum_kernel(x_ref, o_ref):
    r, l = x_ref.shape
    ii = lax.broadcasted_iota(jnp.int32, (LANES, LANES), 0)
    jj = lax.broadcasted_iota(jnp.int32, (LANES, LANES), 1)
    tri = (ii <= jj).astype(BF16)
    carry = jnp.zeros((r, 1), F32)
    for c in range(0, l, LANES):
        y = carry
        for part in _split3(x_ref[:, c:c + LANES]):
            y = y + _dot(part, tri)
        o_ref[:, c:c + LANES] = y
        carry = y[:, LANES - 1:LANES]


def _cumsum_lanes(x):
    r, l = x.shape
    tr = min(r, 256)
    assert r % tr == 0 and l % LANES == 0
    return pl.pallas_call(
        _cumsum_kernel,
        grid=(r // tr,),
        in_specs=[pl.BlockSpec((tr, l), lambda i: (i, 0))],
        out_specs=pl.BlockSpec((tr, l), lambda i: (i, 0)),
        out_shape=jax.ShapeDtypeStruct((r, l), F32),
        compiler_params=_params(("parallel",)),
        name="logf_cumsum",
    )(x)


def _softmax_step(s, m_ref, l_ref, acc_ref, v, idx):
    m_prev = m_ref[idx]
    m_new = jnp.maximum(m_prev, jnp.max(s, axis=1, keepdims=True))
    alpha = jnp.exp(m_prev - m_new)
    p = jnp.exp(s - m_new)
    l_ref[idx] = alpha * l_ref[idx] + jnp.sum(p, axis=1, keepdims=True)
    acc_ref[idx] = alpha * acc_ref[idx] + _dot(p.astype(BF16), v)
    m_ref[idx] = m_new


def _attn_prompt_kernel(q_ref, cq_ref, k_ref, v_ref, ck_ref, o_ref,
                        m_ref, l_ref, acc_ref, *, tq, hd):
    j = pl.program_id(1)
    qi = pl.program_id(2)
    lane = lax.broadcasted_iota(jnp.int32, (1, LANES), 1)
    q2 = q_ref[...]
    qs = (jnp.where(lane < hd, q2, 0), jnp.where(lane >= hd, q2, 0))
    cq_all = cq_ref[...]
    hl = lax.broadcasted_iota(jnp.int32, cq_all.shape, 1)
    cqs = tuple(jnp.sum(jnp.where(hl == 2 * j + hh, cq_all, 0.0), axis=1, keepdims=True)
                for hh in range(2))
    for hh in range(2):
        m_ref[hh] = jnp.full((tq, 1), NEG_INF, F32)
        l_ref[hh] = jnp.zeros((tq, 1), F32)
        acc_ref[hh] = jnp.zeros((tq, LANES), F32)

    def tile(kt, masked):
        start = pl.multiple_of(kt * tq, tq)
        kk = k_ref[pl.ds(start, tq), :]
        vv = v_ref[pl.ds(start, tq), :]
        for hh in range(2):
            ck = ck_ref[hh:hh + 1, pl.ds(start, tq)]
            s = _dot_nt(qs[hh], kk) + (cqs[hh] - ck)
            if masked:
                r = lax.broadcasted_iota(jnp.int32, (tq, tq), 0)
                c = lax.broadcasted_iota(jnp.int32, (tq, tq), 1)
                s = jnp.where(c <= r, s, NEG_INF)
            _softmax_step(s, m_ref, l_ref, acc_ref, vv, hh)

    def body(kt, carry):
        tile(kt, False)
        return carry

    lax.fori_loop(0, qi, body, 0)
    tile(qi, True)
    o0 = acc_ref[0] / l_ref[0]
    o1 = acc_ref[1] / l_ref[1]
    o_ref[...] = jnp.where(lane < hd, o0, o1).astype(o_ref.dtype)


def _attn_prompt(q, k, v, c_col, c_row, n_heads):
    b, s, kvw = q.shape
    hd = kvw // n_heads
    assert 2 * hd == LANES
    tq = min(s, 256)
    assert s % tq == 0
    npairs = n_heads // 2
    return pl.pallas_call(
        functools.partial(_attn_prompt_kernel, tq=tq, hd=hd),
        grid=(b, npairs, s // tq),
        in_specs=[
            pl.BlockSpec((None, tq, LANES), lambda bi, j, qi: (bi, qi, j)),
            pl.BlockSpec((None, tq, n_heads), lambda bi, j, qi: (bi, qi, 0)),
            pl.BlockSpec((None, s, LANES), lambda bi, j, qi: (bi, 0, j)),
            pl.BlockSpec((None, s, LANES), lambda bi, j, qi: (bi, 0, j)),
            pl.BlockSpec((None, None, 2, s), lambda bi, j, qi: (bi, j, 0, 0)),
        ],
        out_specs=pl.BlockSpec((None, tq, LANES), lambda bi, j, qi: (bi, qi, j)),
        out_shape=jax.ShapeDtypeStruct((b, s, kvw), BF16),
        scratch_shapes=[pltpu.VMEM((2, tq, 1), F32), pltpu.VMEM((2, tq, 1), F32),
                        pltpu.VMEM((2, tq, LANES), F32)],
        compiler_params=_params(("parallel", "parallel", "arbitrary")),
        name="fox_prompt",
    )(q, c_col, k, v, c_row)


def _attn_sample_kernel(q_ref, cq_ref, kc_ref, vc_ref, ckc_ref, kn_ref, vn_ref, ckn_ref,
                        o_ref, m_ref, l_ref, acc_ref, *, nq, hd, npairs):
    kt = pl.program_id(1)
    last = pl.num_programs(1) - 1
    lane = lax.broadcasted_iota(jnp.int32, (1, LANES), 1)

    @pl.when(kt == 0)
    def _():
        m_ref[...] = jnp.full(m_ref.shape, NEG_INF, F32)
        l_ref[...] = jnp.zeros(l_ref.shape, F32)
        acc_ref[...] = jnp.zeros(acc_ref.shape, F32)

    def pair_inputs(j):
        q2 = q_ref[:, j * LANES:(j + 1) * LANES]
        qs = jnp.concatenate([jnp.where(lane < hd, q2, 0), jnp.where(lane >= hd, q2, 0)], axis=0)
        cq = jnp.concatenate([cq_ref[:, 2 * j:2 * j + 1], cq_ref[:, 2 * j + 1:2 * j + 2]], axis=0)
        return qs, cq

    def stacked_ck(ck2, n):
        return jnp.concatenate([jnp.broadcast_to(ck2[0:1], (nq, n)),
                                jnp.broadcast_to(ck2[1:2], (nq, n))], axis=0)

    for j in range(npairs):
        qs, cq = pair_inputs(j)
        kk = kc_ref[:, j * LANES:(j + 1) * LANES].astype(BF16)
        vv = vc_ref[:, j * LANES:(j + 1) * LANES].astype(BF16)
        s = _dot_nt(qs, kk) + (cq - stacked_ck(ckc_ref[j], kk.shape[0]))
        _softmax_step(s, m_ref, l_ref, acc_ref, vv, j)

    @pl.when(kt == last)
    def _():
        r = lax.broadcasted_iota(jnp.int32, (2 * nq, nq), 0)
        c = lax.broadcasted_iota(jnp.int32, (2 * nq, nq), 1)
        causal = c <= jnp.where(r >= nq, r - nq, r)
        for j in range(npairs):
            qs, cq = pair_inputs(j)
            kk = kn_ref[:, j * LANES:(j + 1) * LANES]
            vv = vn_ref[:, j * LANES:(j + 1) * LANES]
            s = _dot_nt(qs, kk) + (cq - stacked_ck(ckn_ref[j], nq))
            _softmax_step(jnp.where(causal, s, NEG_INF), m_ref, l_ref, acc_ref, vv, j)
            o = acc_ref[j] / l_ref[j]
            o_ref[:, j * LANES:(j + 1) * LANES] = jnp.where(
                lane < hd, o[:nq], o[nq:]).astype(o_ref.dtype)


def _attn_sample(q, k_new, v_new, cache_k, cache_v, cq, ck_cache, ck_new, n_heads):
    b, n, kvw = q.shape
    p = cache_k.shape[1]
    hd = kvw // n_heads
    assert 2 * hd == LANES
    tk = min(p, 1024)
    assert p % tk == 0
    npairs = n_heads // 2
    fix = lambda bi, kt: (bi, 0, 0)
    return pl.pallas_call(
        functools.partial(_attn_sample_kernel, nq=n, hd=hd, npairs=npairs),
        grid=(b, p // tk),
        in_specs=[
            pl.BlockSpec((None, n, kvw), fix),
            pl.BlockSpec((None, n, n_heads), fix),
            pl.BlockSpec((None, tk, kvw), lambda bi, kt: (bi, kt, 0)),
            pl.BlockSpec((None, tk, kvw), lambda bi, kt: (bi, kt, 0)),
            pl.BlockSpec((None, npairs, 2, tk), lambda bi, kt: (bi, 0, 0, kt)),
            pl.BlockSpec((None, n, kvw), fix),
            pl.BlockSpec((None, n, kvw), fix),
            pl.BlockSpec((None, npairs, 2, n), lambda bi, kt: (bi, 0, 0, 0)),
        ],
        out_specs=pl.BlockSpec((None, n, kvw), fix),
        out_shape=jax.ShapeDtypeStruct((b, n, kvw), BF16),
        scratch_shapes=[pltpu.VMEM((npairs, 2 * n, 1), F32),
                        pltpu.VMEM((npairs, 2 * n, 1), F32),
                        pltpu.VMEM((npairs, 2 * n, LANES), F32)],
        compiler_params=_params(("parallel", "arbitrary"), VMEM_LIMIT),
        name="fox_sample",
    )(q, cq, cache_k, cache_v, ck_cache, k_new, v_new, ck_new)


def _oproj_router_kernel(h_ref, o_ref, wo_ref, g_ref, wr_ref, h2_ref, u_ref, gates_ref, *,
                         n_experts):
    h2 = h_ref[...] + _dot(o_ref[...], wo_ref[...])
    u = _rms_base(h2) * g_ref[...]
    h2_ref[...] = h2
    u_ref[...] = u.astype(BF16)
    u_hi, u_lo = _split2(u)
    logits = _dot(u_hi, wr_ref[0]) + _dot(u_hi, wr_ref[1]) + _dot(u_lo, wr_ref[0])
    lane = lax.broadcasted_iota(jnp.int32, logits.shape, 1).astype(F32)
    logits = jnp.where(lane < n_experts, logits, NEG_INF)
    m1 = jnp.max(logits, axis=1, keepdims=True)
    i1 = jnp.min(jnp.where(logits == m1, lane, float(LANES)), axis=1, keepdims=True)
    rest = jnp.where(lane == i1, NEG_INF, logits)
    m2 = jnp.max(rest, axis=1, keepdims=True)
    i2 = jnp.min(jnp.where(rest == m2, lane, float(LANES)), axis=1, keepdims=True)
    e2 = jnp.exp(m2 - m1)
    w1 = 1.0 / (1.0 + e2)
    w2 = e2 / (1.0 + e2)
    gates = jnp.where(lane == i1, w1, 0.0) + jnp.where(lane == i2, w2, 0.0)
    gates_ref[...] = gates[:, :n_experts]


def _oproj_router(h, o, w_o, g, w_router):
    t, d = h.shape
    kvw = o.shape[1]
    ne = w_router.shape[1]
    tm = min(t, 256)
    assert t % tm == 0 and ne <= LANES
    wr = jnp.pad(w_router, ((0, 0), (0, LANES - ne)))
    wr_hi = wr.astype(BF16)
    wr_lo = (wr - wr_hi.astype(F32)).astype(BF16)
    row = lambda i: (i, 0)
    return pl.pallas_call(
        functools.partial(_oproj_router_kernel, n_experts=ne),
        grid=(t // tm,),
        in_specs=[
            pl.BlockSpec((tm, d), row), pl.BlockSpec((tm, kvw), row),
            pl.BlockSpec((kvw, d), lambda i: (0, 0)),
            pl.BlockSpec((1, d), lambda i: (0, 0)),
            pl.BlockSpec((2, d, LANES), lambda i: (0, 0, 0)),
        ],
        out_specs=[pl.BlockSpec((tm, d), row), pl.BlockSpec((tm, d), row),
                   pl.BlockSpec((tm, ne), row)],
        out_shape=[jax.ShapeDtypeStruct((t, d), F32), jax.ShapeDtypeStruct((t, d), BF16),
                   jax.ShapeDtypeStruct((t, ne), F32)],
        compiler_params=_params(("parallel",)),
        name="oproj_router",
    )(h, o, w_o.astype(BF16), g.reshape(1, d), jnp.stack([wr_hi, wr_lo]))


def _moe_kernel(h_ref, u_ref, gates_ref, wgu_ref, wd_ref, o_ref, *, f):
    e = pl.program_id(1)

    @pl.when(e == 0)
    def _():
        o_ref[...] = h_ref[...]

    y = _swiglu(u_ref[...], wgu_ref, wd_ref, f)
    gates = gates_ref[...]
    lane = lax.broadcasted_iota(jnp.int32, gates.shape, 1)
    gate = jnp.sum(jnp.where(lane == e, gates, 0.0), axis=1, keepdims=True)
    o_ref[...] += gate * y


def _moe(h, u, gates, w_gu, w_down):
    t, d = h.shape
    ne, f, _ = w_down.shape
    tm = min(t, 512)
    assert t % tm == 0
    row = lambda i, e: (i, 0)
    return pl.pallas_call(
        functools.partial(_moe_kernel, f=f),
        grid=(t // tm, ne),
        in_specs=[
            pl.BlockSpec((tm, d), row), pl.BlockSpec((tm, d), row),
            pl.BlockSpec((tm, ne), row),
            pl.BlockSpec((None, d, 2 * f), lambda i, e: (e, 0, 0)),
            pl.BlockSpec((None, f, d), lambda i, e: (e, 0, 0)),
        ],
        out_specs=pl.BlockSpec((tm, d), row),
        out_shape=jax.ShapeDtypeStruct((t, d), F32),
        compiler_params=_params(("parallel", "arbitrary"), VMEM_LIMIT),
        name="moe_experts",
    )(h, u, gates, w_gu.astype(BF16), w_down.astype(BF16))


def _pair_rows(c_t, b, n_heads):
    return c_t.reshape(b, n_heads // 2, 2, c_t.shape[-1])


def _trunk(x, pool_hist, kv_cache, p):
    b, n, d = x.shape
    n_heads = p["b_f"].shape[0]
    start_pos = 0 if kv_cache is None else kv_cache[0].shape[1]

    h, pool_state = _pool_layer(x, None if pool_hist is None else pool_hist[0],
                                p["g_pool_norm"][0], p["w_pool"][0], p["pool_scale"][0],
                                start_pos)
    h = _dense_ffn(h.reshape(b * n, d), p["g_ffn_norm"][0], p["w_ffn_gu"][0],
                   p["w_ffn_down"][0])

    k, v, logf, kb, vb, qb = _qkv_proj(h, p["g_kv_norm"], p["g_attn_norm"][0], p["w_kvf"],
                                       p["b_f"], p["g_k"], p["w_q"][0], p["g_q"][0], n_heads)
    kvw = k.shape[1]
    logf = logf.reshape(b, n, n_heads)
    qb, kb, vb = (a.reshape(b, n, kvw) for a in (qb, kb, vb))

    if kv_cache is None:
        lf_t = jnp.transpose(logf, (0, 2, 1)).reshape(b * n_heads, n)
        c_t = _cumsum_lanes(lf_t)
        c_col = jnp.transpose(c_t.reshape(b, n_heads, n), (0, 2, 1))
        o = _attn_prompt(qb, kb, vb, c_col, _pair_rows(c_t, b, n_heads), n_heads)
    else:
        cache_k, cache_v, cache_logf = kv_cache
        past = cache_k.shape[1]
        lf_all = jnp.concatenate([cache_logf.astype(F32), logf], axis=1)
        total = past + n
        padded = -(-total // LANES) * LANES
        lf_t = jnp.transpose(lf_all, (0, 2, 1)).reshape(b * n_heads, total)
        c_t = _cumsum_lanes(jnp.pad(lf_t, ((0, 0), (0, padded - total))))
        c_new = c_t[:, past:total]
        cq = jnp.transpose(c_new.reshape(b, n_heads, n), (0, 2, 1))
        o = _attn_sample(qb, kb, vb, cache_k.reshape(b, past, kvw), cache_v.reshape(b, past, kvw),
                         cq, _pair_rows(c_t[:, :past], b, n_heads),
                         _pair_rows(c_new, b, n_heads), n_heads)

    h2, u, gates = _oproj_router(h, o.reshape(b * n, kvw), p["w_o"][0], p["g_ffn_norm"][1],
                                 p["w_router"][0])
    y = _moe(h2, u, gates, p["w_moe_gu"][0], p["w_moe_down"][0])

    hd = kvw // n_heads
    return (y.reshape(b, n, d), pool_state[None],
            k.reshape(b, n, n_heads, hd), v.reshape(b, n, n_heads, hd), logf)


def kernel(x_prompt, x_sample, state_pool, cache_k, cache_v, cache_logf, g_pool_norm, w_pool,
           pool_scale, g_kv_norm, w_kvf, b_f, g_k, g_attn_norm, w_q, g_q, w_o, g_ffn_norm,
           w_ffn_gu, w_ffn_down, w_router, w_moe_gu, w_moe_down):
    p = dict(g_pool_norm=g_pool_norm, w_pool=w_pool, pool_scale=pool_scale,
             g_kv_norm=g_kv_norm, w_kvf=w_kvf, b_f=b_f, g_k=g_k, g_attn_norm=g_attn_norm,
             w_q=w_q, g_q=g_q, w_o=w_o, g_ffn_norm=g_ffn_norm, w_ffn_gu=w_ffn_gu,
             w_ffn_down=w_ffn_down, w_router=w_router, w_moe_gu=w_moe_gu,
             w_moe_down=w_moe_down)
    assert w_pool.shape[0] == 1 and w_q.shape[0] == 1 and w_router.shape[0] == 1
    y_p, pool_p, k_p, v_p, lf_p = _trunk(x_prompt, None, None, p)
    y_s, pool_s, k_s, v_s, lf_s = _trunk(
        x_sample, state_pool, (cache_k, cache_v, cache_logf), p)
    return (y_p, y_s, pool_p, k_p, v_p, lf_p, pool_s, k_s, v_s, lf_s)
```

```python
import functools
import math

import jax
import jax.numpy as jnp
from jax import lax
from jax.experimental import pallas as pl
from jax.experimental.pallas import tpu as pltpu

EPS = 1e-6
POOL_WINDOWS = (2, 4, 8, 16)
HALO = max(POOL_WINDOWS)
POOL_HIST = HALO - 1
TOP_K = 2
LANES = 128
MXU_DIM = 256
VMEM_LIMIT = 56 * 1024 * 1024
NEG_INF = float("-inf")

BF16 = jnp.bfloat16
F32 = jnp.float32


def _params(semantics, vmem=None):
    return pltpu.CompilerParams(dimension_semantics=semantics, vmem_limit_bytes=vmem)


def _dot(a, b):
    return jnp.dot(a, b, preferred_element_type=F32)


def _dot_nt(a, b):
    return lax.dot_general(a, b, (((1,), (1,)), ((), ())), preferred_element_type=F32)


def _rms_base(x):
    return x * lax.rsqrt(jnp.mean(x * x, axis=-1, keepdims=True) + EPS)


def _split2(x):
    hi = x.astype(BF16)
    lo = (x - hi.astype(F32)).astype(BF16)
    return hi, lo


def _split3(x):
    hi = x.astype(BF16)
    r = x - hi.astype(F32)
    mid = r.astype(BF16)
    lo = (r - mid.astype(F32)).astype(BF16)
    return hi, mid, lo


def _ffn_chunks(f):
    out, s = [], 0
    while s < f:
        n = min(4 * MXU_DIM, f - s)
        out.append((s, n))
        s += n
    return out


def _pool_kernel(x_ref, halo_ref, g_ref, w_ref, scale_ref, h_ref, st_ref, *,
                 tm, start_pos, halo_is_normed):
    i = pl.program_id(1)
    x = x_ref[...]
    g = g_ref[...]
    u = _rms_base(x) * g
    if halo_is_normed:
        uh = halo_ref[...]
    else:
        uh = _rms_base(halo_ref[...]) * g
        uh = jnp.where(i > 0, uh, 0.0)
    ext = jnp.concatenate([uh, u], axis=0)

    row = lax.broadcasted_iota(jnp.int32, (tm, 1), 0)
    avail = start_pos + i * tm + row + 1
    group = x.shape[1] // len(POOL_WINDOWS)
    ys = []
    for gi, w in enumerate(POOL_WINDOWS):
        lo, hi = gi * group, (gi + 1) * group
        s = ext[:, lo:hi]
        step = 1
        while step < w:
            s = s + pltpu.roll(s, step, 0)
            step *= 2
        inv_cnt = 1.0 / jnp.minimum(avail, w).astype(F32)
        pooled = s[HALO:, :] * inv_cnt
        diff = pooled - u[:, lo:hi]
        ys.append(_dot(diff.astype(BF16), w_ref[gi]))
    y = jnp.concatenate(ys, axis=1) * scale_ref[...]
    h_ref[...] = x + y

    @pl.when(i == pl.num_programs(1) - 1)
    def _():
        st_ref[...] = u[tm - HALO:, :]


def _pool_layer(x, hist, g, w_pool, pool_scale, start_pos):
    b, n, d = x.shape
    tm = min(n, 256)
    assert n % tm == 0 and tm % HALO == 0 and n >= HALO
    nt = n // tm
    if hist is None:
        halo_arr = x
        halo_spec = pl.BlockSpec(
            (None, HALO, d), lambda bi, i: (bi, jnp.maximum(i * (tm // HALO) - 1, 0), 0))
    else:
        assert nt == 1
        halo_arr = jnp.pad(hist, ((0, 0), (HALO - POOL_HIST, 0), (0, 0)))
        halo_spec = pl.BlockSpec((None, HALO, d), lambda bi, i: (bi, 0, 0))
    ng = len(POOL_WINDOWS)
    h, st = pl.pallas_call(
        functools.partial(_pool_kernel, tm=tm, start_pos=start_pos,
                          halo_is_normed=hist is not None),
        grid=(b, nt),
        in_specs=[
            pl.BlockSpec((None, tm, d), lambda bi, i: (bi, i, 0)),
            halo_spec,
            pl.BlockSpec((1, d), lambda bi, i: (0, 0)),
            pl.BlockSpec((ng, d // ng, d // ng), lambda bi, i: (0, 0, 0)),
            pl.BlockSpec((1, d), lambda bi, i: (0, 0)),
        ],
        out_specs=[
            pl.BlockSpec((None, tm, d), lambda bi, i: (bi, i, 0)),
            pl.BlockSpec((None, HALO, d), lambda bi, i: (bi, 0, 0)),
        ],
        out_shape=[jax.ShapeDtypeStruct((b, n, d), F32),
                   jax.ShapeDtypeStruct((b, HALO, d), F32)],
        compiler_params=_params(("parallel", "arbitrary")),
        name="pool_mixer",
    )(x, halo_arr, g.reshape(1, d), w_pool.astype(BF16), pool_scale.reshape(1, d))
    return h, st[:, HALO - POOL_HIST:, :]


def _swiglu(xb, wgu_ref, wd_ref, f):
    acc = None
    for s, n in _ffn_chunks(f):
        gate = _dot(xb, wgu_ref[:, s:s + n])
        up = _dot(xb, wgu_ref[:, f + s:f + s + n])
        act = (gate * jax.nn.sigmoid(gate)) * up
        part = _dot(act.astype(BF16), wd_ref[s:s + n, :])
        acc = part if acc is None else acc + part
    return acc


def _dense_ffn_kernel(h_ref, g_ref, wgu_ref, wd_ref, o_ref, *, f):
    h = h_ref[...]
    u = (_rms_base(h) * g_ref[...]).astype(BF16)
    o_ref[...] = h + _swiglu(u, wgu_ref, wd_ref, f)


def _dense_ffn(h, g, w_gu, w_down):
    t, d = h.shape
    f = w_down.shape[0]
    tm = min(t, 512)
    assert t % tm == 0
    return pl.pallas_call(
        functools.partial(_dense_ffn_kernel, f=f),
        grid=(t // tm,),
        in_specs=[
            pl.BlockSpec((tm, d), lambda i: (i, 0)),
            pl.BlockSpec((1, d), lambda i: (0, 0)),
            pl.BlockSpec((d, 2 * f), lambda i: (0, 0)),
            pl.BlockSpec((f, d), lambda i: (0, 0)),
        ],
        out_specs=pl.BlockSpec((tm, d), lambda i: (i, 0)),
        out_shape=jax.ShapeDtypeStruct((t, d), F32),
        compiler_params=_params(("parallel",), VMEM_LIMIT),
        name="dense_ffn",
    )(h, g.reshape(1, d), w_gu.astype(BF16), w_down.astype(BF16))


def _head_norm(x, bd_ref, g):
    sq_hi, sq_lo = _split2(x * x)
    n = bd_ref.shape[0]
    ms = jnp.concatenate(
        [_dot(sq_hi[:, c:c + n], bd_ref[...]) + _dot(sq_lo[:, c:c + n], bd_ref[...])
         for c in range(0, x.shape[1], n)], axis=1)
    return x * lax.rsqrt(ms + EPS) * g


def _log_sigmoid(z):
    return jnp.minimum(z, 0.0) - jnp.log1p(jnp.exp(-jnp.abs(z)))


def _qkv_kernel(h_ref, gkv_ref, gat_ref, wk_ref, wv_ref, wf_ref, wq_ref, bf_ref,
                gk_ref, gq_ref, bd_ref,
                k_ref, v_ref, lf_ref, kb_ref, vb_ref, qb_ref, *, n_heads, q_scale):
    base = _rms_base(h_ref[...])
    a_kv = (base * gkv_ref[...]).astype(BF16)
    a_q = (base * gat_ref[...]).astype(BF16)
    k = _head_norm(_dot(a_kv, wk_ref[...]), bd_ref, gk_ref[...])
    v = _dot(a_kv, wv_ref[...])
    z = _dot(a_kv, wf_ref[...])[:, :n_heads] + bf_ref[...]
    q = _head_norm(_dot(a_q, wq_ref[...]), bd_ref, gq_ref[...])
    k_ref[...] = k
    v_ref[...] = v
    lf_ref[...] = _log_sigmoid(z)
    kb_ref[...] = k.astype(BF16)
    vb_ref[...] = v.astype(BF16)
    qb_ref[...] = (q * q_scale).astype(BF16)


def _qkv_proj(h, g_kv, g_attn, w_kvf, b_f, g_k, w_q, g_q, n_heads):
    t, d = h.shape
    kvw = w_q.shape[1]
    hd = kvw // n_heads
    tm = min(t, 256)
    assert t % tm == 0 and MXU_DIM % hd == 0 and kvw % MXU_DIM == 0
    wk = w_kvf[:, :kvw].astype(BF16)
    wv = w_kvf[:, kvw:2 * kvw].astype(BF16)
    wf = jnp.pad(w_kvf[:, 2 * kvw:], ((0, 0), (0, LANES - n_heads))).astype(BF16)
    head = jnp.arange(MXU_DIM) // hd
    bd = ((head[:, None] == head[None, :]).astype(F32) / hd).astype(BF16)
    row = lambda i: (i, 0)
    fix = lambda i: (0, 0)
    return pl.pallas_call(
        functools.partial(_qkv_kernel, n_heads=n_heads, q_scale=hd ** -0.5),
        grid=(t // tm,),
        in_specs=[
            pl.BlockSpec((tm, d), row),
            pl.BlockSpec((1, d), fix), pl.BlockSpec((1, d), fix),
            pl.BlockSpec((d, kvw), fix), pl.BlockSpec((d, kvw), fix),
            pl.BlockSpec((d, LANES), fix), pl.BlockSpec((d, kvw), fix),
            pl.BlockSpec((1, n_heads), fix),
            pl.BlockSpec((1, kvw), fix), pl.BlockSpec((1, kvw), fix),
            pl.BlockSpec((MXU_DIM, MXU_DIM), fix),
        ],
        out_specs=[
            pl.BlockSpec((tm, kvw), row), pl.BlockSpec((tm, kvw), row),
            pl.BlockSpec((tm, n_heads), row),
            pl.BlockSpec((tm, kvw), row), pl.BlockSpec((tm, kvw), row),
            pl.BlockSpec((tm, kvw), row),
        ],
        out_shape=[
            jax.ShapeDtypeStruct((t, kvw), F32), jax.ShapeDtypeStruct((t, kvw), F32),
            jax.ShapeDtypeStruct((t, n_heads), F32),
            jax.ShapeDtypeStruct((t, kvw), BF16), jax.ShapeDtypeStruct((t, kvw), BF16),
            jax.ShapeDtypeStruct((t, kvw), BF16),
        ],
        compiler_params=_params(("parallel",), VMEM_LIMIT),
        name="qkv_proj",
    )(h, g_kv.reshape(1, d), g_attn.reshape(1, d), wk, wv, wf, w_q.astype(BF16),
      b_f.reshape(1, n_heads), jnp.tile(g_k, n_heads).reshape(1, kvw),
      jnp.tile(g_q, n_heads).reshape(1, kvw), bd)


def _cumsum_tile(x, tri, carry):
    y = carry
    for part in _split3(x):
        y = y + _dot(part, tri)
    return y


def _qkv_t_kernel(h_ref, gkv_ref, gat_ref, wkt_ref, wvt_ref, wft_ref, wq_ref, bf_ref,
                  gk_ref, gq_ref, bd_ref, tri_ref,
                  kt_ref, vt_ref, lft_ref, ct_ref, ktb_ref, vtb_ref, qb_ref, carry_ref, *,
                  n_heads, q_scale):
    @pl.when(pl.program_id(1) == 0)
    def _():
        carry_ref[...] = jnp.zeros(carry_ref.shape, F32)

    base = _rms_base(h_ref[...])
    a_kv = (base * gkv_ref[...]).astype(BF16)
    a_q = (base * gat_ref[...]).astype(BF16)
    tm = a_kv.shape[0]
    kraw = _dot_nt(wkt_ref[...], a_kv)
    k3 = kraw.reshape(n_heads, kraw.shape[0] // n_heads, tm)
    ms = jnp.mean(k3 * k3, axis=1, keepdims=True)
    kt = (k3 * lax.rsqrt(ms + EPS) * gk_ref[...][None]).reshape(kraw.shape)
    vt = _dot_nt(wvt_ref[...], a_kv)
    lft = _log_sigmoid(_dot_nt(wft_ref[...], a_kv) + bf_ref[...])
    ct = _cumsum_tile(lft, tri_ref[...], carry_ref[...])
    carry_ref[...] = ct[:, tm - 1:tm]
    q = _head_norm(_dot(a_q, wq_ref[...]), bd_ref, gq_ref[...])
    kt_ref[...] = kt
    vt_ref[...] = vt
    lft_ref[...] = lft
    ct_ref[...] = ct
    ktb_ref[...] = kt.astype(BF16)
    vtb_ref[...] = vt.astype(BF16)
    qb_ref[...] = (q * q_scale).astype(BF16)


def _qkv_proj_t(h, b, g_kv, g_attn, w_kvf, b_f, g_k, w_q, g_q, n_heads):
    t, d = h.shape
    s = t // b
    kvw = w_q.shape[1]
    hd = kvw // n_heads
    tm = min(s, 256)
    nt = s // tm
    assert s % tm == 0 and MXU_DIM % hd == 0 and kvw % MXU_DIM == 0
    wkt = w_kvf[:, :kvw].T.astype(BF16)
    wvt = w_kvf[:, kvw:2 * kvw].T.astype(BF16)
    wft = w_kvf[:, 2 * kvw:].T.astype(BF16)
    head = jnp.arange(MXU_DIM) // hd
    bd = ((head[:, None] == head[None, :]).astype(F32) / hd).astype(BF16)
    pos = jnp.arange(tm)
    tri = (pos[:, None] <= pos[None, :]).astype(BF16)
    fix = lambda bi, i: (0, 0)
    feat = lambda rows: pl.BlockSpec((None, rows, tm), lambda bi, i: (bi, 0, i))
    f32 = lambda rows: jax.ShapeDtypeStruct((b, rows, s), F32)
    return pl.pallas_call(
        functools.partial(_qkv_t_kernel, n_heads=n_heads,
                          q_scale=hd ** -0.5 * math.log2(math.e)),
        grid=(b, nt),
        in_specs=[
            pl.BlockSpec((tm, d), lambda bi, i: (bi * nt + i, 0)),
            pl.BlockSpec((1, d), fix), pl.BlockSpec((1, d), fix),
            pl.BlockSpec((kvw, d), fix), pl.BlockSpec((kvw, d), fix),
            pl.BlockSpec((n_heads, d), fix), pl.BlockSpec((d, kvw), fix),
            pl.BlockSpec((n_heads, tm), fix), pl.BlockSpec((hd, tm), fix),
            pl.BlockSpec((1, kvw), fix), pl.BlockSpec((MXU_DIM, MXU_DIM), fix),
            pl.BlockSpec((tm, tm), fix),
        ],
        out_specs=[feat(kvw), feat(kvw), feat(n_heads), feat(n_heads), feat(kvw), feat(kvw),
                   pl.BlockSpec((tm, kvw), lambda bi, i: (bi * nt + i, 0))],
        out_shape=[f32(kvw), f32(kvw), f32(n_heads), f32(n_heads),
                   jax.ShapeDtypeStruct((b, kvw, s), BF16),
                   jax.ShapeDtypeStruct((b, kvw, s), BF16),
                   jax.ShapeDtypeStruct((t, kvw), BF16)],
        scratch_shapes=[pltpu.VMEM((n_heads, 1), F32)],
        compiler_params=_params(("parallel", "arbitrary"), VMEM_LIMIT),
        name="qkv_proj_t",
    )(h, g_kv.reshape(1, d), g_attn.reshape(1, d), wkt, wvt, wft, w_q.astype(BF16),
      jnp.broadcast_to(b_f[:, None], (n_heads, tm)), jnp.broadcast_to(g_k[:, None], (hd, tm)),
      jnp.tile(g_q, n_heads).reshape(1, kvw), bd, tri)


def _cumsum_kernel(x_ref, o_ref):
    r, l = x_ref.shape
    ii = lax.broadcasted_iota(jnp.int32, (LANES, LANES), 0)
    jj = lax.broadcasted_iota(jnp.int32, (LANES, LANES), 1)
    tri = (ii <= jj).astype(BF16)
    carry = jnp.zeros((r, 1), F32)
    for c in range(0, l, LANES):
        y = _cumsum_tile(x_ref[:, c:c + LANES], tri, carry)
        o_ref[:, c:c + LANES] = y
        carry = y[:, LANES - 1:LANES]


def _cumsum_lanes(x):
    r, l = x.shape
    tr = min(r, 256)
    assert r % tr == 0 and l % LANES == 0
    return pl.pallas_call(
        _cumsum_kernel,
        grid=(r // tr,),
        in_specs=[pl.BlockSpec((tr, l), lambda i: (i, 0))],
        out_specs=pl.BlockSpec((tr, l), lambda i: (i, 0)),
        out_shape=jax.ShapeDtypeStruct((r, l), F32),
        compiler_params=_params(("parallel",)),
        name="logf_cumsum",
    )(x)


def _softmax_step(s, m_ref, l_ref, acc_ref, v, idx, v_feature_major):
    m_prev = m_ref[idx]
    m_new = jnp.maximum(m_prev, jnp.max(s, axis=1, keepdims=True))
    alpha = jnp.exp(m_prev - m_new)
    p = jnp.exp(s - m_new)
    pb = p.astype(BF16)
    pv = _dot_nt(pb, v) if v_feature_major else _dot(pb, v)
    l_ref[idx] = alpha * l_ref[idx] + jnp.sum(p, axis=1, keepdims=True)
    acc_ref[idx] = alpha * acc_ref[idx] + pv
    m_ref[idx] = m_new


def _attn_prompt_kernel(q_ref, c_ref, kt_ref, vt_ref, ct_ref, o_ref,
                        kaug_ref, m_ref, l_ref, acc_ref, *, tq, hd):
    j = pl.program_id(1)
    qi = pl.program_id(2)
    log2e = math.log2(math.e)
    s_len = kt_ref.shape[1]
    n_ext = 16

    @pl.when(qi == 0)
    def _():
        row = lax.broadcasted_iota(jnp.int32, (n_ext, s_len), 0)
        for hh in range(2):
            hi, mid, lo = (p.astype(F32) for p in _split3(ct_ref[hh:hh + 1, :] * log2e))
            ext = jnp.where(row < 3, 1.0,
                            jnp.where(row == 3, -hi,
                                      jnp.where(row == 4, -mid,
                                                jnp.where(row == 5, -lo, 0.0))))
            kaug_ref[hh, 0:hd, :] = kt_ref[hh * hd:(hh + 1) * hd, :]
            kaug_ref[hh, hd:hd + n_ext, :] = ext.astype(BF16)
            kaug_ref[hh, hd + n_ext:, :] = jnp.zeros((LANES - hd - n_ext, s_len), BF16)

    lane = lax.broadcasted_iota(jnp.int32, (1, LANES), 1)
    qf = q_ref[...].astype(F32)
    c_all = c_ref[...]
    hl = lax.broadcasted_iota(jnp.int32, c_all.shape, 1)
    q_aug = []
    for hh in range(2):
        cq = jnp.sum(jnp.where(hl == 2 * j + hh, c_all, 0.0), axis=1, keepdims=True) * log2e
        hi, mid, lo = (p.astype(F32) for p in _split3(cq))
        ext = jnp.where(lane == hd, hi,
                        jnp.where(lane == hd + 1, mid,
                                  jnp.where(lane == hd + 2, lo,
                                            jnp.where(lane < hd + 6, 1.0, 0.0))))
        qh = qf if hh == 0 else pltpu.roll(qf, hd, 1)
        q_aug.append(jnp.where(lane < hd, qh, ext).astype(BF16))
        m_ref[hh] = jnp.full((tq, LANES), NEG_INF, F32)
        l_ref[hh] = jnp.zeros((tq, LANES), F32)
        acc_ref[hh] = jnp.zeros((tq, LANES), F32)

    def tile(start, width, masked):
        vv = vt_ref[:, pl.ds(start, width)]
        for hh in range(2):
            s = _dot(q_aug[hh], kaug_ref[hh, :, pl.ds(start, width)])
            if masked:
                r = lax.broadcasted_iota(jnp.int32, (tq, width), 0)
                c = lax.broadcasted_iota(jnp.int32, (tq, width), 1)
                s = jnp.where(c <= r, s, NEG_INF)
            m_prev = m_ref[hh]
            m_new = jnp.maximum(m_prev, jnp.max(s, axis=1, keepdims=True))
            alpha = jnp.exp2(m_prev - m_new)
            p = jnp.exp2(s - jnp.concatenate([m_new] * (width // LANES), axis=1))
            l_ref[hh] = alpha * l_ref[hh] + jnp.sum(p, axis=1, keepdims=True)
            acc_ref[hh] = alpha * acc_ref[hh] + _dot_nt(p.astype(BF16), vv)
            m_ref[hh] = m_new

    wide = 2 * tq

    def body(kt, carry):
        tile(pl.multiple_of(kt * wide, wide), wide, False)
        return carry

    n_wide = lax.shift_right_logical(qi, 1)
    lax.fori_loop(0, n_wide, body, 0)

    @pl.when(lax.rem(qi, 2) == 1)
    def _():
        tile(pl.multiple_of(n_wide * wide, wide), tq, False)

    tile(pl.multiple_of(qi * tq, tq), tq, True)
    o0 = acc_ref[0] / l_ref[0]
    o1 = acc_ref[1] / l_ref[1]
    o_ref[...] = jnp.where(lane < hd, o0, o1).astype(o_ref.dtype)


def _attn_prompt(q, kt, vt, c_col, c_row, n_heads):
    b, s, kvw = q.shape
    hd = kvw // n_heads
    assert 2 * hd == LANES
    tq = min(s, 256)
    assert s % tq == 0
    npairs = n_heads // 2
    return pl.pallas_call(
        functools.partial(_attn_prompt_kernel, tq=tq, hd=hd),
        grid=(b, npairs, s // tq),
        in_specs=[
            pl.BlockSpec((None, tq, LANES), lambda bi, j, qi: (bi, qi, j)),
            pl.BlockSpec((None, tq, n_heads), lambda bi, j, qi: (bi, qi, 0)),
            pl.BlockSpec((None, LANES, s), lambda bi, j, qi: (bi, j, 0)),
            pl.BlockSpec((None, LANES, s), lambda bi, j, qi: (bi, j, 0)),
            pl.BlockSpec((None, None, 2, s), lambda bi, j, qi: (bi, j, 0, 0)),
        ],
        out_specs=pl.BlockSpec((None, tq, LANES), lambda bi, j, qi: (bi, qi, j)),
        out_shape=jax.ShapeDtypeStruct((b, s, kvw), BF16),
        scratch_shapes=[pltpu.VMEM((2, LANES, s), BF16), pltpu.VMEM((2, tq, LANES), F32),
                        pltpu.VMEM((2, tq, LANES), F32), pltpu.VMEM((2, tq, LANES), F32)],
        compiler_params=_params(("parallel", "parallel", "arbitrary")),
        name="fox_prompt",
    )(q, c_col, kt, vt, c_row)


def _attn_sample_kernel(q_ref, cq_ref, kc_ref, vc_ref, ckc_ref, kn_ref, vn_ref, ckn_ref,
                        o_ref, m_ref, l_ref, acc_ref, *, nq, hd, n_heads):
    kt = pl.program_id(1)
    last = pl.num_programs(1) - 1

    @pl.when(kt == 0)
    def _():
        m_ref[...] = jnp.full(m_ref.shape, NEG_INF, F32)
        l_ref[...] = jnp.zeros(l_ref.shape, F32)
        acc_ref[...] = jnp.zeros(acc_ref.shape, F32)

    for h in range(n_heads):
        qh = q_ref[:, h * hd:(h + 1) * hd]
        kk = kc_ref[h * hd:(h + 1) * hd, :].astype(BF16)
        vv = vc_ref[h * hd:(h + 1) * hd, :].astype(BF16)
        s = _dot(qh, kk) + (cq_ref[:, h:h + 1] - ckc_ref[h:h + 1, :])
        _softmax_step(s, m_ref, l_ref, acc_ref, vv, h, True)

    @pl.when(kt == last)
    def _():
        r = lax.broadcasted_iota(jnp.int32, (nq, nq), 0)
        c = lax.broadcasted_iota(jnp.int32, (nq, nq), 1)
        for h in range(n_heads):
            cols = slice(h * hd, (h + 1) * hd)
            s = _dot_nt(q_ref[:, cols], kn_ref[:, cols]) + (
                cq_ref[:, h:h + 1] - ckn_ref[h:h + 1, :])
            _softmax_step(jnp.where(c <= r, s, NEG_INF), m_ref, l_ref, acc_ref,
                          vn_ref[:, cols], h, False)
            o_ref[:, cols] = (acc_ref[h] / l_ref[h]).astype(o_ref.dtype)


def _attn_sample(q, k_new, v_new, cache_kt, cache_vt, cq, ck_cache, ck_new, n_heads):
    b, n, kvw = q.shape
    p = cache_kt.shape[2]
    hd = kvw // n_heads
    tk = min(p, 512)
    assert p % tk == 0
    fix = lambda bi, kt: (bi, 0, 0)
    cache = pl.BlockSpec((None, kvw, tk), lambda bi, kt: (bi, 0, kt))
    return pl.pallas_call(
        functools.partial(_attn_sample_kernel, nq=n, hd=hd, n_heads=n_heads),
        grid=(b, p // tk),
        in_specs=[
            pl.BlockSpec((None, n, kvw), fix),
            pl.BlockSpec((None, n, n_heads), fix),
            cache, cache,
            pl.BlockSpec((None, n_heads, tk), lambda bi, kt: (bi, 0, kt)),
            pl.BlockSpec((None, n, kvw), fix),
            pl.BlockSpec((None, n, kvw), fix),
            pl.BlockSpec((None, n_heads, n), fix),
        ],
        out_specs=pl.BlockSpec((None, n, kvw), fix),
        out_shape=jax.ShapeDtypeStruct((b, n, kvw), BF16),
        scratch_shapes=[pltpu.VMEM((n_heads, n, 1), F32),
                        pltpu.VMEM((n_heads, n, 1), F32),
                        pltpu.VMEM((n_heads, n, hd), F32)],
        compiler_params=_params(("parallel", "arbitrary"), VMEM_LIMIT),
        name="fox_sample",
    )(q, cq, cache_kt, cache_vt, ck_cache, k_new, v_new, ck_new)


def _store_token_tiles(ref, x):
    m, d = x.shape
    nsub = d // LANES
    for s in range(nsub):
        ref[pl.ds(s, m, stride=nsub), :] = x[:, s * LANES:(s + 1) * LANES]


def _load_token_tiles(ref, m, nsub, first=0, stride=None):
    stride = nsub if stride is None else stride
    return jnp.concatenate(
        [ref[pl.ds(first + s, m, stride=stride), :] for s in range(nsub)], axis=1)


def _oproj_router_kernel(h_ref, o_ref, wo_ref, g_ref, wr_ref, h2_ref, u_ref, idx_ref, w_ref, *,
                         n_experts):
    h2 = h_ref[...] + _dot(o_ref[...], wo_ref[...])
    u = _rms_base(h2) * g_ref[...]
    h2_ref[...] = h2
    _store_token_tiles(u_ref, u)
    u_hi, u_lo = _split2(u)
    logits = _dot(u_hi, wr_ref[0]) + _dot(u_hi, wr_ref[1]) + _dot(u_lo, wr_ref[0])
    lane = lax.broadcasted_iota(jnp.int32, logits.shape, 1).astype(F32)
    logits = jnp.where(lane < n_experts, logits, NEG_INF)
    m1 = jnp.max(logits, axis=1, keepdims=True)
    i1 = jnp.min(jnp.where(logits == m1, lane, float(LANES)), axis=1, keepdims=True)
    rest = jnp.where(lane == i1, NEG_INF, logits)
    m2 = jnp.max(rest, axis=1, keepdims=True)
    i2 = jnp.min(jnp.where(rest == m2, lane, float(LANES)), axis=1, keepdims=True)
    e2 = jnp.exp(m2 - m1)
    w1 = 1.0 / (1.0 + e2)
    w2 = e2 / (1.0 + e2)
    idx_ref[...] = jnp.concatenate([i1, i2], axis=1).astype(jnp.int32)
    w_ref[...] = jnp.concatenate([w1, w2], axis=1)


def _oproj_router(h, o, w_o, g, w_router):
    t, d = h.shape
    kvw = o.shape[1]
    ne = w_router.shape[1]
    tm = min(t, 256)
    nsub = d // LANES
    assert t % tm == 0 and ne <= LANES
    wr = jnp.pad(w_router, ((0, 0), (0, LANES - ne)))
    wr_hi = wr.astype(BF16)
    wr_lo = (wr - wr_hi.astype(F32)).astype(BF16)
    row = lambda i: (i, 0)
    return pl.pallas_call(
        functools.partial(_oproj_router_kernel, n_experts=ne),
        grid=(t // tm,),
        in_specs=[
            pl.BlockSpec((tm, d), row), pl.BlockSpec((tm, kvw), row),
            pl.BlockSpec((kvw, d), lambda i: (0, 0)),
            pl.BlockSpec((1, d), lambda i: (0, 0)),
            pl.BlockSpec((2, d, LANES), lambda i: (0, 0, 0)),
        ],
        out_specs=[pl.BlockSpec((tm, d), row), pl.BlockSpec((tm * nsub, LANES), row),
                   pl.BlockSpec((tm, TOP_K), row), pl.BlockSpec((tm, TOP_K), row)],
        out_shape=[jax.ShapeDtypeStruct((t, d), F32),
                   jax.ShapeDtypeStruct((t * nsub, LANES), F32),
                   jax.ShapeDtypeStruct((t, TOP_K), jnp.int32),
                   jax.ShapeDtypeStruct((t, TOP_K), F32)],
        compiler_params=_params(("parallel",)),
        name="oproj_router",
    )(h, o, w_o.astype(BF16), g.reshape(1, d), jnp.stack([wr_hi, wr_lo]))


MOE_ROWS = 512
DMA_UNROLL = 8


def _route_plan(idx, ne, tmg):
    t = idx.shape[0]
    npairs = t * TOP_K
    e_flat = idx.reshape(npairs)
    onehot = (e_flat[:, None] == jnp.arange(ne, dtype=jnp.int32)[None, :]).astype(jnp.int32)
    csum = jnp.cumsum(onehot, axis=0)
    cnt = csum[-1]
    padded = (cnt + tmg - 1) // tmg * tmg
    ends = jnp.cumsum(padded)
    off = ends - padded
    dest = jnp.sum(onehot * (off[None, :] + csum), axis=1) - 1
    total = ends[-1]
    k = jnp.arange(tmg, dtype=jnp.int32)[None, :]
    valid = (k < (padded - cnt)[:, None]).reshape(-1)
    pad_row = ((off + cnt)[:, None] + k).reshape(-1)
    tail_row = total + jnp.cumsum(jnp.logical_not(valid).astype(jnp.int32)) - 1
    fill = jnp.where(valid, pad_row, tail_row)
    n_tiles = total // tmg
    nt_max = (npairs + ne * tmg) // tmg
    tile = jnp.minimum(jnp.arange(nt_max, dtype=jnp.int32), n_tiles - 1)
    tile_expert = jnp.sum((tile[:, None] >= (ends // tmg)[None, :]).astype(jnp.int32), axis=1)
    return dest, fill, tile_expert, n_tiles.reshape(1)


def _dispatch_kernel(dest_ref, fill_ref, u_hbm, zero_hbm, xg_hbm, sem, *, ch, pair_steps):
    i = pl.program_id(0)
    slot = lax.rem(i, 2)

    def issue(src_row, dst_rows):
        def body(j, carry):
            for jj in range(DMA_UNROLL):
                r = j * DMA_UNROLL + jj
                pltpu.make_async_copy(src_row(r), xg_hbm.at[dst_rows[0, r]], sem.at[slot]).start()
            return carry
        lax.fori_loop(0, ch // DMA_UNROLL, body, 0)

    def drain(s):
        pltpu.make_async_copy(xg_hbm.at[pl.ds(0, ch)], xg_hbm.at[pl.ds(0, ch)], sem.at[s]).wait()

    @pl.when(i < pair_steps)
    def _():
        issue(lambda r: u_hbm.at[lax.shift_right_logical(i * ch + r, 1)], dest_ref)

    @pl.when(i >= pair_steps)
    def _():
        issue(lambda r: zero_hbm.at[0], fill_ref)

    @pl.when(i > 0)
    def _():
        drain(1 - slot)

    @pl.when(i == pl.num_programs(0) - 1)
    def _():
        drain(slot)


def _dispatch(u3, dest, fill, n_rows):
    t, nsub, _ = u3.shape
    npairs = dest.shape[0]
    assert TOP_K == 2 and npairs == t * TOP_K
    ch = min(MOE_ROWS, npairs)
    assert npairs % ch == 0 and fill.shape[0] % ch == 0 and ch % DMA_UNROLL == 0
    pair_steps = npairs // ch
    fill_steps = fill.shape[0] // ch
    smem = pltpu.SMEM
    return pl.pallas_call(
        functools.partial(_dispatch_kernel, ch=ch, pair_steps=pair_steps),
        grid=(pair_steps + fill_steps,),
        in_specs=[
            pl.BlockSpec((None, 1, ch), lambda i: (jnp.minimum(i, pair_steps - 1), 0, 0),
                         memory_space=smem),
            pl.BlockSpec((None, 1, ch), lambda i: (jnp.maximum(i - pair_steps, 0), 0, 0),
                         memory_space=smem),
            pl.BlockSpec(memory_space=pl.ANY),
            pl.BlockSpec(memory_space=pl.ANY),
        ],
        out_specs=pl.BlockSpec(memory_space=pl.ANY),
        out_shape=jax.ShapeDtypeStruct((n_rows, nsub, LANES), F32),
        scratch_shapes=[pltpu.SemaphoreType.DMA((2,))],
        compiler_params=_params(("arbitrary",)),
        name="moe_dispatch",
    )(dest.reshape(pair_steps, 1, ch), fill.reshape(fill_steps, 1, ch), u3,
      jnp.zeros((1, nsub, LANES), F32))


def _moe_ffn_kernel(te_ref, nt_ref, x_ref, wgu_ref, wd_ref, o_ref, *, f, tmg, nsub):
    del te_ref
    live = pl.program_id(0) < nt_ref[0]

    @pl.when(live)
    def _():
        x = _load_token_tiles(x_ref, tmg, nsub).astype(BF16)
        _store_token_tiles(o_ref, _swiglu(x, wgu_ref, wd_ref, f))

    @pl.when(jnp.logical_not(live))
    def _():
        o_ref[...] = jnp.zeros(o_ref.shape, o_ref.dtype)


def _moe_ffn(xg, tile_expert, n_tiles, w_gu, w_down, tmg):
    ne, f, d = w_down.shape
    nsub = d // LANES
    nt_max = tile_expert.shape[0]
    assert xg.shape[0] == nt_max * tmg * nsub
    rows = lambda i, te, nt: (jnp.minimum(i, nt[0] - 1), 0)
    return pl.pallas_call(
        functools.partial(_moe_ffn_kernel, f=f, tmg=tmg, nsub=nsub),
        grid_spec=pltpu.PrefetchScalarGridSpec(
            num_scalar_prefetch=2,
            grid=(nt_max,),
            in_specs=[
                pl.BlockSpec((tmg * nsub, LANES), rows),
                pl.BlockSpec((None, d, 2 * f), lambda i, te, nt: (te[i], 0, 0)),
                pl.BlockSpec((None, f, d), lambda i, te, nt: (te[i], 0, 0)),
            ],
            out_specs=pl.BlockSpec((tmg * nsub, LANES), lambda i, te, nt: (i, 0)),
        ),
        out_shape=jax.ShapeDtypeStruct(xg.shape, F32),
        compiler_params=_params(("arbitrary",), VMEM_LIMIT),
        name="moe_experts",
    )(tile_expert, n_tiles, xg, w_gu.astype(BF16), w_down.astype(BF16))


def _combine_kernel(dcur_ref, dnxt_ref, h_ref, w_ref, og_hbm, y_ref, buf, sem, *, tm, nsub):
    i = pl.program_id(0)
    n = pl.num_programs(0)
    rows = TOP_K * tm

    def issue(d_ref, slot):
        def body(j, carry):
            for jj in range(DMA_UNROLL):
                r = j * DMA_UNROLL + jj
                dst = buf.at[slot, pl.ds(pl.multiple_of(r * nsub, nsub), nsub), :]
                pltpu.make_async_copy(og_hbm.at[d_ref[0, r]], dst, sem.at[slot]).start()
            return carry
        lax.fori_loop(0, rows // DMA_UNROLL, body, 0)

    @pl.when(i == 0)
    def _():
        issue(dcur_ref, 0)

    @pl.when(i + 1 < n)
    def _():
        issue(dnxt_ref, lax.rem(i + 1, 2))

    slot = lax.rem(i, 2)
    pltpu.make_async_copy(buf.at[slot], buf.at[slot], sem.at[slot]).wait()
    pair = TOP_K * nsub
    a = _load_token_tiles(buf.at[slot], tm, nsub, first=0, stride=pair)
    b = _load_token_tiles(buf.at[slot], tm, nsub, first=nsub, stride=pair)
    w = w_ref[...]
    y_ref[...] = h_ref[...] + (w[:, 0:1] * a + w[:, 1:2] * b)


def _combine(h, w, og, dest):
    t, d = h.shape
    nsub = d // LANES
    tm = min(t, 256)
    assert t % tm == 0 and (TOP_K * tm) % DMA_UNROLL == 0
    n = t // tm
    dest3 = dest.reshape(n, 1, TOP_K * tm)
    smem = pltpu.SMEM
    return pl.pallas_call(
        functools.partial(_combine_kernel, tm=tm, nsub=nsub),
        grid=(n,),
        in_specs=[
            pl.BlockSpec((None, 1, TOP_K * tm), lambda i: (i, 0, 0), memory_space=smem),
            pl.BlockSpec((None, 1, TOP_K * tm), lambda i: (jnp.minimum(i + 1, n - 1), 0, 0),
                         memory_space=smem),
            pl.BlockSpec((tm, d), lambda i: (i, 0)),
            pl.BlockSpec((tm, TOP_K), lambda i: (i, 0)),
            pl.BlockSpec(memory_space=pl.ANY),
        ],
        out_specs=pl.BlockSpec((tm, d), lambda i: (i, 0)),
        out_shape=jax.ShapeDtypeStruct((t, d), F32),
        scratch_shapes=[pltpu.VMEM((2, TOP_K * tm * nsub, LANES), F32),
                        pltpu.SemaphoreType.DMA((2,))],
        compiler_params=_params(("arbitrary",)),
        name="moe_combine",
    )(dest3, dest3, h, w, og)


def _moe(h, u_tiles, idx, w, w_gu, w_down):
    t, d = h.shape
    ne = w_down.shape[0]
    nsub = d // LANES
    npairs = t * TOP_K
    tmg = MOE_ROWS if npairs >= 32 * MOE_ROWS else min(MOE_ROWS // 2, npairs)
    dest, fill, tile_expert, n_tiles = _route_plan(idx, ne, tmg)
    n_rows = npairs + ne * tmg
    xg = _dispatch(u_tiles.reshape(t, nsub, LANES), dest, fill, n_rows)
    og = _moe_ffn(xg.reshape(n_rows * nsub, LANES), tile_expert, n_tiles, w_gu, w_down, tmg)
    return _combine(h, w, og.reshape(n_rows, nsub, LANES), dest)


def _trunk(x, pool_hist, kv_cache, p):
    b, n, d = x.shape
    n_heads = p["b_f"].shape[0]
    start_pos = 0 if kv_cache is None else kv_cache[0].shape[1]

    h, pool_state = _pool_layer(x, None if pool_hist is None else pool_hist[0],
                                p["g_pool_norm"][0], p["w_pool"][0], p["pool_scale"][0],
                                start_pos)
    h = _dense_ffn(h.reshape(b * n, d), p["g_ffn_norm"][0], p["w_ffn_gu"][0],
                   p["w_ffn_down"][0])

    qkv_args = (p["g_kv_norm"], p["g_attn_norm"][0], p["w_kvf"], p["b_f"], p["g_k"],
                p["w_q"][0], p["g_q"][0], n_heads)
    kvw = p["w_q"].shape[2]
    hd = kvw // n_heads
    if kv_cache is None:
        kt, vt, lft, ct, ktb, vtb, qb = _qkv_proj_t(h, b, *qkv_args)
        k = jnp.transpose(kt.reshape(b, n_heads, hd, n), (0, 3, 1, 2))
        v = jnp.transpose(vt.reshape(b, n_heads, hd, n), (0, 3, 1, 2))
        logf = jnp.transpose(lft, (0, 2, 1))
        o = _attn_prompt(qb.reshape(b, n, kvw), ktb, vtb, jnp.transpose(ct, (0, 2, 1)),
                         ct.reshape(b, n_heads // 2, 2, n), n_heads)
    else:
        k, v, logf, kb, vb, qb = _qkv_proj(h, *qkv_args)
        logf = logf.reshape(b, n, n_heads)
        qb, kb, vb = (a.reshape(b, n, kvw) for a in (qb, kb, vb))
        k = k.reshape(b, n, n_heads, hd)
        v = v.reshape(b, n, n_heads, hd)
        cache_k, cache_v, cache_logf = kv_cache
        past = cache_k.shape[1]
        lf_all = jnp.concatenate([cache_logf.astype(F32), logf], axis=1)
        total = past + n
        padded = -(-total // LANES) * LANES
        lf_t = jnp.transpose(lf_all, (0, 2, 1)).reshape(b * n_heads, total)
        c_t = _cumsum_lanes(jnp.pad(lf_t, ((0, 0), (0, padded - total))))
        c_t = c_t.reshape(b, n_heads, padded)
        c_new = c_t[:, :, past:total]
        cache_kt = jnp.transpose(cache_k, (0, 2, 3, 1)).reshape(b, kvw, past)
        cache_vt = jnp.transpose(cache_v, (0, 2, 3, 1)).reshape(b, kvw, past)
        o = _attn_sample(qb, kb, vb, cache_kt, cache_vt, jnp.transpose(c_new, (0, 2, 1)),
                         c_t[:, :, :past], c_new, n_heads)

    h2, u_tiles, idx, top_w = _oproj_router(h, o.reshape(b * n, kvw), p["w_o"][0],
                                            p["g_ffn_norm"][1], p["w_router"][0])
    y = _moe(h2, u_tiles, idx, top_w, p["w_moe_gu"][0], p["w_moe_down"][0])
    return y.reshape(b, n, d), pool_state[None], k, v, logf


def kernel(x_prompt, x_sample, state_pool, cache_k, cache_v, cache_logf, g_pool_norm, w_pool,
           pool_scale, g_kv_norm, w_kvf, b_f, g_k, g_attn_norm, w_q, g_q, w_o, g_ffn_norm,
           w_ffn_gu, w_ffn_down, w_router, w_moe_gu, w_moe_down):
    p = dict(g_pool_norm=g_pool_norm, w_pool=w_pool, pool_scale=pool_scale,
             g_kv_norm=g_kv_norm, w_kvf=w_kvf, b_f=b_f, g_k=g_k, g_attn_norm=g_attn_norm,
             w_q=w_q, g_q=g_q, w_o=w_o, g_ffn_norm=g_ffn_norm, w_ffn_gu=w_ffn_gu,
             w_ffn_down=w_ffn_down, w_router=w_router, w_moe_gu=w_moe_gu,
             w_moe_down=w_moe_down)
    assert w_pool.shape[0] == 1 and w_q.shape[0] == 1 and w_router.shape[0] == 1
    y_p, pool_p, k_p, v_p, lf_p = _trunk(x_prompt, None, None, p)
    y_s, pool_s, k_s, v_s, lf_s = _trunk(
        x_sample, state_pool, (cache_k, cache_v, cache_logf), p)
    return (y_p, y_s, pool_p, k_p, v_p, lf_p, pool_s, k_s, v_s, lf_s)
```

```python
import functools
import math

import jax
import jax.numpy as jnp
from jax import lax
from jax.experimental import pallas as pl
from jax.experimental.pallas import tpu as pltpu

EPS = 1e-6
POOL_WINDOWS = (2, 4, 8, 16)
HALO = max(POOL_WINDOWS)
POOL_HIST = HALO - 1
TOP_K = 2
LANES = 128
MXU_DIM = 256
VMEM_LIMIT = 56 * 1024 * 1024
NEG_INF = float("-inf")

BF16 = jnp.bfloat16
F32 = jnp.float32


def _params(semantics, vmem=None):
    return pltpu.CompilerParams(dimension_semantics=semantics, vmem_limit_bytes=vmem)


def _dot(a, b):
    return jnp.dot(a, b, preferred_element_type=F32)


def _dot_nt(a, b):
    return lax.dot_general(a, b, (((1,), (1,)), ((), ())), preferred_element_type=F32)


def _rms_base(x):
    return x * lax.rsqrt(jnp.mean(x * x, axis=-1, keepdims=True) + EPS)


def _split2(x):
    hi = x.astype(BF16)
    lo = (x - hi.astype(F32)).astype(BF16)
    return hi, lo


def _split3(x):
    hi = x.astype(BF16)
    r = x - hi.astype(F32)
    mid = r.astype(BF16)
    lo = (r - mid.astype(F32)).astype(BF16)
    return hi, mid, lo


def _ffn_chunks(f):
    out, s = [], 0
    while s < f:
        n = min(4 * MXU_DIM, f - s)
        out.append((s, n))
        s += n
    return out


def _pool_kernel(x_ref, halo_ref, g_ref, w_ref, scale_ref, h_ref, st_ref, *,
                 tm, start_pos, halo_is_normed):
    i = pl.program_id(1)
    x = x_ref[...]
    g = g_ref[...]
    u = _rms_base(x) * g
    if halo_is_normed:
        uh = halo_ref[...]
    else:
        uh = _rms_base(halo_ref[...]) * g
        uh = jnp.where(i > 0, uh, 0.0)
    ext = jnp.concatenate([uh, u], axis=0)

    row = lax.broadcasted_iota(jnp.int32, (tm, 1), 0)
    avail = start_pos + i * tm + row + 1
    group = x.shape[1] // len(POOL_WINDOWS)
    ys = []
    for gi, w in enumerate(POOL_WINDOWS):
        lo, hi = gi * group, (gi + 1) * group
        s = ext[:, lo:hi]
        step = 1
        while step < w:
            s = s + pltpu.roll(s, step, 0)
            step *= 2
        inv_cnt = 1.0 / jnp.minimum(avail, w).astype(F32)
        pooled = s[HALO:, :] * inv_cnt
        diff = pooled - u[:, lo:hi]
        ys.append(_dot(diff.astype(BF16), w_ref[gi]))
    y = jnp.concatenate(ys, axis=1) * scale_ref[...]
    h_ref[...] = x + y

    @pl.when(i == pl.num_programs(1) - 1)
    def _():
        st_ref[...] = u[tm - HALO:, :]


def _pool_layer(x, hist, g, w_pool, pool_scale, start_pos):
    b, n, d = x.shape
    tm = min(n, 256)
    assert n % tm == 0 and tm % HALO == 0 and n >= HALO
    nt = n // tm
    if hist is None:
        halo_arr = x
        halo_spec = pl.BlockSpec(
            (None, HALO, d), lambda bi, i: (bi, jnp.maximum(i * (tm // HALO) - 1, 0), 0))
    else:
        assert nt == 1
        halo_arr = jnp.pad(hist, ((0, 0), (HALO - POOL_HIST, 0), (0, 0)))
        halo_spec = pl.BlockSpec((None, HALO, d), lambda bi, i: (bi, 0, 0))
    ng = len(POOL_WINDOWS)
    h, st = pl.pallas_call(
        functools.partial(_pool_kernel, tm=tm, start_pos=start_pos,
                          halo_is_normed=hist is not None),
        grid=(b, nt),
        in_specs=[
            pl.BlockSpec((None, tm, d), lambda bi, i: (bi, i, 0)),
            halo_spec,
            pl.BlockSpec((1, d), lambda bi, i: (0, 0)),
            pl.BlockSpec((ng, d // ng, d // ng), lambda bi, i: (0, 0, 0)),
            pl.BlockSpec((1, d), lambda bi, i: (0, 0)),
        ],
        out_specs=[
            pl.BlockSpec((None, tm, d), lambda bi, i: (bi, i, 0)),
            pl.BlockSpec((None, HALO, d), lambda bi, i: (bi, 0, 0)),
        ],
        out_shape=[jax.ShapeDtypeStruct((b, n, d), F32),
                   jax.ShapeDtypeStruct((b, HALO, d), F32)],
        compiler_params=_params(("parallel", "arbitrary")),
        name="pool_mixer",
    )(x, halo_arr, g.reshape(1, d), w_pool.astype(BF16), pool_scale.reshape(1, d))
    return h, st[:, HALO - POOL_HIST:, :]


def _swiglu(xb, wgu_ref, wd_ref, f):
    acc = None
    for s, n in _ffn_chunks(f):
        gate = _dot(xb, wgu_ref[:, s:s + n])
        up = _dot(xb, wgu_ref[:, f + s:f + s + n])
        act = (gate * jax.nn.sigmoid(gate)) * up
        part = _dot(act.astype(BF16), wd_ref[s:s + n, :])
        acc = part if acc is None else acc + part
    return acc


def _dense_ffn_kernel(h_ref, g_ref, wgu_ref, wd_ref, o_ref, *, f):
    h = h_ref[...]
    u = (_rms_base(h) * g_ref[...]).astype(BF16)
    o_ref[...] = h + _swiglu(u, wgu_ref, wd_ref, f)


def _dense_ffn(h, g, w_gu, w_down):
    t, d = h.shape
    f = w_down.shape[0]
    tm = min(t, 512)
    assert t % tm == 0
    return pl.pallas_call(
        functools.partial(_dense_ffn_kernel, f=f),
        grid=(t // tm,),
        in_specs=[
            pl.BlockSpec((tm, d), lambda i: (i, 0)),
            pl.BlockSpec((1, d), lambda i: (0, 0)),
            pl.BlockSpec((d, 2 * f), lambda i: (0, 0)),
            pl.BlockSpec((f, d), lambda i: (0, 0)),
        ],
        out_specs=pl.BlockSpec((tm, d), lambda i: (i, 0)),
        out_shape=jax.ShapeDtypeStruct((t, d), F32),
        compiler_params=_params(("parallel",), VMEM_LIMIT),
        name="dense_ffn",
    )(h, g.reshape(1, d), w_gu.astype(BF16), w_down.astype(BF16))


def _head_norm(x, bd_ref, g):
    sq_hi, sq_lo = _split2(x * x)
    n = bd_ref.shape[0]
    ms = jnp.concatenate(
        [_dot(sq_hi[:, c:c + n], bd_ref[...]) + _dot(sq_lo[:, c:c + n], bd_ref[...])
         for c in range(0, x.shape[1], n)], axis=1)
    return x * lax.rsqrt(ms + EPS) * g


def _log_sigmoid(z):
    return jnp.minimum(z, 0.0) - jnp.log1p(jnp.exp(-jnp.abs(z)))


def _qkv_kernel(h_ref, gkv_ref, gat_ref, wk_ref, wv_ref, wf_ref, wq_ref, bf_ref,
                gk_ref, gq_ref, bd_ref,
                k_ref, v_ref, lf_ref, kb_ref, vb_ref, qb_ref, *, n_heads, q_scale):
    base = _rms_base(h_ref[...])
    a_kv = (base * gkv_ref[...]).astype(BF16)
    a_q = (base * gat_ref[...]).astype(BF16)
    k = _head_norm(_dot(a_kv, wk_ref[...]), bd_ref, gk_ref[...])
    v = _dot(a_kv, wv_ref[...])
    z = _dot(a_kv, wf_ref[...])[:, :n_heads] + bf_ref[...]
    q = _head_norm(_dot(a_q, wq_ref[...]), bd_ref, gq_ref[...])
    k_ref[...] = k
    v_ref[...] = v
    lf_ref[...] = _log_sigmoid(z)
    kb_ref[...] = k.astype(BF16)
    vb_ref[...] = v.astype(BF16)
    qb_ref[...] = (q * q_scale).astype(BF16)


def _qkv_proj(h, g_kv, g_attn, w_kvf, b_f, g_k, w_q, g_q, n_heads):
    t, d = h.shape
    kvw = w_q.shape[1]
    hd = kvw // n_heads
    tm = min(t, 256)
    assert t % tm == 0 and MXU_DIM % hd == 0 and kvw % MXU_DIM == 0
    wk = w_kvf[:, :kvw].astype(BF16)
    wv = w_kvf[:, kvw:2 * kvw].astype(BF16)
    wf = jnp.pad(w_kvf[:, 2 * kvw:], ((0, 0), (0, LANES - n_heads))).astype(BF16)
    head = jnp.arange(MXU_DIM) // hd
    bd = ((head[:, None] == head[None, :]).astype(F32) / hd).astype(BF16)
    row = lambda i: (i, 0)
    fix = lambda i: (0, 0)
    return pl.pallas_call(
        functools.partial(_qkv_kernel, n_heads=n_heads, q_scale=hd ** -0.5),
        grid=(t // tm,),
        in_specs=[
            pl.BlockSpec((tm, d), row),
            pl.BlockSpec((1, d), fix), pl.BlockSpec((1, d), fix),
            pl.BlockSpec((d, kvw), fix), pl.BlockSpec((d, kvw), fix),
            pl.BlockSpec((d, LANES), fix), pl.BlockSpec((d, kvw), fix),
            pl.BlockSpec((1, n_heads), fix),
            pl.BlockSpec((1, kvw), fix), pl.BlockSpec((1, kvw), fix),
            pl.BlockSpec((MXU_DIM, MXU_DIM), fix),
        ],
        out_specs=[
            pl.BlockSpec((tm, kvw), row), pl.BlockSpec((tm, kvw), row),
            pl.BlockSpec((tm, n_heads), row),
            pl.BlockSpec((tm, kvw), row), pl.BlockSpec((tm, kvw), row),
            pl.BlockSpec((tm, kvw), row),
        ],
        out_shape=[
            jax.ShapeDtypeStruct((t, kvw), F32), jax.ShapeDtypeStruct((t, kvw), F32),
            jax.ShapeDtypeStruct((t, n_heads), F32),
            jax.ShapeDtypeStruct((t, kvw), BF16), jax.ShapeDtypeStruct((t, kvw), BF16),
            jax.ShapeDtypeStruct((t, kvw), BF16),
        ],
        compiler_params=_params(("parallel",), VMEM_LIMIT),
        name="qkv_proj",
    )(h, g_kv.reshape(1, d), g_attn.reshape(1, d), wk, wv, wf, w_q.astype(BF16),
      b_f.reshape(1, n_heads), jnp.tile(g_k, n_heads).reshape(1, kvw),
      jnp.tile(g_q, n_heads).reshape(1, kvw), bd)


def _cumsum_tile(x, tri, carry):
    y = carry
    for part in _split3(x):
        y = y + _dot(part, tri)
    return y


def _qkv_t_kernel(h_ref, gkv_ref, gat_ref, wkt_ref, wvt_ref, wft_ref, wq_ref, bf_ref,
                  gk_ref, gq_ref, bd_ref, tri_ref,
                  kt_ref, vt_ref, lft_ref, ct_ref, kb_ref, vtb_ref, qb_ref, carry_ref, *,
                  n_heads, q_scale):
    @pl.when(pl.program_id(1) == 0)
    def _():
        carry_ref[...] = jnp.zeros(carry_ref.shape, F32)

    base = _rms_base(h_ref[...])
    a_kv = (base * gkv_ref[...]).astype(BF16)
    a_q = (base * gat_ref[...]).astype(BF16)
    tm = a_kv.shape[0]
    kraw = _dot_nt(wkt_ref[...], a_kv)
    k3 = kraw.reshape(n_heads, kraw.shape[0] // n_heads, tm)
    ms = jnp.mean(k3 * k3, axis=1, keepdims=True)
    kt = (k3 * lax.rsqrt(ms + EPS) * gk_ref[...][None]).reshape(kraw.shape)
    vt = _dot_nt(wvt_ref[...], a_kv)
    lft = _log_sigmoid(_dot_nt(wft_ref[...], a_kv) + bf_ref[...])
    ct = _cumsum_tile(lft, tri_ref[...], carry_ref[...])
    carry_ref[...] = ct[:, tm - 1:tm]
    q = _head_norm(_dot(a_q, wq_ref[...]), bd_ref, gq_ref[...])
    kt_ref[...] = kt
    vt_ref[...] = vt
    lft_ref[...] = lft
    ct_ref[...] = ct
    kb_ref[...] = kt.T.astype(BF16)
    vtb_ref[...] = vt.astype(BF16)
    qb_ref[...] = (q * q_scale).astype(BF16)


def _qkv_proj_t(h, b, g_kv, g_attn, w_kvf, b_f, g_k, w_q, g_q, n_heads):
    t, d = h.shape
    s = t // b
    kvw = w_q.shape[1]
    hd = kvw // n_heads
    tm = min(s, 256)
    nt = s // tm
    assert s % tm == 0 and MXU_DIM % hd == 0 and kvw % MXU_DIM == 0
    wkt = w_kvf[:, :kvw].T.astype(BF16)
    wvt = w_kvf[:, kvw:2 * kvw].T.astype(BF16)
    wft = w_kvf[:, 2 * kvw:].T.astype(BF16)
    head = jnp.arange(MXU_DIM) // hd
    bd = ((head[:, None] == head[None, :]).astype(F32) / hd).astype(BF16)
    pos = jnp.arange(tm)
    tri = (pos[:, None] <= pos[None, :]).astype(BF16)
    fix = lambda bi, i: (0, 0)
    feat = lambda rows: pl.BlockSpec((None, rows, tm), lambda bi, i: (bi, 0, i))
    f32 = lambda rows: jax.ShapeDtypeStruct((b, rows, s), F32)
    tok = pl.BlockSpec((tm, kvw), lambda bi, i: (bi * nt + i, 0))
    return pl.pallas_call(
        functools.partial(_qkv_t_kernel, n_heads=n_heads,
                          q_scale=hd ** -0.5 * math.log2(math.e)),
        grid=(b, nt),
        in_specs=[
            pl.BlockSpec((tm, d), lambda bi, i: (bi * nt + i, 0)),
            pl.BlockSpec((1, d), fix), pl.BlockSpec((1, d), fix),
            pl.BlockSpec((kvw, d), fix), pl.BlockSpec((kvw, d), fix),
            pl.BlockSpec((n_heads, d), fix), pl.BlockSpec((d, kvw), fix),
            pl.BlockSpec((n_heads, tm), fix), pl.BlockSpec((hd, tm), fix),
            pl.BlockSpec((1, kvw), fix), pl.BlockSpec((MXU_DIM, MXU_DIM), fix),
            pl.BlockSpec((tm, tm), fix),
        ],
        out_specs=[feat(kvw), feat(kvw), feat(n_heads), feat(n_heads), tok, feat(kvw), tok],
        out_shape=[f32(kvw), f32(kvw), f32(n_heads), f32(n_heads),
                   jax.ShapeDtypeStruct((t, kvw), BF16),
                   jax.ShapeDtypeStruct((b, kvw, s), BF16),
                   jax.ShapeDtypeStruct((t, kvw), BF16)],
        scratch_shapes=[pltpu.VMEM((n_heads, 1), F32)],
        compiler_params=_params(("parallel", "arbitrary"), VMEM_LIMIT),
        name="qkv_proj_t",
    )(h, g_kv.reshape(1, d), g_attn.reshape(1, d), wkt, wvt, wft, w_q.astype(BF16),
      jnp.broadcast_to(b_f[:, None], (n_heads, tm)), jnp.broadcast_to(g_k[:, None], (hd, tm)),
      jnp.tile(g_q, n_heads).reshape(1, kvw), bd, tri)


def _cumsum_kernel(x_ref, o_ref):
    r, l = x_ref.shape
    ii = lax.broadcasted_iota(jnp.int32, (LANES, LANES), 0)
    jj = lax.broadcasted_iota(jnp.int32, (LANES, LANES), 1)
    tri = (ii <= jj).astype(BF16)
    carry = jnp.zeros((r, 1), F32)
    for c in range(0, l, LANES):
        y = _cumsum_tile(x_ref[:, c:c + LANES], tri, carry)
        o_ref[:, c:c + LANES] = y
        carry = y[:, LANES - 1:LANES]


def _cumsum_lanes(x):
    r, l = x.shape
    tr = min(r, 256)
    assert r % tr == 0 and l % LANES == 0
    return pl.pallas_call(
        _cumsum_kernel,
        grid=(r // tr,),
        in_specs=[pl.BlockSpec((tr, l), lambda i: (i, 0))],
        out_specs=pl.BlockSpec((tr, l), lambda i: (i, 0)),
        out_shape=jax.ShapeDtypeStruct((r, l), F32),
        compiler_params=_params(("parallel",)),
        name="logf_cumsum",
    )(x)


def _bdot(a, b, ca, cb):
    return lax.dot_general(a, b, (((ca,), (cb,)), ((0,), (0,))), preferred_element_type=F32)


def _attn_prompt_kernel(q_ref, cq_ref, k_ref, ck_ref, vt_ref, o_ref,
                        kaug_ref, m_ref, l_ref, acc_ref, *, tq, n_heads):
    qi = pl.program_id(1)
    log2e = math.log2(math.e)
    kvw = q_ref.shape[1]
    hd = kvw // n_heads
    lane = lax.broadcasted_iota(jnp.int32, (1, LANES), 1)

    def augment(x_ref, c_all, h, c_first):
        blk = h * hd // LANES * LANES
        feat = x_ref[:, blk:blk + LANES].astype(F32)
        if (h * hd) % LANES:
            feat = pltpu.roll(feat, LANES - (h * hd) % LANES, 1)
        terms = tuple(p.astype(F32) for p in _split3(c_all[:, h:h + 1] * log2e))
        extras = terms + (1.0, 1.0, 1.0) if c_first else (1.0, 1.0, 1.0) + tuple(-t for t in terms)
        out = jnp.where(lane < hd, feat, 0.0)
        for i, v in enumerate(extras):
            out = jnp.where(lane == hd + i, v, out)
        return out.astype(BF16)

    @pl.when(qi == 0)
    def _():
        ck_all = ck_ref[...]
        for h in range(n_heads):
            kaug_ref[h] = augment(k_ref, ck_all, h, False)

    cq_all = cq_ref[...]
    q_aug = jnp.stack([augment(q_ref, cq_all, h, True) for h in range(n_heads)])
    m_ref[...] = jnp.full(m_ref.shape, NEG_INF, F32)
    l_ref[...] = jnp.zeros(l_ref.shape, F32)
    acc_ref[...] = jnp.zeros(acc_ref.shape, F32)

    def tile(kt, masked):
        start = pl.multiple_of(kt * tq, tq)
        st = _bdot(kaug_ref[:, pl.ds(start, tq), :], q_aug, 2, 2)
        if masked:
            key = lax.broadcasted_iota(jnp.int32, (1, tq, tq), 1)
            qry = lax.broadcasted_iota(jnp.int32, (1, tq, tq), 2)
            st = jnp.where(key <= qry, st, NEG_INF)
        m_prev = m_ref[...]
        m_new = jnp.maximum(m_prev, jnp.max(st, axis=1, keepdims=True))
        alpha = jnp.exp2(m_prev - m_new)
        p = jnp.exp2(st - m_new)
        vt = vt_ref[:, pl.ds(start, tq)].reshape(n_heads, hd, tq)
        l_ref[...] = alpha * l_ref[...] + jnp.sum(p, axis=1, keepdims=True)
        acc_ref[...] = alpha * acc_ref[...] + _bdot(vt, p.astype(BF16), 2, 1)
        m_ref[...] = m_new

    def body(kt, carry):
        tile(kt, False)
        return carry

    lax.fori_loop(0, qi, body, 0)
    tile(qi, True)
    ot = (acc_ref[...] / l_ref[...]).reshape(kvw, tq)
    o_ref[...] = ot.T.astype(o_ref.dtype)


def _attn_prompt(q, k, vt, c_col, n_heads):
    b, s, kvw = q.shape
    hd = kvw // n_heads
    assert LANES % hd == 0 and hd + 6 <= LANES
    tq = min(s, 256)
    assert s % tq == 0
    return pl.pallas_call(
        functools.partial(_attn_prompt_kernel, tq=tq, n_heads=n_heads),
        grid=(b, s // tq),
        in_specs=[
            pl.BlockSpec((None, tq, kvw), lambda bi, qi: (bi, qi, 0)),
            pl.BlockSpec((None, tq, n_heads), lambda bi, qi: (bi, qi, 0)),
            pl.BlockSpec((None, s, kvw), lambda bi, qi: (bi, 0, 0)),
            pl.BlockSpec((None, s, n_heads), lambda bi, qi: (bi, 0, 0)),
            pl.BlockSpec((None, kvw, s), lambda bi, qi: (bi, 0, 0)),
        ],
        out_specs=pl.BlockSpec((None, tq, kvw), lambda bi, qi: (bi, qi, 0)),
        out_shape=jax.ShapeDtypeStruct((b, s, kvw), BF16),
        scratch_shapes=[pltpu.VMEM((n_heads, s, LANES), BF16),
                        pltpu.VMEM((n_heads, 1, tq), F32), pltpu.VMEM((n_heads, 1, tq), F32),
                        pltpu.VMEM((n_heads, hd, tq), F32)],
        compiler_params=_params(("parallel", "arbitrary"), VMEM_LIMIT),
        name="fox_prompt",
    )(q, c_col, k, c_col, vt)


def _attn_sample_kernel(q_ref, cq_ref, kc_ref, vc_ref, ckc_ref, kn_ref, vn_ref, ckn_ref,
                        o_ref, m_ref, l_ref, acc_ref, *, n_heads):
    kt = pl.program_id(1)
    last = pl.num_programs(1) - 1

    @pl.when(kt == 0)
    def _():
        m_ref[...] = jnp.full(m_ref.shape, NEG_INF, F32)
        l_ref[...] = jnp.zeros(l_ref.shape, F32)
        acc_ref[...] = jnp.zeros(acc_ref.shape, F32)

    def update(s, v, v_contract):
        m_prev = m_ref[...]
        m_new = jnp.maximum(m_prev, jnp.max(s, axis=2, keepdims=True))
        alpha = jnp.exp(m_prev - m_new)
        p = jnp.exp(s - m_new)
        l_ref[...] = alpha * l_ref[...] + jnp.sum(p, axis=2, keepdims=True)
        acc_ref[...] = alpha * acc_ref[...] + _bdot(p.astype(BF16), v, 2, v_contract)
        m_ref[...] = m_new

    q = q_ref[...]
    cq = cq_ref[...]
    kvw, tk = kc_ref.shape
    split = (n_heads, kvw // n_heads, tk)
    kk = kc_ref[...].reshape(split).astype(BF16)
    vv = vc_ref[...].reshape(split).astype(BF16)
    update(_bdot(q, kk, 2, 1) + (cq - ckc_ref[...]), vv, 2)

    @pl.when(kt == last)
    def _():
        n = q.shape[1]
        r = lax.broadcasted_iota(jnp.int32, (1, n, n), 1)
        c = lax.broadcasted_iota(jnp.int32, (1, n, n), 2)
        s = _bdot(q, kn_ref[...], 2, 2) + (cq - ckn_ref[...])
        update(jnp.where(c <= r, s, NEG_INF), vn_ref[...], 1)
        o_ref[...] = (acc_ref[...] / l_ref[...]).astype(o_ref.dtype)


def _attn_sample(q, k_new, v_new, cache_kt, cache_vt, cq, ck_cache, ck_new):
    b, n_heads, n, hd = q.shape
    kvw, p = cache_kt.shape[1:]
    tk = min(p, 512)
    assert p % tk == 0 and kvw == n_heads * hd
    fix = lambda bi, kt: (bi, 0, 0, 0)
    rows = pl.BlockSpec((None, n_heads, n, hd), fix)
    cache = pl.BlockSpec((None, kvw, tk), lambda bi, kt: (bi, 0, kt))
    return pl.pallas_call(
        functools.partial(_attn_sample_kernel, n_heads=n_heads),
        grid=(b, p // tk),
        in_specs=[
            rows,
            pl.BlockSpec((None, n_heads, n, 1), fix),
            cache, cache,
            pl.BlockSpec((None, n_heads, 1, tk), lambda bi, kt: (bi, 0, 0, kt)),
            rows, rows,
            pl.BlockSpec((None, n_heads, 1, n), fix),
        ],
        out_specs=rows,
        out_shape=jax.ShapeDtypeStruct((b, n_heads, n, hd), BF16),
        scratch_shapes=[pltpu.VMEM((n_heads, n, 1), F32),
                        pltpu.VMEM((n_heads, n, 1), F32),
                        pltpu.VMEM((n_heads, n, hd), F32)],
        compiler_params=_params(("parallel", "arbitrary"), VMEM_LIMIT),
        name="fox_sample",
    )(q, cq, cache_kt, cache_vt, ck_cache, k_new, v_new, ck_new)


def _store_token_tiles(ref, x):
    m, d = x.shape
    nsub = d // LANES
    for s in range(nsub):
        ref[pl.ds(s, m, stride=nsub), :] = x[:, s * LANES:(s + 1) * LANES]


def _load_token_tiles(ref, m, nsub, first=0, stride=None):
    stride = nsub if stride is None else stride
    return jnp.concatenate(
        [ref[pl.ds(first + s, m, stride=stride), :] for s in range(nsub)], axis=1)


def _oproj_router_kernel(h_ref, o_ref, wo_ref, g_ref, wr_ref, h2_ref, u_ref, idx_ref, w_ref, *,
                         n_experts):
    h2 = h_ref[...] + _dot(o_ref[...], wo_ref[...])
    u = _rms_base(h2) * g_ref[...]
    h2_ref[...] = h2
    _store_token_tiles(u_ref, u)
    u_hi, u_lo = _split2(u)
    logits = _dot(u_hi, wr_ref[0]) + _dot(u_hi, wr_ref[1]) + _dot(u_lo, wr_ref[0])
    lane = lax.broadcasted_iota(jnp.int32, logits.shape, 1).astype(F32)
    logits = jnp.where(lane < n_experts, logits, NEG_INF)
    m1 = jnp.max(logits, axis=1, keepdims=True)
    i1 = jnp.min(jnp.where(logits == m1, lane, float(LANES)), axis=1, keepdims=True)
    rest = jnp.where(lane == i1, NEG_INF, logits)
    m2 = jnp.max(rest, axis=1, keepdims=True)
    i2 = jnp.min(jnp.where(rest == m2, lane, float(LANES)), axis=1, keepdims=True)
    e2 = jnp.exp(m2 - m1)
    w1 = 1.0 / (1.0 + e2)
    w2 = e2 / (1.0 + e2)
    idx_ref[...] = jnp.concatenate([i1, i2], axis=1).astype(jnp.int32)
    w_ref[...] = jnp.concatenate([w1, w2], axis=1)


def _oproj_router(h, o, w_o, g, w_router):
    t, d = h.shape
    kvw = o.shape[1]
    ne = w_router.shape[1]
    tm = min(t, 256)
    nsub = d // LANES
    assert t % tm == 0 and ne <= LANES
    wr = jnp.pad(w_router, ((0, 0), (0, LANES - ne)))
    wr_hi = wr.astype(BF16)
    wr_lo = (wr - wr_hi.astype(F32)).astype(BF16)
    row = lambda i: (i, 0)
    return pl.pallas_call(
        functools.partial(_oproj_router_kernel, n_experts=ne),
        grid=(t // tm,),
        in_specs=[
            pl.BlockSpec((tm, d), row), pl.BlockSpec((tm, kvw), row),
            pl.BlockSpec((kvw, d), lambda i: (0, 0)),
            pl.BlockSpec((1, d), lambda i: (0, 0)),
            pl.BlockSpec((2, d, LANES), lambda i: (0, 0, 0)),
        ],
        out_specs=[pl.BlockSpec((tm, d), row), pl.BlockSpec((tm * nsub, LANES), row),
                   pl.BlockSpec((tm, TOP_K), row), pl.BlockSpec((tm, TOP_K), row)],
        out_shape=[jax.ShapeDtypeStruct((t, d), F32),
                   jax.ShapeDtypeStruct((t * nsub, LANES), F32),
                   jax.ShapeDtypeStruct((t, TOP_K), jnp.int32),
                   jax.ShapeDtypeStruct((t, TOP_K), F32)],
        compiler_params=_params(("parallel",)),
        name="oproj_router",
    )(h, o, w_o.astype(BF16), g.reshape(1, d), jnp.stack([wr_hi, wr_lo]))


MOE_ROWS = 512
DMA_UNROLL = 8


def _route_plan(idx, ne, tmg):
    t = idx.shape[0]
    npairs = t * TOP_K
    e_flat = idx.reshape(npairs)
    onehot = (e_flat[:, None] == jnp.arange(ne, dtype=jnp.int32)[None, :]).astype(jnp.int32)
    csum = jnp.cumsum(onehot, axis=0)
    cnt = csum[-1]
    padded = (cnt + tmg - 1) // tmg * tmg
    ends = jnp.cumsum(padded)
    off = ends - padded
    dest = jnp.sum(onehot * (off[None, :] + csum), axis=1) - 1
    total = ends[-1]
    k = jnp.arange(tmg, dtype=jnp.int32)[None, :]
    valid = (k < (padded - cnt)[:, None]).reshape(-1)
    pad_row = ((off + cnt)[:, None] + k).reshape(-1)
    tail_row = total + jnp.cumsum(jnp.logical_not(valid).astype(jnp.int32)) - 1
    fill = jnp.where(valid, pad_row, tail_row)
    n_tiles = total // tmg
    nt_max = (npairs + ne * tmg) // tmg
    tile = jnp.minimum(jnp.arange(nt_max, dtype=jnp.int32), n_tiles - 1)
    tile_expert = jnp.sum((tile[:, None] >= (ends // tmg)[None, :]).astype(jnp.int32), axis=1)
    return dest, fill, tile_expert, n_tiles.reshape(1)


def _dispatch_kernel(dest_ref, fill_ref, u_ref, xg_hbm, zero_ref, sem, *, ch, pair_steps, nsub):
    i = pl.program_id(0)

    def issue(src_row, dst_rows):
        def body(j, carry):
            for jj in range(DMA_UNROLL):
                pltpu.make_async_copy(src_row(j, jj), xg_hbm.at[dst_rows[0, j * DMA_UNROLL + jj]],
                                      sem.at[0]).start()
            return carry
        lax.fori_loop(0, ch // DMA_UNROLL, body, 0)

    def token_tile(j, jj):
        tok = j * (DMA_UNROLL // TOP_K) + jj // TOP_K
        return u_ref.at[pl.ds(pl.multiple_of(tok * nsub, nsub), nsub), :]

    @pl.when(i < pair_steps)
    def _():
        issue(token_tile, dest_ref)

    @pl.when(i >= pair_steps)
    def _():
        zero_ref[...] = jnp.zeros(zero_ref.shape, zero_ref.dtype)
        issue(lambda j, jj: zero_ref, fill_ref)

    pltpu.make_async_copy(xg_hbm.at[pl.ds(0, ch)], xg_hbm.at[pl.ds(0, ch)], sem.at[0]).wait()


def _dispatch(u_tiles, dest, fill, n_rows, nsub):
    t = u_tiles.shape[0] // nsub
    npairs = dest.shape[0]
    assert npairs == t * TOP_K and DMA_UNROLL % TOP_K == 0
    ch = min(2 * MOE_ROWS, npairs)
    assert npairs % ch == 0 and fill.shape[0] % ch == 0 and ch % DMA_UNROLL == 0
    pair_steps = npairs // ch
    fill_steps = fill.shape[0] // ch
    smem = pltpu.SMEM
    return pl.pallas_call(
        functools.partial(_dispatch_kernel, ch=ch, pair_steps=pair_steps, nsub=nsub),
        grid=(pair_steps + fill_steps,),
        in_specs=[
            pl.BlockSpec((None, 1, ch), lambda i: (jnp.minimum(i, pair_steps - 1), 0, 0),
                         memory_space=smem),
            pl.BlockSpec((None, 1, ch), lambda i: (jnp.maximum(i - pair_steps, 0), 0, 0),
                         memory_space=smem),
            pl.BlockSpec((ch // TOP_K * nsub, LANES),
                         lambda i: (jnp.minimum(i, pair_steps - 1), 0)),
        ],
        out_specs=pl.BlockSpec(memory_space=pl.ANY),
        out_shape=jax.ShapeDtypeStruct((n_rows, nsub, LANES), F32),
        scratch_shapes=[pltpu.VMEM((nsub, LANES), F32), pltpu.SemaphoreType.DMA((1,))],
        compiler_params=_params(("arbitrary",)),
        name="moe_dispatch",
    )(dest.reshape(pair_steps, 1, ch), fill.reshape(fill_steps, 1, ch), u_tiles)


def _moe_ffn_kernel(te_ref, nt_ref, x_ref, wgu_ref, wd_ref, o_ref, *, f, tmg, nsub):
    del te_ref
    live = pl.program_id(0) < nt_ref[0]

    @pl.when(live)
    def _():
        x = _load_token_tiles(x_ref, tmg, nsub).astype(BF16)
        _store_token_tiles(o_ref, _swiglu(x, wgu_ref, wd_ref, f))

    @pl.when(jnp.logical_not(live))
    def _():
        o_ref[...] = jnp.zeros(o_ref.shape, o_ref.dtype)


def _moe_ffn(xg, tile_expert, n_tiles, w_gu, w_down, tmg):
    ne, f, d = w_down.shape
    nsub = d // LANES
    nt_max = tile_expert.shape[0]
    assert xg.shape[0] == nt_max * tmg * nsub
    rows = lambda i, te, nt: (jnp.minimum(i, nt[0] - 1), 0)
    return pl.pallas_call(
        functools.partial(_moe_ffn_kernel, f=f, tmg=tmg, nsub=nsub),
        grid_spec=pltpu.PrefetchScalarGridSpec(
            num_scalar_prefetch=2,
            grid=(nt_max,),
            in_specs=[
                pl.BlockSpec((tmg * nsub, LANES), rows),
                pl.BlockSpec((None, d, 2 * f), lambda i, te, nt: (te[i], 0, 0)),
                pl.BlockSpec((None, f, d), lambda i, te, nt: (te[i], 0, 0)),
            ],
            out_specs=pl.BlockSpec((tmg * nsub, LANES), lambda i, te, nt: (i, 0)),
        ),
        out_shape=jax.ShapeDtypeStruct(xg.shape, F32),
        compiler_params=_params(("arbitrary",), VMEM_LIMIT),
        name="moe_experts",
    )(tile_expert, n_tiles, xg, w_gu.astype(BF16), w_down.astype(BF16))


def _combine_kernel(dcur_ref, dnxt_ref, h_ref, w_ref, og_hbm, y_ref, buf, sem, *, tm, nsub):
    i = pl.program_id(0)
    n = pl.num_programs(0)
    rows = TOP_K * tm

    def issue(d_ref, slot):
        def body(j, carry):
            for jj in range(DMA_UNROLL):
                r = j * DMA_UNROLL + jj
                dst = buf.at[slot, pl.ds(pl.multiple_of(r * nsub, nsub), nsub), :]
                pltpu.make_async_copy(og_hbm.at[d_ref[0, r]], dst, sem.at[slot]).start()
            return carry
        lax.fori_loop(0, rows // DMA_UNROLL, body, 0)

    @pl.when(i == 0)
    def _():
        issue(dcur_ref, 0)

    @pl.when(i + 1 < n)
    def _():
        issue(dnxt_ref, lax.rem(i + 1, 2))

    slot = lax.rem(i, 2)
    pltpu.make_async_copy(buf.at[slot], buf.at[slot], sem.at[slot]).wait()
    pair = TOP_K * nsub
    a = _load_token_tiles(buf.at[slot], tm, nsub, first=0, stride=pair)
    b = _load_token_tiles(buf.at[slot], tm, nsub, first=nsub, stride=pair)
    w = w_ref[...]
    y_ref[...] = h_ref[...] + (w[:, 0:1] * a + w[:, 1:2] * b)


def _combine(h, w, og, dest):
    t, d = h.shape
    nsub = d // LANES
    tm = min(t, 256)
    assert t % tm == 0 and (TOP_K * tm) % DMA_UNROLL == 0
    n = t // tm
    dest3 = dest.reshape(n, 1, TOP_K * tm)
    smem = pltpu.SMEM
    return pl.pallas_call(
        functools.partial(_combine_kernel, tm=tm, nsub=nsub),
        grid=(n,),
        in_specs=[
            pl.BlockSpec((None, 1, TOP_K * tm), lambda i: (i, 0, 0), memory_space=smem),
            pl.BlockSpec((None, 1, TOP_K * tm), lambda i: (jnp.minimum(i + 1, n - 1), 0, 0),
                         memory_space=smem),
            pl.BlockSpec((tm, d), lambda i: (i, 0)),
            pl.BlockSpec((tm, TOP_K), lambda i: (i, 0)),
            pl.BlockSpec(memory_space=pl.ANY),
        ],
        out_specs=pl.BlockSpec((tm, d), lambda i: (i, 0)),
        out_shape=jax.ShapeDtypeStruct((t, d), F32),
        scratch_shapes=[pltpu.VMEM((2, TOP_K * tm * nsub, LANES), F32),
                        pltpu.SemaphoreType.DMA((2,))],
        compiler_params=_params(("arbitrary",)),
        name="moe_combine",
    )(dest3, dest3, h, w, og)


def _moe(h, u_tiles, idx, w, w_gu, w_down):
    t, d = h.shape
    ne = w_down.shape[0]
    nsub = d // LANES
    npairs = t * TOP_K
    tmg = MOE_ROWS if npairs >= 32 * MOE_ROWS else min(MOE_ROWS // 2, npairs)
    dest, fill, tile_expert, n_tiles = _route_plan(idx, ne, tmg)
    n_rows = npairs + ne * tmg
    xg = _dispatch(u_tiles, dest, fill, n_rows, nsub)
    og = _moe_ffn(xg.reshape(n_rows * nsub, LANES), tile_expert, n_tiles, w_gu, w_down, tmg)
    return _combine(h, w, og.reshape(n_rows, nsub, LANES), dest)


def _trunk(x, pool_hist, kv_cache, p):
    b, n, d = x.shape
    n_heads = p["b_f"].shape[0]
    start_pos = 0 if kv_cache is None else kv_cache[0].shape[1]

    h, pool_state = _pool_layer(x, None if pool_hist is None else pool_hist[0],
                                p["g_pool_norm"][0], p["w_pool"][0], p["pool_scale"][0],
                                start_pos)
    h = _dense_ffn(h.reshape(b * n, d), p["g_ffn_norm"][0], p["w_ffn_gu"][0],
                   p["w_ffn_down"][0])

    qkv_args = (p["g_kv_norm"], p["g_attn_norm"][0], p["w_kvf"], p["b_f"], p["g_k"],
                p["w_q"][0], p["g_q"][0], n_heads)
    kvw = p["w_q"].shape[2]
    hd = kvw // n_heads
    if kv_cache is None:
        kt, vt, lft, ct, kb, vtb, qb = _qkv_proj_t(h, b, *qkv_args)
        k = jnp.transpose(kt.reshape(b, n_heads, hd, n), (0, 3, 1, 2))
        v = jnp.transpose(vt.reshape(b, n_heads, hd, n), (0, 3, 1, 2))
        logf = jnp.transpose(lft, (0, 2, 1))
        o = _attn_prompt(qb.reshape(b, n, kvw), kb.reshape(b, n, kvw), vtb,
                         jnp.transpose(ct, (0, 2, 1)), n_heads)
    else:
        k, v, logf, kb, vb, qb = _qkv_proj(h, *qkv_args)
        logf = logf.reshape(b, n, n_heads)
        qb, kb, vb = (a.reshape(b, n, kvw) for a in (qb, kb, vb))
        k = k.reshape(b, n, n_heads, hd)
        v = v.reshape(b, n, n_heads, hd)
        cache_k, cache_v, cache_logf = kv_cache
        past = cache_k.shape[1]
        lf_all = jnp.concatenate([cache_logf.astype(F32), logf], axis=1)
        total = past + n
        padded = -(-total // LANES) * LANES
        lf_t = jnp.transpose(lf_all, (0, 2, 1)).reshape(b * n_heads, total)
        c_t = _cumsum_lanes(jnp.pad(lf_t, ((0, 0), (0, padded - total))))
        c_t = c_t.reshape(b, n_heads, padded)
        c_new = c_t[:, :, past:total]
        cache_kt = jnp.transpose(cache_k, (0, 2, 3, 1)).reshape(b, kvw, past)
        cache_vt = jnp.transpose(cache_v, (0, 2, 3, 1)).reshape(b, kvw, past)
        heads = lambda a: jnp.transpose(a.reshape(b, n, n_heads, hd), (0, 2, 1, 3))
        o = _attn_sample(heads(qb), heads(kb), heads(vb), cache_kt, cache_vt, c_new[..., None],
                         c_t[:, :, None, :past], c_new[:, :, None, :])
        o = jnp.transpose(o, (0, 2, 1, 3))

    h2, u_tiles, idx, top_w = _oproj_router(h, o.reshape(b * n, kvw), p["w_o"][0],
                                            p["g_ffn_norm"][1], p["w_router"][0])
    y = _moe(h2, u_tiles, idx, top_w, p["w_moe_gu"][0], p["w_moe_down"][0])
    return y.reshape(b, n, d), pool_state[None], k, v, logf


def kernel(x_prompt, x_sample, state_pool, cache_k, cache_v, cache_logf, g_pool_norm, w_pool,
           pool_scale, g_kv_norm, w_kvf, b_f, g_k, g_attn_norm, w_q, g_q, w_o, g_ffn_norm,
           w_ffn_gu, w_ffn_down, w_router, w_moe_gu, w_moe_down):
    p = dict(g_pool_norm=g_pool_norm, w_pool=w_pool, pool_scale=pool_scale,
             g_kv_norm=g_kv_norm, w_kvf=w_kvf, b_f=b_f, g_k=g_k, g_attn_norm=g_attn_norm,
             w_q=w_q, g_q=g_q, w_o=w_o, g_ffn_norm=g_ffn_norm, w_ffn_gu=w_ffn_gu,
             w_ffn_down=w_ffn_down, w_router=w_router, w_moe_gu=w_moe_gu,
             w_moe_down=w_moe_down)
    assert w_pool.shape[0] == 1 and w_q.shape[0] == 1 and w_router.shape[0] == 1
    y_p, pool_p, k_p, v_p, lf_p = _trunk(x_prompt, None, None, p)
    y_s, pool_s, k_s, v_s, lf_s = _trunk(
        x_sample, state_pool, (cache_k, cache_v, cache_logf), p)
    return (y_p, y_s, pool_p, k_p, v_p, lf_p, pool_s, k_s, v_s, lf_s)
```

```python
import functools
import math

import jax
import jax.numpy as jnp
from jax import lax
from jax.experimental import pallas as pl
from jax.experimental.pallas import tpu as pltpu

EPS = 1e-6
POOL_WINDOWS = (2, 4, 8, 16)
HALO = max(POOL_WINDOWS)
POOL_HIST = HALO - 1
TOP_K = 2
LANES = 128
MXU_DIM = 256
VMEM_LIMIT = 56 * 1024 * 1024
NEG_INF = float("-inf")

BF16 = jnp.bfloat16
F32 = jnp.float32


def _params(semantics, vmem=None):
    return pltpu.CompilerParams(dimension_semantics=semantics, vmem_limit_bytes=vmem)


def _dot(a, b):
    return jnp.dot(a, b, preferred_element_type=F32)


def _dot_nt(a, b):
    return lax.dot_general(a, b, (((1,), (1,)), ((), ())), preferred_element_type=F32)


def _rms_base(x):
    return x * lax.rsqrt(jnp.mean(x * x, axis=-1, keepdims=True) + EPS)


def _split2(x):
    hi = x.astype(BF16)
    lo = (x - hi.astype(F32)).astype(BF16)
    return hi, lo


def _split3(x):
    hi = x.astype(BF16)
    r = x - hi.astype(F32)
    mid = r.astype(BF16)
    lo = (r - mid.astype(F32)).astype(BF16)
    return hi, mid, lo


def _ffn_chunks(f):
    out, s = [], 0
    while s < f:
        n = min(4 * MXU_DIM, f - s)
        out.append((s, n))
        s += n
    return out


def _pool_kernel(x_ref, halo_ref, g_ref, w_ref, scale_ref, *rest, tm, start_pos, halo_is_normed,
                 f):
    if f is None:
        h_ref, st_ref = rest
    else:
        gf_ref, wgu_ref, wd_ref, h_ref, st_ref = rest
    i = pl.program_id(1)
    x = x_ref[...]
    g = g_ref[...]
    u = _rms_base(x) * g
    if halo_is_normed:
        uh = halo_ref[...]
    else:
        uh = _rms_base(halo_ref[...]) * g
        uh = jnp.where(i > 0, uh, 0.0)
    ext = jnp.concatenate([uh, u], axis=0)

    row = lax.broadcasted_iota(jnp.int32, (tm, 1), 0)
    avail = start_pos + i * tm + row + 1
    group = x.shape[1] // len(POOL_WINDOWS)
    ys = []
    for gi, w in enumerate(POOL_WINDOWS):
        lo, hi = gi * group, (gi + 1) * group
        s = ext[:, lo:hi]
        step = 1
        while step < w:
            s = s + pltpu.roll(s, step, 0)
            step *= 2
        inv_cnt = 1.0 / jnp.minimum(avail, w).astype(F32)
        pooled = s[HALO:, :] * inv_cnt
        diff = pooled - u[:, lo:hi]
        ys.append(_dot(diff.astype(BF16), w_ref[gi]))
    h = x + jnp.concatenate(ys, axis=1) * scale_ref[...]
    if f is not None:
        h = h + _swiglu((_rms_base(h) * gf_ref[...]).astype(BF16), wgu_ref, wd_ref, f)
    h_ref[...] = h

    @pl.when(i == pl.num_programs(1) - 1)
    def _():
        st_ref[...] = u[tm - HALO:, :]


def _pool_layer(x, hist, g, w_pool, pool_scale, start_pos, ffn=None):
    b, n, d = x.shape
    tm = min(n, 256 if ffn is None else 512)
    assert n % tm == 0 and tm % HALO == 0 and n >= HALO
    nt = n // tm
    if hist is None:
        halo_arr = x
        halo_spec = pl.BlockSpec(
            (None, HALO, d), lambda bi, i: (bi, jnp.maximum(i * (tm // HALO) - 1, 0), 0))
    else:
        assert nt == 1
        halo_arr = jnp.pad(hist, ((0, 0), (HALO - POOL_HIST, 0), (0, 0)))
        halo_spec = pl.BlockSpec((None, HALO, d), lambda bi, i: (bi, 0, 0))
    ng = len(POOL_WINDOWS)
    vec = pl.BlockSpec((1, d), lambda bi, i: (0, 0))
    in_specs = [pl.BlockSpec((None, tm, d), lambda bi, i: (bi, i, 0)), halo_spec, vec,
                pl.BlockSpec((ng, d // ng, d // ng), lambda bi, i: (0, 0, 0)), vec]
    args = [x, halo_arr, g.reshape(1, d), w_pool.astype(BF16), pool_scale.reshape(1, d)]
    f = None
    if ffn is not None:
        g_ffn, w_gu, w_down = ffn
        f = w_down.shape[0]
        in_specs += [vec, pl.BlockSpec((d, 2 * f), lambda bi, i: (0, 0)),
                     pl.BlockSpec((f, d), lambda bi, i: (0, 0))]
        args += [g_ffn.reshape(1, d), w_gu.astype(BF16), w_down.astype(BF16)]
    h, st = pl.pallas_call(
        functools.partial(_pool_kernel, tm=tm, start_pos=start_pos,
                          halo_is_normed=hist is not None, f=f),
        grid=(b, nt),
        in_specs=in_specs,
        out_specs=[
            pl.BlockSpec((None, tm, d), lambda bi, i: (bi, i, 0)),
            pl.BlockSpec((None, HALO, d), lambda bi, i: (bi, 0, 0)),
        ],
        out_shape=[jax.ShapeDtypeStruct((b, n, d), F32),
                   jax.ShapeDtypeStruct((b, HALO, d), F32)],
        compiler_params=_params(("parallel", "arbitrary"), None if ffn is None else VMEM_LIMIT),
        name="pool_mixer" if ffn is None else "pool_ffn",
    )(*args)
    return h, st[:, HALO - POOL_HIST:, :]


def _swiglu(xb, wgu_ref, wd_ref, f):
    acc = None
    for s, n in _ffn_chunks(f):
        gate = _dot(xb, wgu_ref[:, s:s + n])
        up = _dot(xb, wgu_ref[:, f + s:f + s + n])
        act = (gate * jax.nn.sigmoid(gate)) * up
        part = _dot(act.astype(BF16), wd_ref[s:s + n, :])
        acc = part if acc is None else acc + part
    return acc


def _dense_ffn_kernel(h_ref, g_ref, wgu_ref, wd_ref, o_ref, *, f):
    h = h_ref[...]
    u = (_rms_base(h) * g_ref[...]).astype(BF16)
    o_ref[...] = h + _swiglu(u, wgu_ref, wd_ref, f)


def _dense_ffn(h, g, w_gu, w_down):
    t, d = h.shape
    f = w_down.shape[0]
    tm = min(t, 512)
    assert t % tm == 0
    return pl.pallas_call(
        functools.partial(_dense_ffn_kernel, f=f),
        grid=(t // tm,),
        in_specs=[
            pl.BlockSpec((tm, d), lambda i: (i, 0)),
            pl.BlockSpec((1, d), lambda i: (0, 0)),
            pl.BlockSpec((d, 2 * f), lambda i: (0, 0)),
            pl.BlockSpec((f, d), lambda i: (0, 0)),
        ],
        out_specs=pl.BlockSpec((tm, d), lambda i: (i, 0)),
        out_shape=jax.ShapeDtypeStruct((t, d), F32),
        compiler_params=_params(("parallel",), VMEM_LIMIT),
        name="dense_ffn",
    )(h, g.reshape(1, d), w_gu.astype(BF16), w_down.astype(BF16))


def _head_norm(x, bd_ref, g):
    sq_hi, sq_lo = _split2(x * x)
    n = bd_ref.shape[0]
    ms = jnp.concatenate(
        [_dot(sq_hi[:, c:c + n], bd_ref[...]) + _dot(sq_lo[:, c:c + n], bd_ref[...])
         for c in range(0, x.shape[1], n)], axis=1)
    return x * lax.rsqrt(ms + EPS) * g


def _log_sigmoid(z):
    return jnp.minimum(z, 0.0) - jnp.log1p(jnp.exp(-jnp.abs(z)))


def _qkv_kernel(h_ref, gkv_ref, gat_ref, wk_ref, wv_ref, wf_ref, wq_ref, bf_ref,
                gk_ref, gq_ref, bd_ref,
                k_ref, v_ref, lf_ref, kb_ref, vb_ref, qb_ref, *, n_heads, q_scale):
    base = _rms_base(h_ref[...])
    a_kv = (base * gkv_ref[...]).astype(BF16)
    a_q = (base * gat_ref[...]).astype(BF16)
    k = _head_norm(_dot(a_kv, wk_ref[...]), bd_ref, gk_ref[...])
    v = _dot(a_kv, wv_ref[...])
    z = _dot(a_kv, wf_ref[...])[:, :n_heads] + bf_ref[...]
    q = _head_norm(_dot(a_q, wq_ref[...]), bd_ref, gq_ref[...])
    k_ref[...] = k
    v_ref[...] = v
    lf_ref[...] = _log_sigmoid(z)
    kb_ref[...] = k.astype(BF16)
    vb_ref[...] = v.astype(BF16)
    qb_ref[...] = (q * q_scale).astype(BF16)


def _qkv_proj(h, g_kv, g_attn, w_kvf, b_f, g_k, w_q, g_q, n_heads):
    t, d = h.shape
    kvw = w_q.shape[1]
    hd = kvw // n_heads
    tm = min(t, 256)
    assert t % tm == 0 and MXU_DIM % hd == 0 and kvw % MXU_DIM == 0
    wk = w_kvf[:, :kvw].astype(BF16)
    wv = w_kvf[:, kvw:2 * kvw].astype(BF16)
    wf = jnp.pad(w_kvf[:, 2 * kvw:], ((0, 0), (0, LANES - n_heads))).astype(BF16)
    head = jnp.arange(MXU_DIM) // hd
    bd = ((head[:, None] == head[None, :]).astype(F32) / hd).astype(BF16)
    row = lambda i: (i, 0)
    fix = lambda i: (0, 0)
    return pl.pallas_call(
        functools.partial(_qkv_kernel, n_heads=n_heads, q_scale=hd ** -0.5),
        grid=(t // tm,),
        in_specs=[
            pl.BlockSpec((tm, d), row),
            pl.BlockSpec((1, d), fix), pl.BlockSpec((1, d), fix),
            pl.BlockSpec((d, kvw), fix), pl.BlockSpec((d, kvw), fix),
            pl.BlockSpec((d, LANES), fix), pl.BlockSpec((d, kvw), fix),
            pl.BlockSpec((1, n_heads), fix),
            pl.BlockSpec((1, kvw), fix), pl.BlockSpec((1, kvw), fix),
            pl.BlockSpec((MXU_DIM, MXU_DIM), fix),
        ],
        out_specs=[
            pl.BlockSpec((tm, kvw), row), pl.BlockSpec((tm, kvw), row),
            pl.BlockSpec((tm, n_heads), row),
            pl.BlockSpec((tm, kvw), row), pl.BlockSpec((tm, kvw), row),
            pl.BlockSpec((tm, kvw), row),
        ],
        out_shape=[
            jax.ShapeDtypeStruct((t, kvw), F32), jax.ShapeDtypeStruct((t, kvw), F32),
            jax.ShapeDtypeStruct((t, n_heads), F32),
            jax.ShapeDtypeStruct((t, kvw), BF16), jax.ShapeDtypeStruct((t, kvw), BF16),
            jax.ShapeDtypeStruct((t, kvw), BF16),
        ],
        compiler_params=_params(("parallel",), VMEM_LIMIT),
        name="qkv_proj",
    )(h, g_kv.reshape(1, d), g_attn.reshape(1, d), wk, wv, wf, w_q.astype(BF16),
      b_f.reshape(1, n_heads), jnp.tile(g_k, n_heads).reshape(1, kvw),
      jnp.tile(g_q, n_heads).reshape(1, kvw), bd)


def _cumsum_tile(x, tri, carry):
    y = carry
    for part in _split3(x):
        y = y + _dot(part, tri)
    return y


def _qkv_t_kernel(h_ref, gkv_ref, gat_ref, wkt_ref, wvt_ref, wft_ref, wq_ref, bf_ref,
                  gk_ref, gq_ref, bd_ref, tri_ref,
                  kt_ref, vt_ref, lft_ref, ct_ref, kb_ref, vtb_ref, qb_ref, carry_ref, *,
                  n_heads, q_scale):
    @pl.when(pl.program_id(1) == 0)
    def _():
        carry_ref[...] = jnp.zeros(carry_ref.shape, F32)

    base = _rms_base(h_ref[...])
    a_kv = (base * gkv_ref[...]).astype(BF16)
    a_q = (base * gat_ref[...]).astype(BF16)
    tm = a_kv.shape[0]
    kraw = _dot_nt(wkt_ref[...], a_kv)
    k3 = kraw.reshape(n_heads, kraw.shape[0] // n_heads, tm)
    ms = jnp.mean(k3 * k3, axis=1, keepdims=True)
    kt = (k3 * lax.rsqrt(ms + EPS) * gk_ref[...][None]).reshape(kraw.shape)
    vt = _dot_nt(wvt_ref[...], a_kv)
    lft = _log_sigmoid(_dot_nt(wft_ref[...], a_kv) + bf_ref[...])
    ct = _cumsum_tile(lft, tri_ref[...], carry_ref[...])
    carry_ref[...] = ct[:, tm - 1:tm]
    q = _head_norm(_dot(a_q, wq_ref[...]), bd_ref, gq_ref[...])
    kt_ref[...] = kt
    vt_ref[...] = vt
    lft_ref[...] = lft
    ct_ref[...] = ct
    kb_ref[...] = kt.T.astype(BF16)
    vtb_ref[...] = vt.astype(BF16)
    qb_ref[...] = (q * q_scale).astype(BF16)


def _qkv_proj_t(h, b, g_kv, g_attn, w_kvf, b_f, g_k, w_q, g_q, n_heads):
    t, d = h.shape
    s = t // b
    kvw = w_q.shape[1]
    hd = kvw // n_heads
    tm = min(s, 256)
    nt = s // tm
    assert s % tm == 0 and MXU_DIM % hd == 0 and kvw % MXU_DIM == 0
    wkt = w_kvf[:, :kvw].T.astype(BF16)
    wvt = w_kvf[:, kvw:2 * kvw].T.astype(BF16)
    wft = w_kvf[:, 2 * kvw:].T.astype(BF16)
    head = jnp.arange(MXU_DIM) // hd
    bd = ((head[:, None] == head[None, :]).astype(F32) / hd).astype(BF16)
    pos = jnp.arange(tm)
    tri = (pos[:, None] <= pos[None, :]).astype(BF16)
    fix = lambda bi, i: (0, 0)
    feat = lambda rows: pl.BlockSpec((None, rows, tm), lambda bi, i: (bi, 0, i))
    f32 = lambda rows: jax.ShapeDtypeStruct((b, rows, s), F32)
    tok = pl.BlockSpec((tm, kvw), lambda bi, i: (bi * nt + i, 0))
    return pl.pallas_call(
        functools.partial(_qkv_t_kernel, n_heads=n_heads,
                          q_scale=hd ** -0.5 * math.log2(math.e)),
        grid=(b, nt),
        in_specs=[
            pl.BlockSpec((tm, d), lambda bi, i: (bi * nt + i, 0)),
            pl.BlockSpec((1, d), fix), pl.BlockSpec((1, d), fix),
            pl.BlockSpec((kvw, d), fix), pl.BlockSpec((kvw, d), fix),
            pl.BlockSpec((n_heads, d), fix), pl.BlockSpec((d, kvw), fix),
            pl.BlockSpec((n_heads, tm), fix), pl.BlockSpec((hd, tm), fix),
            pl.BlockSpec((1, kvw), fix), pl.BlockSpec((MXU_DIM, MXU_DIM), fix),
            pl.BlockSpec((tm, tm), fix),
        ],
        out_specs=[feat(kvw), feat(kvw), feat(n_heads), feat(n_heads), tok, feat(kvw), tok],
        out_shape=[f32(kvw), f32(kvw), f32(n_heads), f32(n_heads),
                   jax.ShapeDtypeStruct((t, kvw), BF16),
                   jax.ShapeDtypeStruct((b, kvw, s), BF16),
                   jax.ShapeDtypeStruct((t, kvw), BF16)],
        scratch_shapes=[pltpu.VMEM((n_heads, 1), F32)],
        compiler_params=_params(("parallel", "arbitrary"), VMEM_LIMIT),
        name="qkv_proj_t",
    )(h, g_kv.reshape(1, d), g_attn.reshape(1, d), wkt, wvt, wft, w_q.astype(BF16),
      jnp.broadcast_to(b_f[:, None], (n_heads, tm)), jnp.broadcast_to(g_k[:, None], (hd, tm)),
      jnp.tile(g_q, n_heads).reshape(1, kvw), bd, tri)


def _cumsum_kernel(x_ref, o_ref):
    r, l = x_ref.shape
    ii = lax.broadcasted_iota(jnp.int32, (LANES, LANES), 0)
    jj = lax.broadcasted_iota(jnp.int32, (LANES, LANES), 1)
    tri = (ii <= jj).astype(BF16)
    carry = jnp.zeros((r, 1), F32)
    for c in range(0, l, LANES):
        y = _cumsum_tile(x_ref[:, c:c + LANES], tri, carry)
        o_ref[:, c:c + LANES] = y
        carry = y[:, LANES - 1:LANES]


def _cumsum_lanes(x):
    r, l = x.shape
    tr = min(r, 256)
    assert r % tr == 0 and l % LANES == 0
    return pl.pallas_call(
        _cumsum_kernel,
        grid=(r // tr,),
        in_specs=[pl.BlockSpec((tr, l), lambda i: (i, 0))],
        out_specs=pl.BlockSpec((tr, l), lambda i: (i, 0)),
        out_shape=jax.ShapeDtypeStruct((r, l), F32),
        compiler_params=_params(("parallel",)),
        name="logf_cumsum",
    )(x)


def _bdot(a, b, ca, cb):
    return lax.dot_general(a, b, (((ca,), (cb,)), ((0,), (0,))), preferred_element_type=F32)


def _attn_prompt_kernel(q_ref, cq_ref, k_ref, ck_ref, vt_ref, o_ref,
                        kaug_ref, m_ref, l_ref, acc_ref, *, tq, n_heads):
    qi = pl.program_id(1)
    log2e = math.log2(math.e)
    kvw = q_ref.shape[1]
    hd = kvw // n_heads
    lane = lax.broadcasted_iota(jnp.int32, (1, LANES), 1)

    def augment(x_ref, c_all, h, c_first):
        blk = h * hd // LANES * LANES
        feat = x_ref[:, blk:blk + LANES].astype(F32)
        if (h * hd) % LANES:
            feat = pltpu.roll(feat, LANES - (h * hd) % LANES, 1)
        terms = tuple(p.astype(F32) for p in _split3(c_all[:, h:h + 1] * log2e))
        extras = terms + (1.0, 1.0, 1.0) if c_first else (1.0, 1.0, 1.0) + tuple(-t for t in terms)
        out = jnp.where(lane < hd, feat, 0.0)
        for i, v in enumerate(extras):
            out = jnp.where(lane == hd + i, v, out)
        return out.astype(BF16)

    @pl.when(qi == 0)
    def _():
        ck_all = ck_ref[...]
        for h in range(n_heads):
            kaug_ref[h] = augment(k_ref, ck_all, h, False)

    cq_all = cq_ref[...]
    q_aug = jnp.stack([augment(q_ref, cq_all, h, True) for h in range(n_heads)])
    m_ref[...] = jnp.full(m_ref.shape, NEG_INF, F32)
    l_ref[...] = jnp.zeros(l_ref.shape, F32)
    acc_ref[...] = jnp.zeros(acc_ref.shape, F32)

    def tile(kt, masked):
        start = pl.multiple_of(kt * tq, tq)
        st = _bdot(kaug_ref[:, pl.ds(start, tq), :], q_aug, 2, 2)
        if masked:
            key = lax.broadcasted_iota(jnp.int32, (1, tq, tq), 1)
            qry = lax.broadcasted_iota(jnp.int32, (1, tq, tq), 2)
            st = jnp.where(key <= qry, st, NEG_INF)
        m_prev = m_ref[...]
        m_new = jnp.maximum(m_prev, jnp.max(st, axis=1, keepdims=True))
        alpha = jnp.exp2(m_prev - m_new)
        p = jnp.exp2(st - m_new)
        vt = vt_ref[:, pl.ds(start, tq)].reshape(n_heads, hd, tq)
        l_ref[...] = alpha * l_ref[...] + jnp.sum(p, axis=1, keepdims=True)
        acc_ref[...] = alpha * acc_ref[...] + _bdot(vt, p.astype(BF16), 2, 1)
        m_ref[...] = m_new

    def body(kt, carry):
        tile(kt, False)
        return carry

    lax.fori_loop(0, qi, body, 0)
    tile(qi, True)
    ot = (acc_ref[...] / l_ref[...]).reshape(kvw, tq)
    o_ref[...] = ot.T.astype(o_ref.dtype)


def _attn_prompt(q, k, vt, c_col, n_heads):
    b, s, kvw = q.shape
    hd = kvw // n_heads
    assert LANES % hd == 0 and hd + 6 <= LANES
    tq = min(s, 256)
    assert s % tq == 0
    return pl.pallas_call(
        functools.partial(_attn_prompt_kernel, tq=tq, n_heads=n_heads),
        grid=(b, s // tq),
        in_specs=[
            pl.BlockSpec((None, tq, kvw), lambda bi, qi: (bi, qi, 0)),
            pl.BlockSpec((None, tq, n_heads), lambda bi, qi: (bi, qi, 0)),
            pl.BlockSpec((None, s, kvw), lambda bi, qi: (bi, 0, 0)),
            pl.BlockSpec((None, s, n_heads), lambda bi, qi: (bi, 0, 0)),
            pl.BlockSpec((None, kvw, s), lambda bi, qi: (bi, 0, 0)),
        ],
        out_specs=pl.BlockSpec((None, tq, kvw), lambda bi, qi: (bi, qi, 0)),
        out_shape=jax.ShapeDtypeStruct((b, s, kvw), BF16),
        scratch_shapes=[pltpu.VMEM((n_heads, s, LANES), BF16),
                        pltpu.VMEM((n_heads, 1, tq), F32), pltpu.VMEM((n_heads, 1, tq), F32),
                        pltpu.VMEM((n_heads, hd, tq), F32)],
        compiler_params=_params(("parallel", "arbitrary"), VMEM_LIMIT),
        name="fox_prompt",
    )(q, c_col, k, c_col, vt)


def _attn_sample_kernel(q_ref, cq_ref, kc_ref, vc_ref, ckc_ref, kn_ref, vn_ref, ckn_ref,
                        o_ref, m_ref, l_ref, acc_ref, *, n_heads):
    kt = pl.program_id(1)
    last = pl.num_programs(1) - 1

    @pl.when(kt == 0)
    def _():
        m_ref[...] = jnp.full(m_ref.shape, NEG_INF, F32)
        l_ref[...] = jnp.zeros(l_ref.shape, F32)
        acc_ref[...] = jnp.zeros(acc_ref.shape, F32)

    def update(s, v, v_contract):
        m_prev = m_ref[...]
        m_new = jnp.maximum(m_prev, jnp.max(s, axis=2, keepdims=True))
        alpha = jnp.exp(m_prev - m_new)
        p = jnp.exp(s - m_new)
        l_ref[...] = alpha * l_ref[...] + jnp.sum(p, axis=2, keepdims=True)
        acc_ref[...] = alpha * acc_ref[...] + _bdot(p.astype(BF16), v, 2, v_contract)
        m_ref[...] = m_new

    q = q_ref[...]
    cq = cq_ref[...]
    kvw, tk = kc_ref.shape
    split = (n_heads, kvw // n_heads, tk)
    kk = kc_ref[...].reshape(split).astype(BF16)
    vv = vc_ref[...].reshape(split).astype(BF16)
    update(_bdot(q, kk, 2, 1) + (cq - ckc_ref[...]), vv, 2)

    @pl.when(kt == last)
    def _():
        n = q.shape[1]
        r = lax.broadcasted_iota(jnp.int32, (1, n, n), 1)
        c = lax.broadcasted_iota(jnp.int32, (1, n, n), 2)
        s = _bdot(q, kn_ref[...], 2, 2) + (cq - ckn_ref[...])
        update(jnp.where(c <= r, s, NEG_INF), vn_ref[...], 1)
        o_ref[...] = (acc_ref[...] / l_ref[...]).astype(o_ref.dtype)


def _attn_sample(q, k_new, v_new, cache_kt, cache_vt, cq, ck_cache, ck_new):
    b, n_heads, n, hd = q.shape
    kvw, p = cache_kt.shape[1:]
    tk = min(p, 1024)
    assert p % tk == 0 and kvw == n_heads * hd
    fix = lambda bi, kt: (bi, 0, 0, 0)
    rows = pl.BlockSpec((None, n_heads, n, hd), fix)
    cache = pl.BlockSpec((None, kvw, tk), lambda bi, kt: (bi, 0, kt))
    return pl.pallas_call(
        functools.partial(_attn_sample_kernel, n_heads=n_heads),
        grid=(b, p // tk),
        in_specs=[
            rows,
            pl.BlockSpec((None, n_heads, n, 1), fix),
            cache, cache,
            pl.BlockSpec((None, n_heads, 1, tk), lambda bi, kt: (bi, 0, 0, kt)),
            rows, rows,
            pl.BlockSpec((None, n_heads, 1, n), fix),
        ],
        out_specs=rows,
        out_shape=jax.ShapeDtypeStruct((b, n_heads, n, hd), BF16),
        scratch_shapes=[pltpu.VMEM((n_heads, n, 1), F32),
                        pltpu.VMEM((n_heads, n, 1), F32),
                        pltpu.VMEM((n_heads, n, hd), F32)],
        compiler_params=_params(("parallel", "arbitrary"), VMEM_LIMIT),
        name="fox_sample",
    )(q, cq, cache_kt, cache_vt, ck_cache, k_new, v_new, ck_new)


def _store_token_tiles(ref, x):
    m, d = x.shape
    nsub = d // LANES
    for s in range(nsub):
        ref[pl.ds(s, m, stride=nsub), :] = x[:, s * LANES:(s + 1) * LANES]


def _load_token_tiles(ref, m, nsub, first=0, stride=None):
    stride = nsub if stride is None else stride
    return jnp.concatenate(
        [ref[pl.ds(first + s, m, stride=stride), :] for s in range(nsub)], axis=1)


def _oproj_router_kernel(h_ref, o_ref, wo_ref, g_ref, wr_ref, h2_ref, u_ref, idx_ref, w_ref, *,
                         n_experts):
    h2 = h_ref[...] + _dot(o_ref[...], wo_ref[...])
    u = _rms_base(h2) * g_ref[...]
    h2_ref[...] = h2
    _store_token_tiles(u_ref, u)
    u_hi, u_lo = _split2(u)
    logits = _dot(u_hi, wr_ref[0]) + _dot(u_hi, wr_ref[1]) + _dot(u_lo, wr_ref[0])
    lane = lax.broadcasted_iota(jnp.int32, logits.shape, 1).astype(F32)
    logits = jnp.where(lane < n_experts, logits, NEG_INF)
    m1 = jnp.max(logits, axis=1, keepdims=True)
    i1 = jnp.min(jnp.where(logits == m1, lane, float(LANES)), axis=1, keepdims=True)
    rest = jnp.where(lane == i1, NEG_INF, logits)
    m2 = jnp.max(rest, axis=1, keepdims=True)
    i2 = jnp.min(jnp.where(rest == m2, lane, float(LANES)), axis=1, keepdims=True)
    e2 = jnp.exp(m2 - m1)
    w1 = 1.0 / (1.0 + e2)
    w2 = e2 / (1.0 + e2)
    idx_ref[...] = jnp.concatenate([i1, i2], axis=1).astype(jnp.int32)
    w_ref[...] = jnp.concatenate([w1, w2], axis=1)


def _oproj_router(h, o, w_o, g, w_router):
    t, d = h.shape
    kvw = o.shape[1]
    ne = w_router.shape[1]
    tm = min(t, 256)
    nsub = d // LANES
    assert t % tm == 0 and ne <= LANES
    wr = jnp.pad(w_router, ((0, 0), (0, LANES - ne)))
    wr_hi = wr.astype(BF16)
    wr_lo = (wr - wr_hi.astype(F32)).astype(BF16)
    row = lambda i: (i, 0)
    return pl.pallas_call(
        functools.partial(_oproj_router_kernel, n_experts=ne),
        grid=(t // tm,),
        in_specs=[
            pl.BlockSpec((tm, d), row), pl.BlockSpec((tm, kvw), row),
            pl.BlockSpec((kvw, d), lambda i: (0, 0)),
            pl.BlockSpec((1, d), lambda i: (0, 0)),
            pl.BlockSpec((2, d, LANES), lambda i: (0, 0, 0)),
        ],
        out_specs=[pl.BlockSpec((tm, d), row), pl.BlockSpec((tm * nsub, LANES), row),
                   pl.BlockSpec((tm, TOP_K), row), pl.BlockSpec((tm, TOP_K), row)],
        out_shape=[jax.ShapeDtypeStruct((t, d), F32),
                   jax.ShapeDtypeStruct((t * nsub, LANES), F32),
                   jax.ShapeDtypeStruct((t, TOP_K), jnp.int32),
                   jax.ShapeDtypeStruct((t, TOP_K), F32)],
        compiler_params=_params(("parallel",)),
        name="oproj_router",
    )(h, o, w_o.astype(BF16), g.reshape(1, d), jnp.stack([wr_hi, wr_lo]))


MOE_ROWS = 512
DMA_UNROLL = 8


def _route_plan(idx, ne, tmg):
    t = idx.shape[0]
    npairs = t * TOP_K
    e_flat = idx.reshape(npairs)
    onehot = (e_flat[:, None] == jnp.arange(ne, dtype=jnp.int32)[None, :]).astype(jnp.int32)
    csum = jnp.cumsum(onehot, axis=0)
    cnt = csum[-1]
    padded = (cnt + tmg - 1) // tmg * tmg
    ends = jnp.cumsum(padded)
    off = ends - padded
    dest = jnp.sum(onehot * (off[None, :] + csum), axis=1) - 1
    total = ends[-1]
    k = jnp.arange(tmg, dtype=jnp.int32)[None, :]
    valid = (k < (padded - cnt)[:, None]).reshape(-1)
    pad_row = ((off + cnt)[:, None] + k).reshape(-1)
    tail_row = total + jnp.cumsum(jnp.logical_not(valid).astype(jnp.int32)) - 1
    fill = jnp.where(valid, pad_row, tail_row)
    n_tiles = total // tmg
    nt_max = (npairs + ne * tmg) // tmg
    tile = jnp.minimum(jnp.arange(nt_max, dtype=jnp.int32), n_tiles - 1)
    tile_expert = jnp.sum((tile[:, None] >= (ends // tmg)[None, :]).astype(jnp.int32), axis=1)
    return dest, fill, tile_expert, n_tiles.reshape(1)


def _dispatch_kernel(dest_ref, fill_ref, u_ref, xg_hbm, zero_ref, sem, *, ch, pair_steps, nsub):
    i = pl.program_id(0)

    def issue(src_row, dst_rows):
        def body(j, carry):
            for jj in range(DMA_UNROLL):
                pltpu.make_async_copy(src_row(j, jj), xg_hbm.at[dst_rows[0, j * DMA_UNROLL + jj]],
                                      sem.at[0]).start(priority=jj % 2)
            return carry
        lax.fori_loop(0, ch // DMA_UNROLL, body, 0)

    def token_tile(j, jj):
        tok = j * (DMA_UNROLL // TOP_K) + jj // TOP_K
        return u_ref.at[pl.ds(pl.multiple_of(tok * nsub, nsub), nsub), :]

    @pl.when(i < pair_steps)
    def _():
        issue(token_tile, dest_ref)

    @pl.when(i >= pair_steps)
    def _():
        zero_ref[...] = jnp.zeros(zero_ref.shape, zero_ref.dtype)
        issue(lambda j, jj: zero_ref, fill_ref)

    pltpu.make_async_copy(xg_hbm.at[pl.ds(0, ch)], xg_hbm.at[pl.ds(0, ch)], sem.at[0]).wait()


def _dispatch(u_tiles, dest, fill, n_rows, nsub):
    t = u_tiles.shape[0] // nsub
    npairs = dest.shape[0]
    assert npairs == t * TOP_K and DMA_UNROLL % TOP_K == 0
    ch = min(2 * MOE_ROWS, npairs)
    assert npairs % ch == 0 and fill.shape[0] % ch == 0 and ch % DMA_UNROLL == 0
    pair_steps = npairs // ch
    fill_steps = fill.shape[0] // ch
    smem = pltpu.SMEM
    return pl.pallas_call(
        functools.partial(_dispatch_kernel, ch=ch, pair_steps=pair_steps, nsub=nsub),
        grid=(pair_steps + fill_steps,),
        in_specs=[
            pl.BlockSpec((None, 1, ch), lambda i: (jnp.minimum(i, pair_steps - 1), 0, 0),
                         memory_space=smem),
            pl.BlockSpec((None, 1, ch), lambda i: (jnp.maximum(i - pair_steps, 0), 0, 0),
                         memory_space=smem),
            pl.BlockSpec((ch // TOP_K * nsub, LANES),
                         lambda i: (jnp.minimum(i, pair_steps - 1), 0)),
        ],
        out_specs=pl.BlockSpec(memory_space=pl.ANY),
        out_shape=jax.ShapeDtypeStruct((n_rows, nsub, LANES), F32),
        scratch_shapes=[pltpu.VMEM((nsub, LANES), F32), pltpu.SemaphoreType.DMA((1,))],
        compiler_params=_params(("arbitrary",)),
        name="moe_dispatch",
    )(dest.reshape(pair_steps, 1, ch), fill.reshape(fill_steps, 1, ch), u_tiles)


def _moe_ffn_kernel(te_ref, nt_ref, x_ref, wgu_ref, wd_ref, o_ref, *, f, tmg, nsub):
    del te_ref
    live = pl.program_id(0) < nt_ref[0]

    @pl.when(live)
    def _():
        x = _load_token_tiles(x_ref, tmg, nsub).astype(BF16)
        _store_token_tiles(o_ref, _swiglu(x, wgu_ref, wd_ref, f))

    @pl.when(jnp.logical_not(live))
    def _():
        o_ref[...] = jnp.zeros(o_ref.shape, o_ref.dtype)


def _moe_ffn(xg, tile_expert, n_tiles, w_gu, w_down, tmg):
    ne, f, d = w_down.shape
    nsub = d // LANES
    nt_max = tile_expert.shape[0]
    assert xg.shape[0] == nt_max * tmg * nsub
    rows = lambda i, te, nt: (jnp.minimum(i, nt[0] - 1), 0)
    return pl.pallas_call(
        functools.partial(_moe_ffn_kernel, f=f, tmg=tmg, nsub=nsub),
        grid_spec=pltpu.PrefetchScalarGridSpec(
            num_scalar_prefetch=2,
            grid=(nt_max,),
            in_specs=[
                pl.BlockSpec((tmg * nsub, LANES), rows),
                pl.BlockSpec((None, d, 2 * f), lambda i, te, nt: (te[i], 0, 0)),
                pl.BlockSpec((None, f, d), lambda i, te, nt: (te[i], 0, 0)),
            ],
            out_specs=pl.BlockSpec((tmg * nsub, LANES), lambda i, te, nt: (i, 0)),
        ),
        out_shape=jax.ShapeDtypeStruct(xg.shape, F32),
        compiler_params=_params(("arbitrary",), VMEM_LIMIT),
        name="moe_experts",
    )(tile_expert, n_tiles, xg, w_gu.astype(BF16), w_down.astype(BF16))


def _combine_kernel(dcur_ref, dnxt_ref, h_ref, w_ref, og_hbm, y_ref, buf, sem, *, tm, nsub):
    i = pl.program_id(0)
    n = pl.num_programs(0)
    rows = TOP_K * tm

    def issue(d_ref, slot):
        def body(j, carry):
            for jj in range(DMA_UNROLL):
                r = j * DMA_UNROLL + jj
                dst = buf.at[slot, pl.ds(pl.multiple_of(r * nsub, nsub), nsub), :]
                pltpu.make_async_copy(og_hbm.at[d_ref[0, r]], dst,
                                      sem.at[slot]).start(priority=jj % 2)
            return carry
        lax.fori_loop(0, rows // DMA_UNROLL, body, 0)

    @pl.when(i == 0)
    def _():
        issue(dcur_ref, 0)

    @pl.when(i + 1 < n)
    def _():
        issue(dnxt_ref, lax.rem(i + 1, 2))

    slot = lax.rem(i, 2)
    pltpu.make_async_copy(buf.at[slot], buf.at[slot], sem.at[slot]).wait()
    pair = TOP_K * nsub
    a = _load_token_tiles(buf.at[slot], tm, nsub, first=0, stride=pair)
    b = _load_token_tiles(buf.at[slot], tm, nsub, first=nsub, stride=pair)
    w = w_ref[...]
    y_ref[...] = h_ref[...] + (w[:, 0:1] * a + w[:, 1:2] * b)


def _combine(h, w, og, dest):
    t, d = h.shape
    nsub = d // LANES
    tm = min(t, 256)
    assert t % tm == 0 and (TOP_K * tm) % DMA_UNROLL == 0
    n = t // tm
    dest3 = dest.reshape(n, 1, TOP_K * tm)
    smem = pltpu.SMEM
    return pl.pallas_call(
        functools.partial(_combine_kernel, tm=tm, nsub=nsub),
        grid=(n,),
        in_specs=[
            pl.BlockSpec((None, 1, TOP_K * tm), lambda i: (i, 0, 0), memory_space=smem),
            pl.BlockSpec((None, 1, TOP_K * tm), lambda i: (jnp.minimum(i + 1, n - 1), 0, 0),
                         memory_space=smem),
            pl.BlockSpec((tm, d), lambda i: (i, 0)),
            pl.BlockSpec((tm, TOP_K), lambda i: (i, 0)),
            pl.BlockSpec(memory_space=pl.ANY),
        ],
        out_specs=pl.BlockSpec((tm, d), lambda i: (i, 0)),
        out_shape=jax.ShapeDtypeStruct((t, d), F32),
        scratch_shapes=[pltpu.VMEM((2, TOP_K * tm * nsub, LANES), F32),
                        pltpu.SemaphoreType.DMA((2,))],
        compiler_params=_params(("arbitrary",)),
        name="moe_combine",
    )(dest3, dest3, h, w, og)


def _moe(h, u_tiles, idx, w, w_gu, w_down):
    t, d = h.shape
    ne = w_down.shape[0]
    nsub = d // LANES
    npairs = t * TOP_K
    tmg = MOE_ROWS if npairs >= 32 * MOE_ROWS else min(MOE_ROWS // 2, npairs)
    dest, fill, tile_expert, n_tiles = _route_plan(idx, ne, tmg)
    n_rows = npairs + ne * tmg
    xg = _dispatch(u_tiles, dest, fill, n_rows, nsub)
    og = _moe_ffn(xg.reshape(n_rows * nsub, LANES), tile_expert, n_tiles, w_gu, w_down, tmg)
    return _combine(h, w, og.reshape(n_rows, nsub, LANES), dest)


def _trunk(x, pool_hist, kv_cache, p):
    b, n, d = x.shape
    n_heads = p["b_f"].shape[0]
    start_pos = 0 if kv_cache is None else kv_cache[0].shape[1]

    pool_args = (p["g_pool_norm"][0], p["w_pool"][0], p["pool_scale"][0], start_pos)
    ffn_args = (p["g_ffn_norm"][0], p["w_ffn_gu"][0], p["w_ffn_down"][0])
    if n >= 256:
        h, pool_state = _pool_layer(x, None if pool_hist is None else pool_hist[0], *pool_args,
                                    ffn=ffn_args)
        h = h.reshape(b * n, d)
    else:
        h, pool_state = _pool_layer(x, None if pool_hist is None else pool_hist[0], *pool_args)
        h = _dense_ffn(h.reshape(b * n, d), *ffn_args)

    qkv_args = (p["g_kv_norm"], p["g_attn_norm"][0], p["w_kvf"], p["b_f"], p["g_k"],
                p["w_q"][0], p["g_q"][0], n_heads)
    kvw = p["w_q"].shape[2]
    hd = kvw // n_heads
    if kv_cache is None:
        kt, vt, lft, ct, kb, vtb, qb = _qkv_proj_t(h, b, *qkv_args)
        k = jnp.transpose(kt.reshape(b, n_heads, hd, n), (0, 3, 1, 2))
        v = jnp.transpose(vt.reshape(b, n_heads, hd, n), (0, 3, 1, 2))
        logf = jnp.transpose(lft, (0, 2, 1))
        o = _attn_prompt(qb.reshape(b, n, kvw), kb.reshape(b, n, kvw), vtb,
                         jnp.transpose(ct, (0, 2, 1)), n_heads)
    else:
        k, v, logf, kb, vb, qb = _qkv_proj(h, *qkv_args)
        logf = logf.reshape(b, n, n_heads)
        qb, kb, vb = (a.reshape(b, n, kvw) for a in (qb, kb, vb))
        k = k.reshape(b, n, n_heads, hd)
        v = v.reshape(b, n, n_heads, hd)
        cache_k, cache_v, cache_logf = kv_cache
        past = cache_k.shape[1]
        lf_all = jnp.concatenate([cache_logf.astype(F32), logf], axis=1)
        total = past + n
        padded = -(-total // LANES) * LANES
        lf_t = jnp.transpose(lf_all, (0, 2, 1)).reshape(b * n_heads, total)
        c_t = _cumsum_lanes(jnp.pad(lf_t, ((0, 0), (0, padded - total))))
        c_t = c_t.reshape(b, n_heads, padded)
        c_new = c_t[:, :, past:total]
        cache_kt = jnp.transpose(cache_k, (0, 2, 3, 1)).reshape(b, kvw, past)
        cache_vt = jnp.transpose(cache_v, (0, 2, 3, 1)).reshape(b, kvw, past)
        heads = lambda a: jnp.transpose(a.reshape(b, n, n_heads, hd), (0, 2, 1, 3))
        o = _attn_sample(heads(qb), heads(kb), heads(vb), cache_kt, cache_vt, c_new[..., None],
                         c_t[:, :, None, :past], c_new[:, :, None, :])
        o = jnp.transpose(o, (0, 2, 1, 3))

    h2, u_tiles, idx, top_w = _oproj_router(h, o.reshape(b * n, kvw), p["w_o"][0],
                                            p["g_ffn_norm"][1], p["w_router"][0])
    y = _moe(h2, u_tiles, idx, top_w, p["w_moe_gu"][0], p["w_moe_down"][0])
    return y.reshape(b, n, d), pool_state[None], k, v, logf


def kernel(x_prompt, x_sample, state_pool, cache_k, cache_v, cache_logf, g_pool_norm, w_pool,
           pool_scale, g_kv_norm, w_kvf, b_f, g_k, g_attn_norm, w_q, g_q, w_o, g_ffn_norm,
           w_ffn_gu, w_ffn_down, w_router, w_moe_gu, w_moe_down):
    p = dict(g_pool_norm=g_pool_norm, w_pool=w_pool, pool_scale=pool_scale,
             g_kv_norm=g_kv_norm, w_kvf=w_kvf, b_f=b_f, g_k=g_k, g_attn_norm=g_attn_norm,
             w_q=w_q, g_q=g_q, w_o=w_o, g_ffn_norm=g_ffn_norm, w_ffn_gu=w_ffn_gu,
             w_ffn_down=w_ffn_down, w_router=w_router, w_moe_gu=w_moe_gu,
             w_moe_down=w_moe_down)
    assert w_pool.shape[0] == 1 and w_q.shape[0] == 1 and w_router.shape[0] == 1
    y_p, pool_p, k_p, v_p, lf_p = _trunk(x_prompt, None, None, p)
    y_s, pool_s, k_s, v_s, lf_s = _trunk(
        x_sample, state_pool, (cache_k, cache_v, cache_logf), p)
    return (y_p, y_s, pool_p, k_p, v_p, lf_p, pool_s, k_s, v_s, lf_s)
```

```python
import functools
import math

import jax
import jax.numpy as jnp
from jax import lax
from jax.experimental import pallas as pl
from jax.experimental.pallas import tpu as pltpu

EPS = 1e-6
POOL_WINDOWS = (2, 4, 8, 16)
HALO = max(POOL_WINDOWS)
POOL_HIST = HALO - 1
TOP_K = 2
LANES = 128
BF16_ROWS = 16
MXU_DIM = 256
VMEM_LIMIT = 56 * 1024 * 1024
NEG_INF = float("-inf")

BF16 = jnp.bfloat16
F32 = jnp.float32


def _params(semantics, vmem=None):
    return pltpu.CompilerParams(dimension_semantics=semantics, vmem_limit_bytes=vmem)


def _dot(a, b):
    return jnp.dot(a, b, preferred_element_type=F32)


def _dot_nt(a, b):
    return lax.dot_general(a, b, (((1,), (1,)), ((), ())), preferred_element_type=F32)


def _rms_base(x):
    return x * lax.rsqrt(jnp.mean(x * x, axis=-1, keepdims=True) + EPS)


def _split2(x):
    hi = x.astype(BF16)
    lo = (x - hi.astype(F32)).astype(BF16)
    return hi, lo


def _split3(x):
    hi = x.astype(BF16)
    r = x - hi.astype(F32)
    mid = r.astype(BF16)
    lo = (r - mid.astype(F32)).astype(BF16)
    return hi, mid, lo


def _ffn_chunks(f):
    out, s = [], 0
    while s < f:
        n = min(4 * MXU_DIM, f - s)
        out.append((s, n))
        s += n
    return out


def _pool_kernel(x_ref, halo_ref, g_ref, w_ref, scale_ref, *rest, tm, start_pos, halo_is_normed,
                 f):
    if f is None:
        h_ref, st_ref = rest
    else:
        gf_ref, wgu_ref, wd_ref, h_ref, st_ref = rest
    i = pl.program_id(1)
    x = x_ref[...]
    g = g_ref[...]
    u = _rms_base(x) * g
    if halo_is_normed:
        uh = halo_ref[...]
    else:
        uh = _rms_base(halo_ref[...]) * g
        uh = jnp.where(i > 0, uh, 0.0)
    ext = jnp.concatenate([uh, u], axis=0)

    row = lax.broadcasted_iota(jnp.int32, (tm, 1), 0)
    avail = start_pos + i * tm + row + 1
    group = x.shape[1] // len(POOL_WINDOWS)
    ys = []
    for gi, w in enumerate(POOL_WINDOWS):
        lo, hi = gi * group, (gi + 1) * group
        s = ext[:, lo:hi]
        step = 1
        while step < w:
            s = s + pltpu.roll(s, step, 0)
            step *= 2
        inv_cnt = 1.0 / jnp.minimum(avail, w).astype(F32)
        pooled = s[HALO:, :] * inv_cnt
        diff = pooled - u[:, lo:hi]
        ys.append(_dot(diff.astype(BF16), w_ref[gi]))
    h = x + jnp.concatenate(ys, axis=1) * scale_ref[...]
    if f is not None:
        h = h + _swiglu((_rms_base(h) * gf_ref[...]).astype(BF16), wgu_ref, wd_ref, f)
    h_ref[...] = h

    @pl.when(i == pl.num_programs(1) - 1)
    def _():
        st_ref[...] = u[tm - HALO:, :]


def _pool_layer(x, hist, g, w_pool, pool_scale, start_pos, ffn=None):
    b, n, d = x.shape
    tm = min(n, 256 if ffn is None else 512)
    assert n % tm == 0 and tm % HALO == 0 and n >= HALO
    nt = n // tm
    if hist is None:
        halo_arr = x
        halo_spec = pl.BlockSpec(
            (None, HALO, d), lambda bi, i: (bi, jnp.maximum(i * (tm // HALO) - 1, 0), 0))
    else:
        assert nt == 1
        halo_arr = jnp.pad(hist, ((0, 0), (HALO - POOL_HIST, 0), (0, 0)))
        halo_spec = pl.BlockSpec((None, HALO, d), lambda bi, i: (bi, 0, 0))
    ng = len(POOL_WINDOWS)
    vec = pl.BlockSpec((1, d), lambda bi, i: (0, 0))
    in_specs = [pl.BlockSpec((None, tm, d), lambda bi, i: (bi, i, 0)), halo_spec, vec,
                pl.BlockSpec((ng, d // ng, d // ng), lambda bi, i: (0, 0, 0)), vec]
    args = [x, halo_arr, g.reshape(1, d), w_pool.astype(BF16), pool_scale.reshape(1, d)]
    f = None
    if ffn is not None:
        g_ffn, w_gu, w_down = ffn
        f = w_down.shape[0]
        in_specs += [vec, pl.BlockSpec((d, 2 * f), lambda bi, i: (0, 0)),
                     pl.BlockSpec((f, d), lambda bi, i: (0, 0))]
        args += [g_ffn.reshape(1, d), w_gu.astype(BF16), w_down.astype(BF16)]
    h, st = pl.pallas_call(
        functools.partial(_pool_kernel, tm=tm, start_pos=start_pos,
                          halo_is_normed=hist is not None, f=f),
        grid=(b, nt),
        in_specs=in_specs,
        out_specs=[
            pl.BlockSpec((None, tm, d), lambda bi, i: (bi, i, 0)),
            pl.BlockSpec((None, HALO, d), lambda bi, i: (bi, 0, 0)),
        ],
        out_shape=[jax.ShapeDtypeStruct((b, n, d), F32),
                   jax.ShapeDtypeStruct((b, HALO, d), F32)],
        compiler_params=_params(("parallel", "arbitrary"), None if ffn is None else VMEM_LIMIT),
        name="pool_mixer" if ffn is None else "pool_ffn",
    )(*args)
    return h, st[:, HALO - POOL_HIST:, :]


def _swiglu(xb, wgu_ref, wd_ref, f):
    acc = None
    for s, n in _ffn_chunks(f):
        gate = _dot(xb, wgu_ref[:, s:s + n])
        up = _dot(xb, wgu_ref[:, f + s:f + s + n])
        act = (gate * jax.nn.sigmoid(gate)) * up
        part = _dot(act.astype(BF16), wd_ref[s:s + n, :])
        acc = part if acc is None else acc + part
    return acc


def _dense_ffn_kernel(h_ref, g_ref, wgu_ref, wd_ref, o_ref, *, f):
    h = h_ref[...]
    u = (_rms_base(h) * g_ref[...]).astype(BF16)
    o_ref[...] = h + _swiglu(u, wgu_ref, wd_ref, f)


def _dense_ffn(h, g, w_gu, w_down):
    t, d = h.shape
    f = w_down.shape[0]
    tm = min(t, 512)
    assert t % tm == 0
    return pl.pallas_call(
        functools.partial(_dense_ffn_kernel, f=f),
        grid=(t // tm,),
        in_specs=[
            pl.BlockSpec((tm, d), lambda i: (i, 0)),
            pl.BlockSpec((1, d), lambda i: (0, 0)),
            pl.BlockSpec((d, 2 * f), lambda i: (0, 0)),
            pl.BlockSpec((f, d), lambda i: (0, 0)),
        ],
        out_specs=pl.BlockSpec((tm, d), lambda i: (i, 0)),
        out_shape=jax.ShapeDtypeStruct((t, d), F32),
        compiler_params=_params(("parallel",), VMEM_LIMIT),
        name="dense_ffn",
    )(h, g.reshape(1, d), w_gu.astype(BF16), w_down.astype(BF16))


def _head_norm(x, bd_ref, g):
    sq_hi, sq_lo = _split2(x * x)
    n = bd_ref.shape[0]
    ms = jnp.concatenate(
        [_dot(sq_hi[:, c:c + n], bd_ref[...]) + _dot(sq_lo[:, c:c + n], bd_ref[...])
         for c in range(0, x.shape[1], n)], axis=1)
    return x * lax.rsqrt(ms + EPS) * g


def _log_sigmoid(z):
    return jnp.minimum(z, 0.0) - jnp.log1p(jnp.exp(-jnp.abs(z)))


def _qkv_kernel(h_ref, gkv_ref, gat_ref, wk_ref, wv_ref, wf_ref, wq_ref, bf_ref,
                gk_ref, gq_ref, bd_ref,
                k_ref, v_ref, lf_ref, kb_ref, vb_ref, qb_ref, *, n_heads, q_scale):
    base = _rms_base(h_ref[...])
    a_kv = (base * gkv_ref[...]).astype(BF16)
    a_q = (base * gat_ref[...]).astype(BF16)
    k = _head_norm(_dot(a_kv, wk_ref[...]), bd_ref, gk_ref[...])
    v = _dot(a_kv, wv_ref[...])
    z = _dot(a_kv, wf_ref[...])[:, :n_heads] + bf_ref[...]
    q = _head_norm(_dot(a_q, wq_ref[...]), bd_ref, gq_ref[...])
    k_ref[...] = k
    v_ref[...] = v
    lf_ref[...] = _log_sigmoid(z)
    kb_ref[...] = k.astype(BF16)
    vb_ref[...] = v.astype(BF16)
    qb_ref[...] = (q * q_scale).astype(BF16)


def _qkv_proj(h, g_kv, g_attn, w_kvf, b_f, g_k, w_q, g_q, n_heads):
    t, d = h.shape
    kvw = w_q.shape[1]
    hd = kvw // n_heads
    tm = min(t, 256)
    assert t % tm == 0 and MXU_DIM % hd == 0 and kvw % MXU_DIM == 0
    wk = w_kvf[:, :kvw].astype(BF16)
    wv = w_kvf[:, kvw:2 * kvw].astype(BF16)
    wf = jnp.pad(w_kvf[:, 2 * kvw:], ((0, 0), (0, LANES - n_heads))).astype(BF16)
    head = jnp.arange(MXU_DIM) // hd
    bd = ((head[:, None] == head[None, :]).astype(F32) / hd).astype(BF16)
    row = lambda i: (i, 0)
    fix = lambda i: (0, 0)
    return pl.pallas_call(
        functools.partial(_qkv_kernel, n_heads=n_heads, q_scale=hd ** -0.5),
        grid=(t // tm,),
        in_specs=[
            pl.BlockSpec((tm, d), row),
            pl.BlockSpec((1, d), fix), pl.BlockSpec((1, d), fix),
            pl.BlockSpec((d, kvw), fix), pl.BlockSpec((d, kvw), fix),
            pl.BlockSpec((d, LANES), fix), pl.BlockSpec((d, kvw), fix),
            pl.BlockSpec((1, n_heads), fix),
            pl.BlockSpec((1, kvw), fix), pl.BlockSpec((1, kvw), fix),
            pl.BlockSpec((MXU_DIM, MXU_DIM), fix),
        ],
        out_specs=[
            pl.BlockSpec((tm, kvw), row), pl.BlockSpec((tm, kvw), row),
            pl.BlockSpec((tm, n_heads), row),
            pl.BlockSpec((tm, kvw), row), pl.BlockSpec((tm, kvw), row),
            pl.BlockSpec((tm, kvw), row),
        ],
        out_shape=[
            jax.ShapeDtypeStruct((t, kvw), F32), jax.ShapeDtypeStruct((t, kvw), F32),
            jax.ShapeDtypeStruct((t, n_heads), F32),
            jax.ShapeDtypeStruct((t, kvw), BF16), jax.ShapeDtypeStruct((t, kvw), BF16),
            jax.ShapeDtypeStruct((t, kvw), BF16),
        ],
        compiler_params=_params(("parallel",), VMEM_LIMIT),
        name="qkv_proj",
    )(h, g_kv.reshape(1, d), g_attn.reshape(1, d), wk, wv, wf, w_q.astype(BF16),
      b_f.reshape(1, n_heads), jnp.tile(g_k, n_heads).reshape(1, kvw),
      jnp.tile(g_q, n_heads).reshape(1, kvw), bd)


def _cumsum_tile(x, tri, carry):
    y = carry
    for part in _split3(x):
        y = y + _dot(part, tri)
    return y


def _qkv_t_kernel(h_ref, gkv_ref, gat_ref, wkt_ref, wvt_ref, wft_ref, wq_ref, bf_ref,
                  gk_ref, gq_ref, bd_ref, tri_ref,
                  kt_ref, vt_ref, lft_ref, ct_ref, kb_ref, vtb_ref, qb_ref, carry_ref, *,
                  n_heads, q_scale):
    @pl.when(pl.program_id(1) == 0)
    def _():
        carry_ref[...] = jnp.zeros(carry_ref.shape, F32)

    base = _rms_base(h_ref[...])
    a_kv = (base * gkv_ref[...]).astype(BF16)
    a_q = (base * gat_ref[...]).astype(BF16)
    tm = a_kv.shape[0]
    kraw = _dot_nt(wkt_ref[...], a_kv)
    k3 = kraw.reshape(n_heads, kraw.shape[0] // n_heads, tm)
    ms = jnp.mean(k3 * k3, axis=1, keepdims=True)
    kt = (k3 * lax.rsqrt(ms + EPS) * gk_ref[...][None]).reshape(kraw.shape)
    vt = _dot_nt(wvt_ref[...], a_kv)
    lft = _log_sigmoid(_dot_nt(wft_ref[...], a_kv) + bf_ref[...])
    ct = _cumsum_tile(lft, tri_ref[...], carry_ref[...])
    carry_ref[...] = ct[:, tm - 1:tm]
    q = _head_norm(_dot(a_q, wq_ref[...]), bd_ref, gq_ref[...])
    kt_ref[...] = kt
    vt_ref[...] = vt
    lft_ref[...] = lft
    ct_ref[...] = ct
    kb_ref[...] = kt.T.astype(BF16)
    vtb_ref[...] = vt.astype(BF16)
    qb_ref[...] = (q * q_scale).astype(BF16)


def _qkv_proj_t(h, b, g_kv, g_attn, w_kvf, b_f, g_k, w_q, g_q, n_heads):
    t, d = h.shape
    s = t // b
    kvw = w_q.shape[1]
    hd = kvw // n_heads
    tm = min(s, 512)
    nt = s // tm
    assert s % tm == 0 and MXU_DIM % hd == 0 and kvw % MXU_DIM == 0
    wkt =w_kvf[:, :kvw].T.astype(BF16)
    wvt = w_kvf[:, kvw:2 * kvw].T.astype(BF16)
    wft = w_kvf[:, 2 * kvw:].T.astype(BF16)
    head = jnp.arange(MXU_DIM) // hd
    bd = ((head[:, None] == head[None, :]).astype(F32) / hd).astype(BF16)
    pos = jnp.arange(tm)
    tri = (pos[:, None] <= pos[None, :]).astype(BF16)
    fix = lambda bi, i: (0, 0)
    feat = lambda rows: pl.BlockSpec((None, rows, tm), lambda bi, i: (bi, 0, i))
    f32 = lambda rows: jax.ShapeDtypeStruct((b, rows, s), F32)
    tok = pl.BlockSpec((tm, kvw), lambda bi, i: (bi * nt + i, 0))
    return pl.pallas_call(
        functools.partial(_qkv_t_kernel, n_heads=n_heads,
                          q_scale=hd ** -0.5 * math.log2(math.e)),
        grid=(b, nt),
        in_specs=[
            pl.BlockSpec((tm, d), lambda bi, i: (bi * nt + i, 0)),
            pl.BlockSpec((1, d), fix), pl.BlockSpec((1, d), fix),
            pl.BlockSpec((kvw, d), fix), pl.BlockSpec((kvw, d), fix),
            pl.BlockSpec((n_heads, d), fix), pl.BlockSpec((d, kvw), fix),
            pl.BlockSpec((n_heads, tm), fix), pl.BlockSpec((hd, tm), fix),
            pl.BlockSpec((1, kvw), fix), pl.BlockSpec((MXU_DIM, MXU_DIM), fix),
            pl.BlockSpec((tm, tm), fix),
        ],
        out_specs=[feat(kvw), feat(kvw), feat(n_heads), feat(n_heads), tok, feat(kvw), tok],
        out_shape=[f32(kvw), f32(kvw), f32(n_heads), f32(n_heads),
                   jax.ShapeDtypeStruct((t, kvw), BF16),
                   jax.ShapeDtypeStruct((b, kvw, s), BF16),
                   jax.ShapeDtypeStruct((t, kvw), BF16)],
        scratch_shapes=[pltpu.VMEM((n_heads, 1), F32)],
        compiler_params=_params(("parallel", "arbitrary"), VMEM_LIMIT),
        name="qkv_proj_t",
    )(h, g_kv.reshape(1, d), g_attn.reshape(1, d), wkt, wvt, wft, w_q.astype(BF16),
      jnp.broadcast_to(b_f[:, None], (n_heads, tm)), jnp.broadcast_to(g_k[:, None], (hd, tm)),
      jnp.tile(g_q, n_heads).reshape(1, kvw), bd, tri)


def _cumsum_kernel(x_ref, o_ref):
    r, l = x_ref.shape
    ii = lax.broadcasted_iota(jnp.int32, (LANES, LANES), 0)
    jj = lax.broadcasted_iota(jnp.int32, (LANES, LANES), 1)
    tri = (ii <= jj).astype(BF16)
    carry = jnp.zeros((r, 1), F32)
    for c in range(0, l, LANES):
        y = _cumsum_tile(x_ref[:, c:c + LANES], tri, carry)
        o_ref[:, c:c + LANES] = y
        carry = y[:, LANES - 1:LANES]


def _cumsum_lanes(x):
    r, l = x.shape
    tr = min(r, 256)
    assert r % tr == 0 and l % LANES == 0
    return pl.pallas_call(
        _cumsum_kernel,
        grid=(r // tr,),
        in_specs=[pl.BlockSpec((tr, l), lambda i: (i, 0))],
        out_specs=pl.BlockSpec((tr, l), lambda i: (i, 0)),
        out_shape=jax.ShapeDtypeStruct((r, l), F32),
        compiler_params=_params(("parallel",)),
        name="logf_cumsum",
    )(x)


def _bdot(a, b, ca, cb):
    return lax.dot_general(a, b, (((ca,), (cb,)), ((0,), (0,))), preferred_element_type=F32)


def _attn_prompt_kernel(q_ref, cq_ref, k_ref, ck_ref, vt_ref, o_ref,
                        kaug_ref, m_ref, acc_ref, *, tq, n_heads):
    qi = pl.program_id(1)
    log2e = math.log2(math.e)
    kvw = q_ref.shape[1]
    hd = kvw // n_heads
    lane = lax.broadcasted_iota(jnp.int32, (1, LANES), 1)

    def augment(x_ref, c_all, h, c_first):
        blk = h * hd // LANES * LANES
        lo = h * hd - blk
        terms = tuple(p.astype(F32) for p in _split3(c_all[:, h:h + 1] * log2e))
        extras = terms + (1.0, 1.0, 1.0) if c_first else (1.0, 1.0, 1.0) + tuple(-t for t in terms)
        ext = jnp.zeros((c_all.shape[0], LANES), F32)
        for i, v in enumerate(extras):
            ext = jnp.where(lane == (lo + hd + i) % LANES, v, ext)
        own = jnp.logical_and(lane >= lo, lane < lo + hd)
        return jnp.where(own, x_ref[:, blk:blk + LANES], ext.astype(BF16))

    @pl.when(qi == 0)
    def _():
        ck_all = ck_ref[...]
        for h in range(n_heads):
            kaug_ref[h] = augment(k_ref, ck_all, h, False)

    cq_all = cq_ref[...]
    q_aug = jnp.stack([augment(q_ref, cq_all, h, True) for h in range(n_heads)])
    m_ref[...] = jnp.full(m_ref.shape, NEG_INF, F32)
    acc_ref[...] = jnp.zeros(acc_ref.shape, F32)
    n_den = acc_ref.shape[1] - hd
    den_rows = (lax.broadcasted_iota(jnp.int32, (n_heads, n_den, tq), 1) == 0).astype(BF16)

    def tile(kt, masked):
        start = pl.multiple_of(kt * tq, tq)
        st = _bdot(kaug_ref[:, pl.ds(start, tq), :], q_aug, 2, 2)
        if masked:
            key = lax.broadcasted_iota(jnp.int32, (1, tq, tq), 1)
            qry = lax.broadcasted_iota(jnp.int32, (1, tq, tq), 2)
            st = jnp.where(key <= qry, st, NEG_INF)
        m_prev = m_ref[...]
        m_new = jnp.maximum(m_prev, jnp.max(st, axis=1, keepdims=True))
        alpha = jnp.exp2(m_prev - m_new)
        p = jnp.exp2(st - m_new)
        vt = vt_ref[:, pl.ds(start, tq)].reshape(n_heads, hd, tq)
        vt = jnp.concatenate([vt, den_rows], axis=1)
        acc_ref[...] = alpha * acc_ref[...] + _bdot(vt, p.astype(BF16), 2, 1)
        m_ref[...] = m_new

    def body(kt, carry):
        tile(kt, False)
        return carry

    lax.fori_loop(0, qi, body, 0)
    tile(qi, True)
    ot = (acc_ref[:, :hd, :] / acc_ref[:, hd:hd + 1, :]).reshape(kvw, tq)
    o_ref[...] = ot.T.astype(o_ref.dtype)


def _attn_prompt(q, k, vt, c_col, n_heads):
    b, s, kvw = q.shape
    hd = kvw // n_heads
    assert LANES % hd == 0 and hd + 6 <= LANES
    tq = min(s, 256)
    assert s % tq == 0
    return pl.pallas_call(
        functools.partial(_attn_prompt_kernel, tq=tq, n_heads=n_heads),
        grid=(b, s // tq),
        in_specs=[
            pl.BlockSpec((None, tq, kvw), lambda bi, qi: (bi, qi, 0)),
            pl.BlockSpec((None, tq, n_heads), lambda bi, qi: (bi, qi, 0)),
            pl.BlockSpec((None, s, kvw), lambda bi, qi: (bi, 0, 0)),
            pl.BlockSpec((None, s, n_heads), lambda bi, qi: (bi, 0, 0)),
            pl.BlockSpec((None, kvw, s), lambda bi, qi: (bi, 0, 0)),
        ],
        out_specs=pl.BlockSpec((None, tq, kvw), lambda bi, qi: (bi, qi, 0)),
        out_shape=jax.ShapeDtypeStruct((b, s, kvw), BF16),
        scratch_shapes=[pltpu.VMEM((n_heads, s, LANES), BF16),
                        pltpu.VMEM((n_heads, 1, tq), F32),
                        pltpu.VMEM((n_heads, hd + BF16_ROWS, tq), F32)],
        compiler_params=_params(("parallel", "arbitrary"), VMEM_LIMIT),
        name="fox_prompt",
    )(q, c_col, k, c_col, vt)


def _attn_sample_kernel(q_ref, cq_ref, kc_ref, vc_ref, ckc_ref, kn_ref, vn_ref, ckn_ref,
                        o_ref, m_ref, l_ref, acc_ref, *, n_heads):
    kt = pl.program_id(1)
    last = pl.num_programs(1) - 1

    @pl.when(kt == 0)
    def _():
        m_ref[...] = jnp.full(m_ref.shape, NEG_INF, F32)
        l_ref[...] = jnp.zeros(l_ref.shape, F32)
        acc_ref[...] = jnp.zeros(acc_ref.shape, F32)

    def update(s, v, v_contract):
        m_prev = m_ref[...]
        m_new = jnp.maximum(m_prev, jnp.max(s, axis=2, keepdims=True))
        alpha = jnp.exp(m_prev - m_new)
        p = jnp.exp(s - m_new)
        l_ref[...] = alpha * l_ref[...] + jnp.sum(p, axis=2, keepdims=True)
        acc_ref[...] = alpha * acc_ref[...] + _bdot(p.astype(BF16), v, 2, v_contract)
        m_ref[...] = m_new

    q = q_ref[...]
    cq = cq_ref[...]
    kvw, tk = kc_ref.shape
    split = (n_heads, kvw // n_heads, tk)
    kk = kc_ref[...].reshape(split).astype(BF16)
    vv = vc_ref[...].reshape(split).astype(BF16)
    update(_bdot(q, kk, 2, 1) + (cq - ckc_ref[...]), vv, 2)

    @pl.when(kt == last)
    def _():
        n = q.shape[1]
        r = lax.broadcasted_iota(jnp.int32, (1, n, n), 1)
        c = lax.broadcasted_iota(jnp.int32, (1, n, n), 2)
        s = _bdot(q, kn_ref[...], 2, 2) + (cq - ckn_ref[...])
        update(jnp.where(c <= r, s, NEG_INF), vn_ref[...], 1)
        o_ref[...] = (acc_ref[...] / l_ref[...]).astype(o_ref.dtype)


def _attn_sample(q, k_new, v_new, cache_kt, cache_vt, cq, ck_cache, ck_new):
    b, n_heads, n, hd = q.shape
    kvw, p = cache_kt.shape[1:]
    tk = min(p, 1024)
    assert p % tk == 0 and kvw == n_heads * hd
    fix = lambda bi, kt: (bi, 0, 0, 0)
    rows = pl.BlockSpec((None, n_heads, n, hd), fix)
    cache = pl.BlockSpec((None, kvw, tk), lambda bi, kt: (bi, 0, kt))
    return pl.pallas_call(
        functools.partial(_attn_sample_kernel, n_heads=n_heads),
        grid=(b, p // tk),
        in_specs=[
            rows,
            pl.BlockSpec((None, n_heads, n, 1), fix),
            cache, cache,
            pl.BlockSpec((None, n_heads, 1, tk), lambda bi, kt: (bi, 0, 0, kt)),
            rows, rows,
            pl.BlockSpec((None, n_heads, 1, n), fix),
        ],
        out_specs=rows,
        out_shape=jax.ShapeDtypeStruct((b, n_heads, n, hd), BF16),
        scratch_shapes=[pltpu.VMEM((n_heads, n, 1), F32),
                        pltpu.VMEM((n_heads, n, 1), F32),
                        pltpu.VMEM((n_heads, n, hd), F32)],
        compiler_params=_params(("parallel", "arbitrary"), VMEM_LIMIT),
        name="fox_sample",
    )(q, cq, cache_kt, cache_vt, ck_cache, k_new, v_new, ck_new)


def _store_token_tiles(ref, x):
    m, d = x.shape
    nsub = d // LANES
    for s in range(nsub):
        ref[pl.ds(s, m, stride=nsub), :] = x[:, s * LANES:(s + 1) * LANES]


def _load_token_tiles(ref, m, nsub, first=0, stride=None):
    stride = nsub if stride is None else stride
    return jnp.concatenate(
        [ref[pl.ds(first + s, m, stride=stride), :] for s in range(nsub)], axis=1)


def _oproj_router_kernel(h_ref, o_ref, wo_ref, g_ref, wr_ref, h2_ref, u_ref, idx_ref, w_ref, *,
                         n_experts):
    h2 = h_ref[...] + _dot(o_ref[...], wo_ref[...])
    u = _rms_base(h2) * g_ref[...]
    h2_ref[...] = h2
    _store_token_tiles(u_ref, u)
    u_hi, u_lo = _split2(u)
    both = _dot(u_hi, wr_ref[...])
    logits = both[:, :LANES] + both[:, LANES:] + _dot(u_lo, wr_ref[:, :LANES])
    lane = lax.broadcasted_iota(jnp.int32, logits.shape, 1).astype(F32)
    logits = jnp.where(lane < n_experts, logits, NEG_INF)
    m1 = jnp.max(logits, axis=1, keepdims=True)
    i1 = jnp.min(jnp.where(logits == m1, lane, float(LANES)), axis=1, keepdims=True)
    rest = jnp.where(lane == i1, NEG_INF, logits)
    m2 = jnp.max(rest, axis=1, keepdims=True)
    i2 = jnp.min(jnp.where(rest == m2, lane, float(LANES)), axis=1, keepdims=True)
    e2 = jnp.exp(m2 - m1)
    w1 = 1.0 / (1.0 + e2)
    w2 = e2 / (1.0 + e2)
    idx_ref[...] = jnp.concatenate([i1, i2], axis=1).astype(jnp.int32)
    w_ref[...] = jnp.concatenate([w1, w2], axis=1)


def _oproj_router(h, o, w_o, g, w_router):
    t, d = h.shape
    kvw = o.shape[1]
    ne = w_router.shape[1]
    tm = min(t, 512)
    nsub = d // LANES
    assert t % tm == 0 and ne <= LANES
    wr = jnp.pad(w_router, ((0, 0), (0, LANES - ne)))
    wr_hi = wr.astype(BF16)
    wr_lo = (wr - wr_hi.astype(F32)).astype(BF16)
    row = lambda i: (i, 0)
    return pl.pallas_call(
        functools.partial(_oproj_router_kernel, n_experts=ne),
        grid=(t // tm,),
        in_specs=[
            pl.BlockSpec((tm, d), row), pl.BlockSpec((tm, kvw), row),
            pl.BlockSpec((kvw, d), lambda i: (0, 0)),
            pl.BlockSpec((1, d), lambda i: (0, 0)),
            pl.BlockSpec((d, 2 * LANES), lambda i: (0, 0)),
        ],
        out_specs=[pl.BlockSpec((tm, d), row), pl.BlockSpec((tm * nsub, LANES), row),
                   pl.BlockSpec((tm, TOP_K), row), pl.BlockSpec((tm, TOP_K), row)],
        out_shape=[jax.ShapeDtypeStruct((t, d), F32),
                   jax.ShapeDtypeStruct((t * nsub, LANES), F32),
                   jax.ShapeDtypeStruct((t, TOP_K), jnp.int32),
                   jax.ShapeDtypeStruct((t, TOP_K), F32)],
        compiler_params=_params(("parallel",)),
        name="oproj_router",
    )(h, o, w_o.astype(BF16), g.reshape(1, d), jnp.concatenate([wr_hi, wr_lo], axis=1))


MOE_ROWS = 512
DMA_UNROLL = 8


def _route_plan(idx, ne, tmg):
    t = idx.shape[0]
    npairs = t * TOP_K
    e_flat = idx.reshape(npairs)
    onehot = (e_flat[:, None] == jnp.arange(ne, dtype=jnp.int32)[None, :]).astype(jnp.int32)
    csum = jnp.cumsum(onehot, axis=0)
    cnt = csum[-1]
    padded = (cnt + tmg - 1) // tmg * tmg
    ends = jnp.cumsum(padded)
    off = ends - padded
    dest = jnp.sum(onehot * (off[None, :] + csum), axis=1) - 1
    total = ends[-1]
    k = jnp.arange(tmg, dtype=jnp.int32)[None, :]
    valid = (k < (padded - cnt)[:, None]).reshape(-1)
    pad_row = ((off + cnt)[:, None] + k).reshape(-1)
    tail_row = total + jnp.cumsum(jnp.logical_not(valid).astype(jnp.int32)) - 1
    fill = jnp.where(valid, pad_row, tail_row)
    n_tiles = total // tmg
    nt_max = (npairs + ne * tmg) // tmg
    tile = jnp.minimum(jnp.arange(nt_max, dtype=jnp.int32), n_tiles - 1)
    tile_expert = jnp.sum((tile[:, None] >= (ends // tmg)[None, :]).astype(jnp.int32), axis=1)
    return dest, fill, tile_expert, n_tiles.reshape(1)


def _dispatch_kernel(dest_ref, fill_ref, u_ref, xg_hbm, zero_ref, sem, *, ch, pair_steps, nsub):
    i = pl.program_id(0)

    def issue(src_row, dst_rows):
        def body(j, carry):
            for jj in range(DMA_UNROLL):
                pltpu.make_async_copy(src_row(j, jj), xg_hbm.at[dst_rows[0, j * DMA_UNROLL + jj]],
                                      sem.at[0]).start(priority=jj % 2)
            return carry
        lax.fori_loop(0, ch // DMA_UNROLL, body, 0)

    def token_tile(j, jj):
        tok = j * (DMA_UNROLL // TOP_K) + jj // TOP_K
        return u_ref.at[pl.ds(pl.multiple_of(tok * nsub, nsub), nsub), :]

    @pl.when(i < pair_steps)
    def _():
        issue(token_tile, dest_ref)

    @pl.when(i >= pair_steps)
    def _():
        zero_ref[...] = jnp.zeros(zero_ref.shape, zero_ref.dtype)
        issue(lambda j, jj: zero_ref, fill_ref)

    pltpu.make_async_copy(xg_hbm.at[pl.ds(0, ch)], xg_hbm.at[pl.ds(0, ch)], sem.at[0]).wait()


def _dispatch(u_tiles, dest, fill, n_rows, nsub):
    t = u_tiles.shape[0] // nsub
    npairs = dest.shape[0]
    assert npairs == t * TOP_K and DMA_UNROLL % TOP_K == 0
    ch = min(2 * MOE_ROWS, npairs)
    assert npairs % ch == 0 and fill.shape[0] % ch == 0 and ch % DMA_UNROLL == 0
    pair_steps = npairs // ch
    fill_steps = fill.shape[0] // ch
    smem = pltpu.SMEM
    return pl.pallas_call(
        functools.partial(_dispatch_kernel, ch=ch, pair_steps=pair_steps, nsub=nsub),
        grid=(pair_steps + fill_steps,),
        in_specs=[
            pl.BlockSpec((None, 1, ch), lambda i: (jnp.minimum(i, pair_steps - 1), 0, 0),
                         memory_space=smem),
            pl.BlockSpec((None, 1, ch), lambda i: (jnp.maximum(i - pair_steps, 0), 0, 0),
                         memory_space=smem),
            pl.BlockSpec((ch // TOP_K * nsub, LANES),
                         lambda i: (jnp.minimum(i, pair_steps - 1), 0)),
        ],
        out_specs=pl.BlockSpec(memory_space=pl.ANY),
        out_shape=jax.ShapeDtypeStruct((n_rows, nsub, LANES), F32),
        scratch_shapes=[pltpu.VMEM((nsub, LANES), F32), pltpu.SemaphoreType.DMA((1,))],
        compiler_params=_params(("arbitrary",)),
        name="moe_dispatch",
    )(dest.reshape(pair_steps, 1, ch), fill.reshape(fill_steps, 1, ch), u_tiles)


def _moe_ffn_kernel(te_ref, nt_ref, x_ref, wgu_ref, wd_ref, o_ref, *, f, tmg, nsub):
    del te_ref
    live = pl.program_id(0) < nt_ref[0]

    @pl.when(live)
    def _():
        x = _load_token_tiles(x_ref, tmg, nsub).astype(BF16)
        _store_token_tiles(o_ref, _swiglu(x, wgu_ref, wd_ref, f))

    @pl.when(jnp.logical_not(live))
    def _():
        o_ref[...] = jnp.zeros(o_ref.shape, o_ref.dtype)


def _moe_ffn(xg, tile_expert, n_tiles, w_gu, w_down, tmg):
    ne, f, d = w_down.shape
    nsub = d // LANES
    nt_max = tile_expert.shape[0]
    assert xg.shape[0] == nt_max * tmg * nsub
    rows = lambda i, te, nt: (jnp.minimum(i, nt[0] - 1), 0)
    return pl.pallas_call(
        functools.partial(_moe_ffn_kernel, f=f, tmg=tmg, nsub=nsub),
        grid_spec=pltpu.PrefetchScalarGridSpec(
            num_scalar_prefetch=2,
            grid=(nt_max,),
            in_specs=[
                pl.BlockSpec((tmg * nsub, LANES), rows),
                pl.BlockSpec((None, d, 2 * f), lambda i, te, nt: (te[i], 0, 0)),
                pl.BlockSpec((None, f, d), lambda i, te, nt: (te[i], 0, 0)),
            ],
            out_specs=pl.BlockSpec((tmg * nsub, LANES), lambda i, te, nt: (i, 0)),
        ),
        out_shape=jax.ShapeDtypeStruct(xg.shape, F32),
        compiler_params=_params(("arbitrary",), VMEM_LIMIT),
        name="moe_experts",
    )(tile_expert, n_tiles, xg, w_gu.astype(BF16), w_down.astype(BF16))


def _combine_kernel(dcur_ref, dnxt_ref, h_ref, w_ref, og_hbm, y_ref, buf, sem, *, tm, nsub):
    i = pl.program_id(0)
    n = pl.num_programs(0)
    rows = TOP_K * tm

    def issue(d_ref, slot):
        def body(j, carry):
            for jj in range(DMA_UNROLL):
                r = j * DMA_UNROLL + jj
                dst = buf.at[slot, pl.ds(pl.multiple_of(r * nsub, nsub), nsub), :]
                pltpu.make_async_copy(og_hbm.at[d_ref[0, r]], dst,
                                      sem.at[slot]).start(priority=jj % 2)
            return carry
        lax.fori_loop(0, rows // DMA_UNROLL, body, 0)

    @pl.when(i == 0)
    def _():
        issue(dcur_ref, 0)

    @pl.when(i + 1 < n)
    def _():
        issue(dnxt_ref, lax.rem(i + 1, 2))

    slot = lax.rem(i, 2)
    pltpu.make_async_copy(buf.at[slot], buf.at[slot], sem.at[slot]).wait()
    pair = TOP_K * nsub
    a = _load_token_tiles(buf.at[slot], tm, nsub, first=0, stride=pair)
    b = _load_token_tiles(buf.at[slot], tm, nsub, first=nsub, stride=pair)
    w = w_ref[...]
    y_ref[...] = h_ref[...] + (w[:, 0:1] * a + w[:, 1:2] * b)


def _combine(h, w, og, dest):
    t, d = h.shape
    nsub = d // LANES
    tm = min(t, 256)
    assert t % tm == 0 and (TOP_K * tm) % DMA_UNROLL == 0
    n = t // tm
    dest3 = dest.reshape(n, 1, TOP_K * tm)
    smem = pltpu.SMEM
    return pl.pallas_call(
        functools.partial(_combine_kernel, tm=tm, nsub=nsub),
        grid=(n,),
        in_specs=[
            pl.BlockSpec((None, 1, TOP_K * tm), lambda i: (i, 0, 0), memory_space=smem),
            pl.BlockSpec((None, 1, TOP_K * tm), lambda i: (jnp.minimum(i + 1, n - 1), 0, 0),
                         memory_space=smem),
            pl.BlockSpec((tm, d), lambda i: (i, 0)),
            pl.BlockSpec((tm, TOP_K), lambda i: (i, 0)),
            pl.BlockSpec(memory_space=pl.ANY),
        ],
        out_specs=pl.BlockSpec((tm, d), lambda i: (i, 0)),
        out_shape=jax.ShapeDtypeStruct((t, d), F32),
        scratch_shapes=[pltpu.VMEM((2, TOP_K * tm * nsub, LANES), F32),
                        pltpu.SemaphoreType.DMA((2,))],
        compiler_params=_params(("arbitrary",)),
        name="moe_combine",
    )(dest3, dest3, h, w, og)


def _moe(h, u_tiles, idx, w, w_gu, w_down):
    t, d = h.shape
    ne = w_down.shape[0]
    nsub = d // LANES
    npairs = t * TOP_K
    tmg = MOE_ROWS if npairs >= 32 * MOE_ROWS else min(MOE_ROWS // 2, npairs)
    dest, fill, tile_expert, n_tiles = _route_plan(idx, ne, tmg)
    n_rows = npairs + ne * tmg
    xg = _dispatch(u_tiles, dest, fill, n_rows, nsub)
    og = _moe_ffn(xg.reshape(n_rows * nsub, LANES), tile_expert, n_tiles, w_gu, w_down, tmg)
    return _combine(h, w, og.reshape(n_rows, nsub, LANES), dest)


def _trunk(x, pool_hist, kv_cache, p):
    b, n, d = x.shape
    n_heads = p["b_f"].shape[0]
    start_pos = 0 if kv_cache is None else kv_cache[0].shape[1]

    pool_args = (p["g_pool_norm"][0], p["w_pool"][0], p["pool_scale"][0], start_pos)
    ffn_args = (p["g_ffn_norm"][0], p["w_ffn_gu"][0], p["w_ffn_down"][0])
    if n >= 256:
        h, pool_state = _pool_layer(x, None if pool_hist is None else pool_hist[0], *pool_args,
                                    ffn=ffn_args)
        h = h.reshape(b * n, d)
    else:
        h, pool_state = _pool_layer(x, None if pool_hist is None else pool_hist[0], *pool_args)
        h = _dense_ffn(h.reshape(b * n, d), *ffn_args)

    qkv_args = (p["g_kv_norm"], p["g_attn_norm"][0], p["w_kvf"], p["b_f"], p["g_k"],
                p["w_q"][0], p["g_q"][0], n_heads)
    kvw = p["w_q"].shape[2]
    hd = kvw // n_heads
    if kv_cache is None:
        kt, vt, lft, ct, kb, vtb, qb = _qkv_proj_t(h, b, *qkv_args)
        k = jnp.transpose(kt.reshape(b, n_heads, hd, n), (0, 3, 1, 2))
        v = jnp.transpose(vt.reshape(b, n_heads, hd, n), (0, 3, 1, 2))
        logf = jnp.transpose(lft, (0, 2, 1))
        o = _attn_prompt(qb.reshape(b, n, kvw), kb.reshape(b, n, kvw), vtb,
                         jnp.transpose(ct, (0, 2, 1)), n_heads)
    else:
        k, v, logf, kb, vb, qb = _qkv_proj(h, *qkv_args)
        logf = logf.reshape(b, n, n_heads)
        qb, kb, vb = (a.reshape(b, n, kvw) for a in (qb, kb, vb))
        k = k.reshape(b, n, n_heads, hd)
        v = v.reshape(b, n, n_heads, hd)
        cache_k, cache_v, cache_logf = kv_cache
        past = cache_k.shape[1]
        lf_all = jnp.concatenate([cache_logf.astype(F32), logf], axis=1)
        total = past + n
        padded = -(-total // LANES) * LANES
        lf_t = jnp.transpose(lf_all, (0, 2, 1)).reshape(b * n_heads, total)
        c_t = _cumsum_lanes(jnp.pad(lf_t, ((0, 0), (0, padded - total))))
        c_t = c_t.reshape(b, n_heads, padded)
        c_new = c_t[:, :, past:total]
        cache_kt = jnp.transpose(cache_k, (0, 2, 3, 1)).reshape(b, kvw, past)
        cache_vt = jnp.transpose(cache_v, (0, 2, 3, 1)).reshape(b, kvw, past)
        heads = lambda a: jnp.transpose(a.reshape(b, n, n_heads, hd), (0, 2, 1, 3))
        o = _attn_sample(heads(qb), heads(kb), heads(vb), cache_kt, cache_vt, c_new[..., None],
                         c_t[:, :, None, :past], c_new[:, :, None, :])
        o = jnp.transpose(o, (0, 2, 1, 3))

    h2, u_tiles, idx, top_w = _oproj_router(h, o.reshape(b * n, kvw), p["w_o"][0],
                                            p["g_ffn_norm"][1], p["w_router"][0])
    y = _moe(h2, u_tiles, idx, top_w, p["w_moe_gu"][0], p["w_moe_down"][0])
    return y.reshape(b, n, d), pool_state[None], k, v, logf


def kernel(x_prompt, x_sample, state_pool, cache_k, cache_v, cache_logf, g_pool_norm, w_pool,
           pool_scale, g_kv_norm, w_kvf, b_f, g_k, g_attn_norm, w_q, g_q, w_o, g_ffn_norm,
           w_ffn_gu, w_ffn_down, w_router, w_moe_gu, w_moe_down):
    p = dict(g_pool_norm=g_pool_norm, w_pool=w_pool, pool_scale=pool_scale,
             g_kv_norm=g_kv_norm, w_kvf=w_kvf, b_f=b_f, g_k=g_k, g_attn_norm=g_attn_norm,
             w_q=w_q, g_q=g_q, w_o=w_o, g_ffn_norm=g_ffn_norm, w_ffn_gu=w_ffn_gu,
             w_ffn_down=w_ffn_down, w_router=w_router, w_moe_gu=w_moe_gu,
             w_moe_down=w_moe_down)
    assert w_pool.shape[0] == 1 and w_q.shape[0] == 1 and w_router.shape[0] == 1
    y_p, pool_p, k_p, v_p, lf_p = _trunk(x_prompt, None, None, p)
    y_s, pool_s, k_s, v_s, lf_s = _trunk(
        x_sample, state_pool, (cache_k, cache_v, cache_logf), p)
    return (y_p, y_s, pool_p, k_p, v_p, lf_p, pool_s, k_s, v_s, lf_s)
```

```python
import functools
import math

import jax
import jax.numpy as jnp
from jax import lax
from jax.experimental import pallas as pl
from jax.experimental.pallas import tpu as pltpu

EPS = 1e-6
POOL_WINDOWS = (2, 4, 8, 16)
HALO = max(POOL_WINDOWS)
POOL_HIST = HALO - 1
TOP_K = 2
LANES = 128
BF16_ROWS = 16
MXU_DIM = 256
VMEM_LIMIT = 56 * 1024 * 1024
NEG_INF = float("-inf")

BF16 = jnp.bfloat16
F32 = jnp.float32


def _params(semantics, vmem=None):
    return pltpu.CompilerParams(dimension_semantics=semantics, vmem_limit_bytes=vmem)


def _dot(a, b):
    return jnp.dot(a, b, preferred_element_type=F32)


def _dot_nt(a, b):
    return lax.dot_general(a, b, (((1,), (1,)), ((), ())), preferred_element_type=F32)


def _rms_base(x):
    return x * lax.rsqrt(jnp.mean(x * x, axis=-1, keepdims=True) + EPS)


def _split2(x):
    hi = x.astype(BF16)
    lo = (x - hi.astype(F32)).astype(BF16)
    return hi, lo


def _split3(x):
    hi = x.astype(BF16)
    r = x - hi.astype(F32)
    mid = r.astype(BF16)
    lo = (r - mid.astype(F32)).astype(BF16)
    return hi, mid, lo


def _ffn_chunks(f):
    out, s = [], 0
    while s < f:
        n = min(4 * MXU_DIM, f - s)
        out.append((s, n))
        s += n
    return out


def _pool_kernel(x_ref, halo_ref, g_ref, w_ref, scale_ref, *rest, tm, start_pos, halo_is_normed,
                 f):
    if f is None:
        h_ref, st_ref = rest
    else:
        gf_ref, wgu_ref, wd_ref, h_ref, st_ref = rest
    i = pl.program_id(1)
    x = x_ref[...]
    g = g_ref[...]
    u = _rms_base(x) * g
    if halo_is_normed:
        uh = halo_ref[...]
    else:
        uh = _rms_base(halo_ref[...]) * g
        uh = jnp.where(i > 0, uh, 0.0)
    ext = jnp.concatenate([uh, u], axis=0)

    row = lax.broadcasted_iota(jnp.int32, (tm, 1), 0)
    avail = start_pos + i * tm + row + 1
    group = x.shape[1] // len(POOL_WINDOWS)
    ys = []
    for gi, w in enumerate(POOL_WINDOWS):
        lo, hi = gi * group, (gi + 1) * group
        s = ext[:, lo:hi]
        step = 1
        while step < w:
            s = s + pltpu.roll(s, step, 0)
            step *= 2
        inv_cnt = 1.0 / jnp.minimum(avail, w).astype(F32)
        pooled = s[HALO:, :] * inv_cnt
        diff = pooled - u[:, lo:hi]
        ys.append(_dot(diff.astype(BF16), w_ref[gi]))
    h = x + jnp.concatenate(ys, axis=1) * scale_ref[...]
    if f is not None:
        h = h + _swiglu((_rms_base(h) * gf_ref[...]).astype(BF16), wgu_ref, wd_ref, f)
    h_ref[...] = h

    @pl.when(i == pl.num_programs(1) - 1)
    def _():
        st_ref[...] = u[tm - HALO:, :]


def _pool_layer(x, hist, g, w_pool, pool_scale, start_pos, ffn=None):
    b, n, d = x.shape
    tm = min(n, 256 if ffn is None else 512)
    assert n % tm == 0 and tm % HALO == 0 and n >= HALO
    nt = n // tm
    if hist is None:
        halo_arr = x
        halo_spec = pl.BlockSpec(
            (None, HALO, d), lambda bi, i: (bi, jnp.maximum(i * (tm // HALO) - 1, 0), 0))
    else:
        assert nt == 1
        halo_arr = jnp.pad(hist, ((0, 0), (HALO - POOL_HIST, 0), (0, 0)))
        halo_spec = pl.BlockSpec((None, HALO, d), lambda bi, i: (bi, 0, 0))
    ng = len(POOL_WINDOWS)
    vec = pl.BlockSpec((1, d), lambda bi, i: (0, 0))
    in_specs = [pl.BlockSpec((None, tm, d), lambda bi, i: (bi, i, 0)), halo_spec, vec,
                pl.BlockSpec((ng, d // ng, d // ng), lambda bi, i: (0, 0, 0)), vec]
    args = [x, halo_arr, g.reshape(1, d), w_pool.astype(BF16), pool_scale.reshape(1, d)]
    f = None
    if ffn is not None:
        g_ffn, w_gu, w_down = ffn
        f = w_down.shape[0]
        in_specs += [vec, pl.BlockSpec((d, 2 * f), lambda bi, i: (0, 0)),
                     pl.BlockSpec((f, d), lambda bi, i: (0, 0))]
        args += [g_ffn.reshape(1, d), w_gu.astype(BF16), w_down.astype(BF16)]
    h, st = pl.pallas_call(
        functools.partial(_pool_kernel, tm=tm, start_pos=start_pos,
                          halo_is_normed=hist is not None, f=f),
        grid=(b, nt),
        in_specs=in_specs,
        out_specs=[
            pl.BlockSpec((None, tm, d), lambda bi, i: (bi, i, 0)),
            pl.BlockSpec((None, HALO, d), lambda bi, i: (bi, 0, 0)),
        ],
        out_shape=[jax.ShapeDtypeStruct((b, n, d), F32),
                   jax.ShapeDtypeStruct((b, HALO, d), F32)],
        compiler_params=_params(("parallel", "arbitrary"), None if ffn is None else VMEM_LIMIT),
        name="pool_mixer" if ffn is None else "pool_ffn",
    )(*args)
    return h, st[:, HALO - POOL_HIST:, :]


def _swiglu(xb, wgu_ref, wd_ref, f):
    acc = None
    for s, n in _ffn_chunks(f):
        gate = _dot(xb, wgu_ref[:, s:s + n])
        up = _dot(xb, wgu_ref[:, f + s:f + s + n])
        act = (gate * jax.nn.sigmoid(gate)) * up
        part = _dot(act.astype(BF16), wd_ref[s:s + n, :])
        acc = part if acc is None else acc + part
    return acc


def _dense_ffn_kernel(h_ref, g_ref, wgu_ref, wd_ref, o_ref, *, f):
    h = h_ref[...]
    u = (_rms_base(h) * g_ref[...]).astype(BF16)
    o_ref[...] = h + _swiglu(u, wgu_ref, wd_ref, f)


def _dense_ffn(h, g, w_gu, w_down):
    t, d = h.shape
    f = w_down.shape[0]
    tm = min(t, 512)
    assert t % tm == 0
    return pl.pallas_call(
        functools.partial(_dense_ffn_kernel, f=f),
        grid=(t // tm,),
        in_specs=[
            pl.BlockSpec((tm, d), lambda i: (i, 0)),
            pl.BlockSpec((1, d), lambda i: (0, 0)),
            pl.BlockSpec((d, 2 * f), lambda i: (0, 0)),
            pl.BlockSpec((f, d), lambda i: (0, 0)),
        ],
        out_specs=pl.BlockSpec((tm, d), lambda i: (i, 0)),
        out_shape=jax.ShapeDtypeStruct((t, d), F32),
        compiler_params=_params(("parallel",), VMEM_LIMIT),
        name="dense_ffn",
    )(h, g.reshape(1, d), w_gu.astype(BF16), w_down.astype(BF16))


def _head_norm(x, bd_ref, g):
    sq_hi, sq_lo = _split2(x * x)
    n = bd_ref.shape[0]
    ms = jnp.concatenate(
        [_dot(sq_hi[:, c:c + n], bd_ref[...]) + _dot(sq_lo[:, c:c + n], bd_ref[...])
         for c in range(0, x.shape[1], n)], axis=1)
    return x * lax.rsqrt(ms + EPS) * g


def _log_sigmoid(z):
    return jnp.minimum(z, 0.0) - jnp.log1p(jnp.exp(-jnp.abs(z)))


def _qkv_kernel(h_ref, gkv_ref, gat_ref, wk_ref, wv_ref, wf_ref, wq_ref, bf_ref,
                gk_ref, gq_ref, bd_ref,
                k_ref, v_ref, lf_ref, kb_ref, vb_ref, qb_ref, *, n_heads, q_scale):
    base = _rms_base(h_ref[...])
    a_kv = (base * gkv_ref[...]).astype(BF16)
    a_q = (base * gat_ref[...]).astype(BF16)
    k = _head_norm(_dot(a_kv, wk_ref[...]), bd_ref, gk_ref[...])
    v = _dot(a_kv, wv_ref[...])
    z = _dot(a_kv, wf_ref[...])[:, :n_heads] + bf_ref[...]
    q = _head_norm(_dot(a_q, wq_ref[...]), bd_ref, gq_ref[...])
    k_ref[...] = k
    v_ref[...] = v
    lf_ref[...] = _log_sigmoid(z)
    kb_ref[...] = k.astype(BF16)
    vb_ref[...] = v.astype(BF16)
    qb_ref[...] = (q * q_scale).astype(BF16)


def _qkv_proj(h, g_kv, g_attn, w_kvf, b_f, g_k, w_q, g_q, n_heads):
    t, d = h.shape
    kvw = w_q.shape[1]
    hd = kvw // n_heads
    tm = min(t, 256)
    assert t % tm == 0 and MXU_DIM % hd == 0 and kvw % MXU_DIM == 0
    wk = w_kvf[:, :kvw].astype(BF16)
    wv = w_kvf[:, kvw:2 * kvw].astype(BF16)
    wf = jnp.pad(w_kvf[:, 2 * kvw:], ((0, 0), (0, LANES - n_heads))).astype(BF16)
    head = jnp.arange(MXU_DIM) // hd
    bd = ((head[:, None] == head[None, :]).astype(F32) / hd).astype(BF16)
    row = lambda i: (i, 0)
    fix = lambda i: (0, 0)
    return pl.pallas_call(
        functools.partial(_qkv_kernel, n_heads=n_heads, q_scale=hd ** -0.5),
        grid=(t // tm,),
        in_specs=[
            pl.BlockSpec((tm, d), row),
            pl.BlockSpec((1, d), fix), pl.BlockSpec((1, d), fix),
            pl.BlockSpec((d, kvw), fix), pl.BlockSpec((d, kvw), fix),
            pl.BlockSpec((d, LANES), fix), pl.BlockSpec((d, kvw), fix),
            pl.BlockSpec((1, n_heads), fix),
            pl.BlockSpec((1, kvw), fix), pl.BlockSpec((1, kvw), fix),
            pl.BlockSpec((MXU_DIM, MXU_DIM), fix),
        ],
        out_specs=[
            pl.BlockSpec((tm, kvw), row), pl.BlockSpec((tm, kvw), row),
            pl.BlockSpec((tm, n_heads), row),
            pl.BlockSpec((tm, kvw), row), pl.BlockSpec((tm, kvw), row),
            pl.BlockSpec((tm, kvw), row),
        ],
        out_shape=[
            jax.ShapeDtypeStruct((t, kvw), F32), jax.ShapeDtypeStruct((t, kvw), F32),
            jax.ShapeDtypeStruct((t, n_heads), F32),
            jax.ShapeDtypeStruct((t, kvw), BF16), jax.ShapeDtypeStruct((t, kvw), BF16),
            jax.ShapeDtypeStruct((t, kvw), BF16),
        ],
        compiler_params=_params(("parallel",), VMEM_LIMIT),
        name="qkv_proj",
    )(h, g_kv.reshape(1, d), g_attn.reshape(1, d), wk, wv, wf, w_q.astype(BF16),
      b_f.reshape(1, n_heads), jnp.tile(g_k, n_heads).reshape(1, kvw),
      jnp.tile(g_q, n_heads).reshape(1, kvw), bd)


def _cumsum_tile(x, tri, carry):
    y = carry
    for part in _split3(x):
        y = y + _dot(part, tri)
    return y


def _qkv_t_kernel(h_ref, gkv_ref, gat_ref, wkt_ref, wvt_ref, wft_ref, wq_ref, bf_ref,
                  gk_ref, gq_ref, bd_ref, tri_ref,
                  kt_ref, vt_ref, lft_ref, ct_ref, kb_ref, vtb_ref, qb_ref, carry_ref, *,
                  n_heads, q_scale):
    @pl.when(pl.program_id(1) == 0)
    def _():
        carry_ref[...] = jnp.zeros(carry_ref.shape, F32)

    base = _rms_base(h_ref[...])
    a_kv = (base * gkv_ref[...]).astype(BF16)
    a_q = (base * gat_ref[...]).astype(BF16)
    tm = a_kv.shape[0]
    kraw = _dot_nt(wkt_ref[...], a_kv)
    k3 = kraw.reshape(n_heads, kraw.shape[0] // n_heads, tm)
    ms = jnp.mean(k3 * k3, axis=1, keepdims=True)
    kt = (k3 * lax.rsqrt(ms + EPS) * gk_ref[...][None]).reshape(kraw.shape)
    vt = _dot_nt(wvt_ref[...], a_kv)
    lft = _log_sigmoid(_dot_nt(wft_ref[...], a_kv) + bf_ref[...])
    ct = _cumsum_tile(lft, tri_ref[...], carry_ref[...])
    carry_ref[...] = ct[:, tm - 1:tm]
    q = _head_norm(_dot(a_q, wq_ref[...]), bd_ref, gq_ref[...])
    kt_ref[...] = kt
    vt_ref[...] = vt
    lft_ref[...] = lft
    ct_ref[...] = ct
    kb_ref[...] = kt.T.astype(BF16)
    vtb_ref[...] = vt.astype(BF16)
    qb_ref[...] = (q * q_scale).astype(BF16)


def _qkv_proj_t(h, b, g_kv, g_attn, w_kvf, b_f, g_k, w_q, g_q, n_heads):
    t, d = h.shape
    s = t // b
    kvw = w_q.shape[1]
    hd = kvw // n_heads
    tm = min(s, 512)
    nt = s // tm
    assert s % tm == 0 and MXU_DIM % hd == 0 and kvw % MXU_DIM == 0
    wkt =w_kvf[:, :kvw].T.astype(BF16)
    wvt = w_kvf[:, kvw:2 * kvw].T.astype(BF16)
    wft = w_kvf[:, 2 * kvw:].T.astype(BF16)
    head = jnp.arange(MXU_DIM) // hd
    bd = ((head[:, None] == head[None, :]).astype(F32) / hd).astype(BF16)
    pos = jnp.arange(tm)
    tri = (pos[:, None] <= pos[None, :]).astype(BF16)
    fix = lambda bi, i: (0, 0)
    feat = lambda rows: pl.BlockSpec((None, rows, tm), lambda bi, i: (bi, 0, i))
    f32 = lambda rows: jax.ShapeDtypeStruct((b, rows, s), F32)
    tok = pl.BlockSpec((tm, kvw), lambda bi, i: (bi * nt + i, 0))
    return pl.pallas_call(
        functools.partial(_qkv_t_kernel, n_heads=n_heads,
                          q_scale=hd ** -0.5 * math.log2(math.e)),
        grid=(b, nt),
        in_specs=[
            pl.BlockSpec((tm, d), lambda bi, i: (bi * nt + i, 0)),
            pl.BlockSpec((1, d), fix), pl.BlockSpec((1, d), fix),
            pl.BlockSpec((kvw, d), fix), pl.BlockSpec((kvw, d), fix),
            pl.BlockSpec((n_heads, d), fix), pl.BlockSpec((d, kvw), fix),
            pl.BlockSpec((n_heads, tm), fix), pl.BlockSpec((hd, tm), fix),
            pl.BlockSpec((1, kvw), fix), pl.BlockSpec((MXU_DIM, MXU_DIM), fix),
            pl.BlockSpec((tm, tm), fix),
        ],
        out_specs=[feat(kvw), feat(kvw), feat(n_heads), feat(n_heads), tok, feat(kvw), tok],
        out_shape=[f32(kvw), f32(kvw), f32(n_heads), f32(n_heads),
                   jax.ShapeDtypeStruct((t, kvw), BF16),
                   jax.ShapeDtypeStruct((b, kvw, s), BF16),
                   jax.ShapeDtypeStruct((t, kvw), BF16)],
        scratch_shapes=[pltpu.VMEM((n_heads, 1), F32)],
        compiler_params=_params(("parallel", "arbitrary"), VMEM_LIMIT),
        name="qkv_proj_t",
    )(h, g_kv.reshape(1, d), g_attn.reshape(1, d), wkt, wvt, wft, w_q.astype(BF16),
      jnp.broadcast_to(b_f[:, None], (n_heads, tm)), jnp.broadcast_to(g_k[:, None], (hd, tm)),
      jnp.tile(g_q, n_heads).reshape(1, kvw), bd, tri)


def _cumsum_kernel(x_ref, o_ref):
    r, l = x_ref.shape
    ii = lax.broadcasted_iota(jnp.int32, (LANES, LANES), 0)
    jj = lax.broadcasted_iota(jnp.int32, (LANES, LANES), 1)
    tri = (ii <= jj).astype(BF16)
    carry = jnp.zeros((r, 1), F32)
    for c in range(0, l, LANES):
        y = _cumsum_tile(x_ref[:, c:c + LANES], tri, carry)
        o_ref[:, c:c + LANES] = y
        carry = y[:, LANES - 1:LANES]


def _cumsum_lanes(x):
    r, l = x.shape
    tr = min(r, 256)
    assert r % tr == 0 and l % LANES == 0
    return pl.pallas_call(
        _cumsum_kernel,
        grid=(r // tr,),
        in_specs=[pl.BlockSpec((tr, l), lambda i: (i, 0))],
        out_specs=pl.BlockSpec((tr, l), lambda i: (i, 0)),
        out_shape=jax.ShapeDtypeStruct((r, l), F32),
        compiler_params=_params(("parallel",)),
        name="logf_cumsum",
    )(x)


def _bdot(a, b, ca, cb):
    return lax.dot_general(a, b, (((ca,), (cb,)), ((0,), (0,))), preferred_element_type=F32)


def _attn_prompt_kernel(q_ref, cq_ref, k_ref, ck_ref, vt_ref, o_ref,
                        kaug_ref, m_ref, acc_ref, *, tq, n_heads):
    qi = pl.program_id(1)
    log2e = math.log2(math.e)
    kvw = q_ref.shape[1]
    hd = kvw // n_heads
    lane = lax.broadcasted_iota(jnp.int32, (1, LANES), 1)

    def augment(x_ref, c_all, h, c_first):
        blk = h * hd // LANES * LANES
        lo = h * hd - blk
        terms = tuple(p.astype(F32) for p in _split3(c_all[:, h:h + 1] * log2e))
        extras = terms + (1.0, 1.0, 1.0) if c_first else (1.0, 1.0, 1.0) + tuple(-t for t in terms)
        ext = jnp.zeros((c_all.shape[0], LANES), F32)
        for i, v in enumerate(extras):
            ext = jnp.where(lane == (lo + hd + i) % LANES, v, ext)
        own = jnp.logical_and(lane >= lo, lane < lo + hd)
        return jnp.where(own, x_ref[:, blk:blk + LANES], ext.astype(BF16))

    @pl.when(qi == 0)
    def _():
        ck_all = ck_ref[...]
        for h in range(n_heads):
            kaug_ref[h] = augment(k_ref, ck_all, h, False)

    cq_all = cq_ref[...]
    q_aug = jnp.stack([augment(q_ref, cq_all, h, True) for h in range(n_heads)])
    m_ref[...] = jnp.full(m_ref.shape, NEG_INF, F32)
    acc_ref[...] = jnp.zeros(acc_ref.shape, F32)
    n_den = acc_ref.shape[1] - hd
    den_rows = (lax.broadcasted_iota(jnp.int32, (n_heads, n_den, tq), 1) == 0).astype(BF16)

    def tile(kt, masked):
        start = pl.multiple_of(kt * tq, tq)
        st = _bdot(kaug_ref[:, pl.ds(start, tq), :], q_aug, 2, 2)
        if masked:
            key = lax.broadcasted_iota(jnp.int32, (1, tq, tq), 1)
            qry = lax.broadcasted_iota(jnp.int32, (1, tq, tq), 2)
            st = jnp.where(key <= qry, st, NEG_INF)
        m_prev = m_ref[...]
        m_new = jnp.maximum(m_prev, jnp.max(st, axis=1, keepdims=True))
        alpha = jnp.exp2(m_prev - m_new)
        p = jnp.exp2(st - m_new)
        vt = vt_ref[:, pl.ds(start, tq)].reshape(n_heads, hd, tq)
        vt = jnp.concatenate([vt, den_rows], axis=1)
        acc_ref[...] = alpha * acc_ref[...] + _bdot(vt, p.astype(BF16), 2, 1)
        m_ref[...] = m_new

    def body(kt, carry):
        tile(kt, False)
        return carry

    lax.fori_loop(0, qi, body, 0)
    tile(qi, True)
    ot = (acc_ref[:, :hd, :] / acc_ref[:, hd:hd + 1, :]).reshape(kvw, tq)
    o_ref[...] = ot.T.astype(o_ref.dtype)


def _attn_prompt(q, k, vt, c_col, n_heads):
    b, s, kvw = q.shape
    hd = kvw // n_heads
    assert LANES % hd == 0 and hd + 6 <= LANES
    tq = min(s, 256)
    assert s % tq == 0
    return pl.pallas_call(
        functools.partial(_attn_prompt_kernel, tq=tq, n_heads=n_heads),
        grid=(b, s // tq),
        in_specs=[
            pl.BlockSpec((None, tq, kvw), lambda bi, qi: (bi, qi, 0)),
            pl.BlockSpec((None, tq, n_heads), lambda bi, qi: (bi, qi, 0)),
            pl.BlockSpec((None, s, kvw), lambda bi, qi: (bi, 0, 0)),
            pl.BlockSpec((None, s, n_heads), lambda bi, qi: (bi, 0, 0)),
            pl.BlockSpec((None, kvw, s), lambda bi, qi: (bi, 0, 0)),
        ],
        out_specs=pl.BlockSpec((None, tq, kvw), lambda bi, qi: (bi, qi, 0)),
        out_shape=jax.ShapeDtypeStruct((b, s, kvw), BF16),
        scratch_shapes=[pltpu.VMEM((n_heads, s, LANES), BF16),
                        pltpu.VMEM((n_heads, 1, tq), F32),
                        pltpu.VMEM((n_heads, hd + BF16_ROWS, tq), F32)],
        compiler_params=_params(("parallel", "arbitrary"), VMEM_LIMIT),
        name="fox_prompt",
    )(q, c_col, k, c_col, vt)


def _attn_sample_kernel(q_ref, cq_ref, kc_ref, vc_ref, ckc_ref, kn_ref, vn_ref, ckn_ref,
                        o_ref, m_ref, acc_ref, *, n_heads):
    kt = pl.program_id(1)
    last = pl.num_programs(1) - 1
    q = q_ref[...]
    cq = cq_ref[...]
    n, hd = q.shape[1:]
    n_den = acc_ref.shape[2] - hd

    @pl.when(kt == 0)
    def _():
        m_ref[...] = jnp.full(m_ref.shape, NEG_INF, F32)
        acc_ref[...] = jnp.zeros(acc_ref.shape, F32)

    def update(s, v, v_contract):
        m_prev = m_ref[...]
        m_new = jnp.maximum(m_prev, jnp.max(s, axis=2, keepdims=True))
        p = jnp.exp(s - m_new).astype(BF16)
        acc_ref[...] = jnp.exp(m_prev - m_new) * acc_ref[...] + _bdot(p, v, 2, v_contract)
        m_ref[...] = m_new

    kvw, tk = kc_ref.shape
    split = (n_heads, kvw // n_heads, tk)
    kk = kc_ref[...].reshape(split).astype(BF16)
    den = (lax.broadcasted_iota(jnp.int32, (n_heads, n_den, tk), 1) == 0).astype(BF16)
    vv = jnp.concatenate([vc_ref[...].reshape(split).astype(BF16), den], axis=1)
    update(_bdot(q, kk, 2, 1) + (cq - ckc_ref[...]), vv, 2)

    @pl.when(kt == last)
    def _():
        r = lax.broadcasted_iota(jnp.int32, (1, n, n), 1)
        c = lax.broadcasted_iota(jnp.int32, (1, n, n), 2)
        s = _bdot(q, kn_ref[...], 2, 2) + (cq - ckn_ref[...])
        den_new = (lax.broadcasted_iota(jnp.int32, (n_heads, n, n_den), 2) == 0).astype(BF16)
        update(jnp.where(c <= r, s, NEG_INF), jnp.concatenate([vn_ref[...], den_new], axis=2), 1)
        acc = acc_ref[...]
        o_ref[...] = (acc[:, :, :hd] / acc[:, :, hd:hd + 1]).astype(o_ref.dtype)


def _attn_sample(q, k_new, v_new, cache_kt, cache_vt, cq, ck_cache, ck_new):
    b, n_heads, n, hd = q.shape
    kvw, p = cache_kt.shape[1:]
    tk = min(p, 1024)
    assert p % tk == 0 and kvw == n_heads * hd
    fix = lambda bi, kt: (bi, 0, 0, 0)
    rows = pl.BlockSpec((None, n_heads, n, hd), fix)
    cache = pl.BlockSpec((None, kvw, tk), lambda bi, kt: (bi, 0, kt))
    return pl.pallas_call(
        functools.partial(_attn_sample_kernel, n_heads=n_heads),
        grid=(b, p // tk),
        in_specs=[
            rows,
            pl.BlockSpec((None, n_heads, n, 1), fix),
            cache, cache,
            pl.BlockSpec((None, n_heads, 1, tk), lambda bi, kt: (bi, 0, 0, kt)),
            rows, rows,
            pl.BlockSpec((None, n_heads, 1, n), fix),
        ],
        out_specs=rows,
        out_shape=jax.ShapeDtypeStruct((b, n_heads, n, hd), BF16),
        scratch_shapes=[pltpu.VMEM((n_heads, n, 1), F32),
                        pltpu.VMEM((n_heads, n, hd + BF16_ROWS), F32)],
        compiler_params=_params(("parallel", "arbitrary"), VMEM_LIMIT),
        name="fox_sample",
    )(q, cq, cache_kt, cache_vt, ck_cache, k_new, v_new, ck_new)


def _store_token_tiles(ref, x):
    m, d = x.shape
    nsub = d // LANES
    for s in range(nsub):
        ref[pl.ds(s, m, stride=nsub), :] = x[:, s * LANES:(s + 1) * LANES]


def _load_token_tiles(ref, m, nsub, first=0, stride=None):
    stride = nsub if stride is None else stride
    return jnp.concatenate(
        [ref[pl.ds(first + s, m, stride=stride), :] for s in range(nsub)], axis=1)


def _oproj_router_kernel(h_ref, o_ref, wo_ref, g_ref, wr_ref, h2_ref, u_ref, idx_ref, w_ref, *,
                         n_experts):
    h2 = h_ref[...] + _dot(o_ref[...], wo_ref[...])
    u = _rms_base(h2) * g_ref[...]
    h2_ref[...] = h2
    _store_token_tiles(u_ref, u)
    u_hi, u_lo = _split2(u)
    both = _dot(u_hi, wr_ref[...])
    logits = both[:, :LANES] + both[:, LANES:] + _dot(u_lo, wr_ref[:, :LANES])
    lane = lax.broadcasted_iota(jnp.int32, logits.shape, 1).astype(F32)
    logits = jnp.where(lane < n_experts, logits, NEG_INF)
    m1 = jnp.max(logits, axis=1, keepdims=True)
    i1 = jnp.min(jnp.where(logits == m1, lane, float(LANES)), axis=1, keepdims=True)
    rest = jnp.where(lane == i1, NEG_INF, logits)
    m2 = jnp.max(rest, axis=1, keepdims=True)
    i2 = jnp.min(jnp.where(rest == m2, lane, float(LANES)), axis=1, keepdims=True)
    e2 = jnp.exp(m2 - m1)
    w1 = 1.0 / (1.0 + e2)
    w2 = e2 / (1.0 + e2)
    idx_ref[...] = jnp.concatenate([i1, i2], axis=1).astype(jnp.int32)
    w_ref[...] = jnp.concatenate([w1, w2], axis=1)


def _oproj_router(h, o, w_o, g, w_router):
    t, d = h.shape
    kvw = o.shape[1]
    ne = w_router.shape[1]
    tm = min(t, 512)
    nsub = d // LANES
    assert t % tm == 0 and ne <= LANES
    wr = jnp.pad(w_router, ((0, 0), (0, LANES - ne)))
    wr_hi = wr.astype(BF16)
    wr_lo = (wr - wr_hi.astype(F32)).astype(BF16)
    row = lambda i: (i, 0)
    return pl.pallas_call(
        functools.partial(_oproj_router_kernel, n_experts=ne),
        grid=(t // tm,),
        in_specs=[
            pl.BlockSpec((tm, d), row), pl.BlockSpec((tm, kvw), row),
            pl.BlockSpec((kvw, d), lambda i: (0, 0)),
            pl.BlockSpec((1, d), lambda i: (0, 0)),
            pl.BlockSpec((d, 2 * LANES), lambda i: (0, 0)),
        ],
        out_specs=[pl.BlockSpec((tm, d), row), pl.BlockSpec((tm * nsub, LANES), row),
                   pl.BlockSpec((tm, TOP_K), row), pl.BlockSpec((tm, TOP_K), row)],
        out_shape=[jax.ShapeDtypeStruct((t, d), F32),
                   jax.ShapeDtypeStruct((t * nsub, LANES), F32),
                   jax.ShapeDtypeStruct((t, TOP_K), jnp.int32),
                   jax.ShapeDtypeStruct((t, TOP_K), F32)],
        compiler_params=_params(("parallel",)),
        name="oproj_router",
    )(h, o, w_o.astype(BF16), g.reshape(1, d), jnp.concatenate([wr_hi, wr_lo], axis=1))


MOE_ROWS = 512
DMA_UNROLL = 8


def _route_plan(idx, ne, tmg):
    t = idx.shape[0]
    npairs = t * TOP_K
    e_flat = idx.reshape(npairs)
    onehot = (e_flat[:, None] == jnp.arange(ne, dtype=jnp.int32)[None, :]).astype(jnp.int32)
    csum = jnp.cumsum(onehot, axis=0)
    cnt = csum[-1]
    padded = (cnt + tmg - 1) // tmg * tmg
    ends = jnp.cumsum(padded)
    off = ends - padded
    dest = jnp.sum(onehot * (off[None, :] + csum), axis=1) - 1
    total = ends[-1]
    k = jnp.arange(tmg, dtype=jnp.int32)[None, :]
    valid = (k < (padded - cnt)[:, None]).reshape(-1)
    pad_row = ((off + cnt)[:, None] + k).reshape(-1)
    tail_row = total + jnp.cumsum(jnp.logical_not(valid).astype(jnp.int32)) - 1
    fill = jnp.where(valid, pad_row, tail_row)
    n_tiles = total // tmg
    nt_max = (npairs + ne * tmg) // tmg
    tile = jnp.minimum(jnp.arange(nt_max, dtype=jnp.int32), n_tiles - 1)
    tile_expert = jnp.sum((tile[:, None] >= (ends // tmg)[None, :]).astype(jnp.int32), axis=1)
    return dest, fill, tile_expert, n_tiles.reshape(1)


def _step_ranges(steps):
    out, first = [], 0
    for n in steps:
        out.append((first, n))
        first += n
    return out


def _in_range(i, first, count):
    return jnp.logical_and(i >= first, i < first + count)


def _dispatch_kernel(dest_ref, fill_ref, *rest, ch, ranges, nsub):
    u_refs, (xg_hbm, zero_ref, sem) = rest[:len(ranges)], rest[len(ranges):]
    i = pl.program_id(0)

    def issue(src_row, dst_rows):
        def body(j, carry):
            for jj in range(DMA_UNROLL):
                pltpu.make_async_copy(src_row(j, jj), xg_hbm.at[dst_rows[0, j * DMA_UNROLL + jj]],
                                      sem.at[0]).start(priority=jj % 2)
            return carry
        lax.fori_loop(0, ch // DMA_UNROLL, body, 0)

    for u_ref, (first, count) in zip(u_refs, ranges):
        def token_tile(j, jj, u_ref=u_ref):
            tok = j * (DMA_UNROLL // TOP_K) + jj // TOP_K
            return u_ref.at[pl.ds(pl.multiple_of(tok * nsub, nsub), nsub), :]

        @pl.when(_in_range(i, first, count))
        def _(token_tile=token_tile):
            issue(token_tile, dest_ref)

    pair_steps = ranges[-1][0] + ranges[-1][1]

    @pl.when(i >= pair_steps)
    def _():
        zero_ref[...] = jnp.zeros(zero_ref.shape, zero_ref.dtype)
        issue(lambda j, jj: zero_ref, fill_ref)

    pltpu.make_async_copy(xg_hbm.at[pl.ds(0, ch)], xg_hbm.at[pl.ds(0, ch)], sem.at[0]).wait()


def _dispatch(u_groups, dest, fill, n_rows, nsub):
    pairs = [u.shape[0] // nsub * TOP_K for u in u_groups]
    assert sum(pairs) == dest.shape[0] and DMA_UNROLL % TOP_K == 0
    ch = math.gcd(2 * MOE_ROWS, *pairs)
    assert fill.shape[0] % ch == 0 and ch % DMA_UNROLL == 0
    ranges = _step_ranges([p // ch for p in pairs])
    pair_steps = sum(p // ch for p in pairs)
    fill_steps = fill.shape[0] // ch
    smem = pltpu.SMEM
    clamp = lambda i, first, count: jnp.clip(i - first, 0, count - 1)
    return pl.pallas_call(
        functools.partial(_dispatch_kernel, ch=ch, ranges=ranges, nsub=nsub),
        grid=(pair_steps + fill_steps,),
        in_specs=[
            pl.BlockSpec((None, 1, ch), lambda i: (jnp.minimum(i, pair_steps - 1), 0, 0),
                         memory_space=smem),
            pl.BlockSpec((None, 1, ch), lambda i: (jnp.maximum(i - pair_steps, 0), 0, 0),
                         memory_space=smem),
        ] + [pl.BlockSpec((ch // TOP_K * nsub, LANES),
                          lambda i, first=first, count=count: (clamp(i, first, count), 0))
             for first, count in ranges],
        out_specs=pl.BlockSpec(memory_space=pl.ANY),
        out_shape=jax.ShapeDtypeStruct((n_rows, nsub, LANES), F32),
        scratch_shapes=[pltpu.VMEM((nsub, LANES), F32), pltpu.SemaphoreType.DMA((1,))],
        compiler_params=_params(("arbitrary",)),
        name="moe_dispatch",
    )(dest.reshape(pair_steps, 1, ch), fill.reshape(fill_steps, 1, ch), *u_groups)


def _moe_ffn_kernel(te_ref, nt_ref, x_ref, wgu_ref, wd_ref, o_ref, *, f, tmg, nsub):
    del te_ref
    live = pl.program_id(0) < nt_ref[0]

    @pl.when(live)
    def _():
        x = _load_token_tiles(x_ref, tmg, nsub).astype(BF16)
        _store_token_tiles(o_ref, _swiglu(x, wgu_ref, wd_ref, f))

    @pl.when(jnp.logical_not(live))
    def _():
        o_ref[...] = jnp.zeros(o_ref.shape, o_ref.dtype)


def _moe_ffn(xg, tile_expert, n_tiles, w_gu, w_down, tmg):
    ne, f, d = w_down.shape
    nsub = d // LANES
    nt_max = tile_expert.shape[0]
    assert xg.shape[0] == nt_max * tmg * nsub
    rows = lambda i, te, nt: (jnp.minimum(i, nt[0] - 1), 0)
    return pl.pallas_call(
        functools.partial(_moe_ffn_kernel, f=f, tmg=tmg, nsub=nsub),
        grid_spec=pltpu.PrefetchScalarGridSpec(
            num_scalar_prefetch=2,
            grid=(nt_max,),
            in_specs=[
                pl.BlockSpec((tmg * nsub, LANES), rows),
                pl.BlockSpec((None, d, 2 * f), lambda i, te, nt: (te[i], 0, 0)),
                pl.BlockSpec((None, f, d), lambda i, te, nt: (te[i], 0, 0)),
            ],
            out_specs=pl.BlockSpec((tmg * nsub, LANES), lambda i, te, nt: (i, 0)),
        ),
        out_shape=jax.ShapeDtypeStruct(xg.shape, F32),
        compiler_params=_params(("arbitrary",), VMEM_LIMIT),
        name="moe_experts",
    )(tile_expert, n_tiles, xg, w_gu.astype(BF16), w_down.astype(BF16))


def _combine_kernel(dcur_ref, dnxt_ref, w_ref, og_hbm, *rest, tm, nsub, ranges):
    ng = len(ranges)
    h_refs, y_refs, (buf, sem) = rest[:ng], rest[ng:2 * ng], rest[2 * ng:]
    i = pl.program_id(0)
    n = pl.num_programs(0)
    rows = TOP_K * tm

    def issue(d_ref, slot):
        def body(j, carry):
            for jj in range(DMA_UNROLL):
                tok = j * (DMA_UNROLL // TOP_K) + jj // TOP_K
                dst = buf.at[slot, jj % TOP_K, pl.ds(pl.multiple_of(tok * nsub, nsub), nsub), :]
                pltpu.make_async_copy(og_hbm.at[d_ref[0, j * DMA_UNROLL + jj]], dst,
                                      sem.at[slot]).start(priority=jj % 2)
            return carry
        lax.fori_loop(0, rows // DMA_UNROLL, body, 0)

    @pl.when(i == 0)
    def _():
        issue(dcur_ref, 0)

    @pl.when(i + 1 < n)
    def _():
        issue(dnxt_ref, lax.rem(i + 1, 2))

    slot = lax.rem(i, 2)
    pltpu.make_async_copy(buf.at[slot], buf.at[slot], sem.at[slot]).wait()
    w = w_ref[...]
    moe = (w[:, 0:1] * _load_token_tiles(buf.at[slot, 0], tm, nsub)
           + w[:, 1:2] * _load_token_tiles(buf.at[slot, 1], tm, nsub))
    for h_ref, y_ref, (first, count) in zip(h_refs, y_refs, ranges):
        @pl.when(_in_range(i, first, count))
        def _(h_ref=h_ref, y_ref=y_ref):
            y_ref[...] = h_ref[...] + moe


def _combine(h_groups, w, og, dest):
    assert TOP_K == 2
    d = h_groups[0].shape[1]
    nsub = d // LANES
    tm = math.gcd(256, *[h.shape[0] for h in h_groups])
    assert (TOP_K * tm) % DMA_UNROLL == 0
    ranges = _step_ranges([h.shape[0] // tm for h in h_groups])
    n = sum(count for _, count in ranges)
    dest3 = dest.reshape(n, 1, TOP_K * tm)
    smem = pltpu.SMEM
    clamp = lambda i, first, count: jnp.clip(i - first, 0, count - 1)
    rows = [pl.BlockSpec((tm, d), lambda i, first=first, count=count: (clamp(i, first, count), 0))
            for first, count in ranges]
    return pl.pallas_call(
        functools.partial(_combine_kernel, tm=tm, nsub=nsub, ranges=ranges),
        grid=(n,),
        in_specs=[
            pl.BlockSpec((None, 1, TOP_K * tm), lambda i: (i, 0, 0), memory_space=smem),
            pl.BlockSpec((None, 1, TOP_K * tm), lambda i: (jnp.minimum(i + 1, n - 1), 0, 0),
                         memory_space=smem),
            pl.BlockSpec((tm, TOP_K), lambda i: (i, 0)),
            pl.BlockSpec(memory_space=pl.ANY),
        ] + rows,
        out_specs=rows,
        out_shape=[jax.ShapeDtypeStruct(h.shape, F32) for h in h_groups],
        scratch_shapes=[pltpu.VMEM((2, TOP_K, tm * nsub, LANES), F32),
                        pltpu.SemaphoreType.DMA((2,))],
        compiler_params=_params(("arbitrary",)),
        name="moe_combine",
    )(dest3, dest3, w, og, *h_groups)


def _moe(groups, w_gu, w_down):
    ne, _, d = w_down.shape
    nsub = d // LANES
    idx = jnp.concatenate([g[2] for g in groups], axis=0)
    w = jnp.concatenate([g[3] for g in groups], axis=0)
    npairs = idx.shape[0] * TOP_K
    tmg = math.gcd(MOE_ROWS, npairs)
    dest, fill, tile_expert, n_tiles = _route_plan(idx, ne, tmg)
    n_rows = npairs + ne * tmg
    xg = _dispatch([g[1] for g in groups], dest, fill, n_rows, nsub)
    og = _moe_ffn(xg.reshape(n_rows * nsub, LANES), tile_expert, n_tiles, w_gu, w_down, tmg)
    return _combine([g[0] for g in groups], w, og.reshape(n_rows, nsub, LANES), dest)


def _trunk(x, pool_hist, kv_cache, p):
    b, n, d = x.shape
    n_heads = p["b_f"].shape[0]
    start_pos = 0 if kv_cache is None else kv_cache[0].shape[1]

    pool_args = (p["g_pool_norm"][0], p["w_pool"][0], p["pool_scale"][0], start_pos)
    ffn_args = (p["g_ffn_norm"][0], p["w_ffn_gu"][0], p["w_ffn_down"][0])
    if n >= 256:
        h, pool_state = _pool_layer(x, None if pool_hist is None else pool_hist[0], *pool_args,
                                    ffn=ffn_args)
        h = h.reshape(b * n, d)
    else:
        h, pool_state = _pool_layer(x, None if pool_hist is None else pool_hist[0], *pool_args)
        h = _dense_ffn(h.reshape(b * n, d), *ffn_args)

    qkv_args = (p["g_kv_norm"], p["g_attn_norm"][0], p["w_kvf"], p["b_f"], p["g_k"],
                p["w_q"][0], p["g_q"][0], n_heads)
    kvw = p["w_q"].shape[2]
    hd = kvw // n_heads
    if kv_cache is None:
        kt, vt, lft, ct, kb, vtb, qb = _qkv_proj_t(h, b, *qkv_args)
        k = jnp.transpose(kt.reshape(b, n_heads, hd, n), (0, 3, 1, 2))
        v = jnp.transpose(vt.reshape(b, n_heads, hd, n), (0, 3, 1, 2))
        logf = jnp.transpose(lft, (0, 2, 1))
        o = _attn_prompt(qb.reshape(b, n, kvw), kb.reshape(b, n, kvw), vtb,
                         jnp.transpose(ct, (0, 2, 1)), n_heads)
    else:
        k, v, logf, kb, vb, qb = _qkv_proj(h, *qkv_args)
        logf = logf.reshape(b, n, n_heads)
        qb, kb, vb = (a.reshape(b, n, kvw) for a in (qb, kb, vb))
        k = k.reshape(b, n, n_heads, hd)
        v = v.reshape(b, n, n_heads, hd)
        cache_k, cache_v, cache_logf = kv_cache
        past = cache_k.shape[1]
        lf_all = jnp.concatenate([cache_logf.astype(F32), logf], axis=1)
        total = past + n
        padded = -(-total // LANES) * LANES
        lf_t = jnp.transpose(lf_all, (0, 2, 1)).reshape(b * n_heads, total)
        c_t = _cumsum_lanes(jnp.pad(lf_t, ((0, 0), (0, padded - total))))
        c_t = c_t.reshape(b, n_heads, padded)
        c_new = c_t[:, :, past:total]
        cache_kt = jnp.transpose(cache_k, (0, 2, 3, 1)).reshape(b, kvw, past)
        cache_vt = jnp.transpose(cache_v, (0, 2, 3, 1)).reshape(b, kvw, past)
        heads = lambda a: jnp.transpose(a.reshape(b, n, n_heads, hd), (0, 2, 1, 3))
        o = _attn_sample(heads(qb), heads(kb), heads(vb), cache_kt, cache_vt, c_new[..., None],
                         c_t[:, :, None, :past], c_new[:, :, None, :])
        o = jnp.transpose(o, (0, 2, 1, 3))

    h2, u_tiles, idx, top_w = _oproj_router(h, o.reshape(b * n, kvw), p["w_o"][0],
                                            p["g_ffn_norm"][1], p["w_router"][0])
    return (h2, u_tiles, idx, top_w), pool_state[None], k, v, logf


def kernel(x_prompt, x_sample, state_pool, cache_k, cache_v, cache_logf, g_pool_norm, w_pool,
           pool_scale, g_kv_norm, w_kvf, b_f, g_k, g_attn_norm, w_q, g_q, w_o, g_ffn_norm,
           w_ffn_gu, w_ffn_down, w_router, w_moe_gu, w_moe_down):
    p = dict(g_pool_norm=g_pool_norm, w_pool=w_pool, pool_scale=pool_scale,
             g_kv_norm=g_kv_norm, w_kvf=w_kvf, b_f=b_f, g_k=g_k, g_attn_norm=g_attn_norm,
             w_q=w_q, g_q=g_q, w_o=w_o, g_ffn_norm=g_ffn_norm, w_ffn_gu=w_ffn_gu,
             w_ffn_down=w_ffn_down, w_router=w_router, w_moe_gu=w_moe_gu,
             w_moe_down=w_moe_down)
    assert w_pool.shape[0] == 1 and w_q.shape[0] == 1 and w_router.shape[0] == 1
    moe_p, pool_p, k_p, v_p, lf_p = _trunk(x_prompt, None, None, p)
    moe_s, pool_s, k_s, v_s, lf_s = _trunk(
        x_sample, state_pool, (cache_k, cache_v, cache_logf), p)
    y_p, y_s = _moe([moe_p, moe_s], w_moe_gu[0], w_moe_down[0])
    return (y_p.reshape(x_prompt.shape), y_s.reshape(x_sample.shape), pool_p, k_p, v_p, lf_p,
            pool_s, k_s, v_s, lf_s)
```

```python
import functools
import math

import jax
import jax.numpy as jnp
import numpy as np
from jax import lax
from jax.experimental import pallas as pl
from jax.experimental.pallas import tpu as pltpu

EPS = 1e-6
POOL_WINDOWS = (2, 4, 8, 16)
HALO = max(POOL_WINDOWS)
POOL_HIST = HALO - 1
TOP_K = 2
LANES = 128
BF16_ROWS = 16
MXU_DIM = 256
VMEM_LIMIT = 56 * 1024 * 1024
NEG_INF = float("-inf")

BF16 = jnp.bfloat16
F32 = jnp.float32


def _params(semantics, vmem=None):
    return pltpu.CompilerParams(dimension_semantics=semantics, vmem_limit_bytes=vmem)


def _dot(a, b):
    return jnp.dot(a, b, preferred_element_type=F32)


def _dot_nt(a, b):
    return lax.dot_general(a, b, (((1,), (1,)), ((), ())), preferred_element_type=F32)


def _rms_base(x):
    return x * lax.rsqrt(jnp.mean(x * x, axis=-1, keepdims=True) + EPS)


def _split2(x):
    hi = x.astype(BF16)
    lo = (x - hi.astype(F32)).astype(BF16)
    return hi, lo


def _split3(x):
    hi = x.astype(BF16)
    r = x - hi.astype(F32)
    mid = r.astype(BF16)
    lo = (r - mid.astype(F32)).astype(BF16)
    return hi, mid, lo


def _ffn_chunks(f):
    out, s = [], 0
    while s < f:
        n = min(4 * MXU_DIM, f - s)
        out.append((s, n))
        s += n
    return out


def _pool_kernel(x_ref, halo_ref, g_ref, w_ref, scale_ref, *rest, tm, start_pos, halo_is_normed,
                 f):
    if f is None:
        h_ref, st_ref = rest
    else:
        gf_ref, wgu_ref, wd_ref, h_ref, st_ref = rest
    i = pl.program_id(1)
    x = x_ref[...]
    g = g_ref[...]
    u = _rms_base(x) * g
    if halo_is_normed:
        uh = halo_ref[...]
    else:
        uh = _rms_base(halo_ref[...]) * g
        uh = jnp.where(i > 0, uh, 0.0)
    ext = jnp.concatenate([uh, u], axis=0)

    row = lax.broadcasted_iota(jnp.int32, (tm, 1), 0)
    avail = start_pos + i * tm + row + 1
    group = x.shape[1] // len(POOL_WINDOWS)
    ys = []
    for gi, w in enumerate(POOL_WINDOWS):
        lo, hi = gi * group, (gi + 1) * group
        s = ext[:, lo:hi]
        step = 1
        while step < w:
            s = s + pltpu.roll(s, step, 0)
            step *= 2
        inv_cnt = 1.0 / jnp.minimum(avail, w).astype(F32)
        pooled = s[HALO:, :] * inv_cnt
        diff = pooled - u[:, lo:hi]
        ys.append(_dot(diff.astype(BF16), w_ref[gi]))
    h = x + jnp.concatenate(ys, axis=1) * scale_ref[...]
    if f is not None:
        h = h + _swiglu((_rms_base(h) * gf_ref[...]).astype(BF16), wgu_ref, wd_ref, f)
    h_ref[...] = h

    @pl.when(i == pl.num_programs(1) - 1)
    def _():
        st_ref[...] = u[tm - HALO:, :]


def _pool_layer(x, hist, g, w_pool, pool_scale, start_pos, ffn=None):
    b, n, d = x.shape
    tm = min(n, 256 if ffn is None else 512)
    assert n % tm == 0 and tm % HALO == 0 and n >= HALO
    nt = n // tm
    if hist is None:
        halo_arr = x
        halo_spec = pl.BlockSpec(
            (None, HALO, d), lambda bi, i: (bi, jnp.maximum(i * (tm // HALO) - 1, 0), 0))
    else:
        assert nt == 1
        halo_arr = jnp.pad(hist, ((0, 0), (HALO - POOL_HIST, 0), (0, 0)))
        halo_spec = pl.BlockSpec((None, HALO, d), lambda bi, i: (bi, 0, 0))
    ng = len(POOL_WINDOWS)
    vec = pl.BlockSpec((1, d), lambda bi, i: (0, 0))
    in_specs = [pl.BlockSpec((None, tm, d), lambda bi, i: (bi, i, 0)), halo_spec, vec,
                pl.BlockSpec((ng, d // ng, d // ng), lambda bi, i: (0, 0, 0)), vec]
    args = [x, halo_arr, g.reshape(1, d), w_pool.astype(BF16), pool_scale.reshape(1, d)]
    f = None
    if ffn is not None:
        g_ffn, w_gu, w_down = ffn
        f = w_down.shape[0]
        in_specs += [vec, pl.BlockSpec((d, 2 * f), lambda bi, i: (0, 0)),
                     pl.BlockSpec((f, d), lambda bi, i: (0, 0))]
        args += [g_ffn.reshape(1, d), w_gu.astype(BF16), w_down.astype(BF16)]
    h, st = pl.pallas_call(
        functools.partial(_pool_kernel, tm=tm, start_pos=start_pos,
                          halo_is_normed=hist is not None, f=f),
        grid=(b, nt),
        in_specs=in_specs,
        out_specs=[
            pl.BlockSpec((None, tm, d), lambda bi, i: (bi, i, 0)),
            pl.BlockSpec((None, HALO, d), lambda bi, i: (bi, 0, 0)),
        ],
        out_shape=[jax.ShapeDtypeStruct((b, n, d), F32),
                   jax.ShapeDtypeStruct((b, HALO, d), F32)],
        compiler_params=_params(("parallel", "arbitrary"), None if ffn is None else VMEM_LIMIT),
        name="pool_mixer" if ffn is None else "pool_ffn",
    )(*args)
    return h, st[:, HALO - POOL_HIST:, :]


def _swiglu(xb, wgu_ref, wd_ref, f):
    acc = None
    for s, n in _ffn_chunks(f):
        gate = _dot(xb, wgu_ref[:, s:s + n])
        up = _dot(xb, wgu_ref[:, f + s:f + s + n])
        act = (gate * jax.nn.sigmoid(gate)) * up
        part = _dot(act.astype(BF16), wd_ref[s:s + n, :])
        acc = part if acc is None else acc + part
    return acc


def _dense_ffn_kernel(h_ref, g_ref, wgu_ref, wd_ref, o_ref, *, f):
    h = h_ref[...]
    u = (_rms_base(h) * g_ref[...]).astype(BF16)
    o_ref[...] = h + _swiglu(u, wgu_ref, wd_ref, f)


def _dense_ffn(h, g, w_gu, w_down):
    t, d = h.shape
    f = w_down.shape[0]
    tm = min(t, 512)
    assert t % tm == 0
    return pl.pallas_call(
        functools.partial(_dense_ffn_kernel, f=f),
        grid=(t // tm,),
        in_specs=[
            pl.BlockSpec((tm, d), lambda i: (i, 0)),
            pl.BlockSpec((1, d), lambda i: (0, 0)),
            pl.BlockSpec((d, 2 * f), lambda i: (0, 0)),
            pl.BlockSpec((f, d), lambda i: (0, 0)),
        ],
        out_specs=pl.BlockSpec((tm, d), lambda i: (i, 0)),
        out_shape=jax.ShapeDtypeStruct((t, d), F32),
        compiler_params=_params(("parallel",), VMEM_LIMIT),
        name="dense_ffn",
    )(h, g.reshape(1, d), w_gu.astype(BF16), w_down.astype(BF16))


def _head_norm(x, bd_ref, g):
    sq_hi, sq_lo = _split2(x * x)
    n = bd_ref.shape[0]
    ms = jnp.concatenate(
        [_dot(sq_hi[:, c:c + n], bd_ref[...]) + _dot(sq_lo[:, c:c + n], bd_ref[...])
         for c in range(0, x.shape[1], n)], axis=1)
    return x * lax.rsqrt(ms + EPS) * g


def _log_sigmoid(z):
    return jnp.minimum(z, 0.0) - jnp.log1p(jnp.exp(-jnp.abs(z)))


def _qkv_kernel(h_ref, gkv_ref, gat_ref, wk_ref, wv_ref, wf_ref, wq_ref, bf_ref,
                gk_ref, gq_ref, bd_ref,
                k_ref, v_ref, lf_ref, kb_ref, vb_ref, qb_ref, *, n_heads, q_scale):
    base = _rms_base(h_ref[...])
    a_kv = (base * gkv_ref[...]).astype(BF16)
    a_q = (base * gat_ref[...]).astype(BF16)
    k = _head_norm(_dot(a_kv, wk_ref[...]), bd_ref, gk_ref[...])
    v = _dot(a_kv, wv_ref[...])
    z = _dot(a_kv, wf_ref[...])[:, :n_heads] + bf_ref[...]
    q = _head_norm(_dot(a_q, wq_ref[...]), bd_ref, gq_ref[...])
    k_ref[...] = k
    v_ref[...] = v
    lf_ref[...] = _log_sigmoid(z)
    kb_ref[...] = k.astype(BF16)
    vb_ref[...] = v.astype(BF16)
    qb_ref[...] = (q * q_scale).astype(BF16)


def _qkv_proj(h, g_kv, g_attn, w_kvf, b_f, g_k, w_q, g_q, n_heads):
    t, d = h.shape
    kvw = w_q.shape[1]
    hd = kvw // n_heads
    tm = min(t, 256)
    assert t % tm == 0 and MXU_DIM % hd == 0 and kvw % MXU_DIM == 0
    wk = w_kvf[:, :kvw].astype(BF16)
    wv = w_kvf[:, kvw:2 * kvw].astype(BF16)
    wf = jnp.pad(w_kvf[:, 2 * kvw:], ((0, 0), (0, LANES - n_heads))).astype(BF16)
    head = jnp.arange(MXU_DIM) // hd
    bd = ((head[:, None] == head[None, :]).astype(F32) / hd).astype(BF16)
    row = lambda i: (i, 0)
    fix = lambda i: (0, 0)
    return pl.pallas_call(
        functools.partial(_qkv_kernel, n_heads=n_heads, q_scale=hd ** -0.5),
        grid=(t // tm,),
        in_specs=[
            pl.BlockSpec((tm, d), row),
            pl.BlockSpec((1, d), fix), pl.BlockSpec((1, d), fix),
            pl.BlockSpec((d, kvw), fix), pl.BlockSpec((d, kvw), fix),
            pl.BlockSpec((d, LANES), fix), pl.BlockSpec((d, kvw), fix),
            pl.BlockSpec((1, n_heads), fix),
            pl.BlockSpec((1, kvw), fix), pl.BlockSpec((1, kvw), fix),
            pl.BlockSpec((MXU_DIM, MXU_DIM), fix),
        ],
        out_specs=[
            pl.BlockSpec((tm, kvw), row), pl.BlockSpec((tm, kvw), row),
            pl.BlockSpec((tm, n_heads), row),
            pl.BlockSpec((tm, kvw), row), pl.BlockSpec((tm, kvw), row),
            pl.BlockSpec((tm, kvw), row),
        ],
        out_shape=[
            jax.ShapeDtypeStruct((t, kvw), F32), jax.ShapeDtypeStruct((t, kvw), F32),
            jax.ShapeDtypeStruct((t, n_heads), F32),
            jax.ShapeDtypeStruct((t, kvw), BF16), jax.ShapeDtypeStruct((t, kvw), BF16),
            jax.ShapeDtypeStruct((t, kvw), BF16),
        ],
        compiler_params=_params(("parallel",), VMEM_LIMIT),
        name="qkv_proj",
    )(h, g_kv.reshape(1, d), g_attn.reshape(1, d), wk, wv, wf, w_q.astype(BF16),
      b_f.reshape(1, n_heads), jnp.tile(g_k, n_heads).reshape(1, kvw),
      jnp.tile(g_q, n_heads).reshape(1, kvw), bd)


def _cumsum_tile(x, tri, carry):
    y = carry
    for part in _split3(x):
        y = y + _dot(part, tri)
    return y


def _bias_placement(n_heads, hd, c_first):
    p = np.zeros((4 * n_heads, n_heads * LANES), np.float32)
    for h in range(n_heads):
        first = h * LANES + (h * hd % LANES + hd) % LANES
        for i in range(3):
            term, one = (first + i, first + 3 + i) if c_first else (first + 3 + i, first + i)
            p[i * n_heads + h, term] = 1.0 if c_first else -1.0
            p[3 * n_heads, one] = 1.0
    return jnp.asarray(p, BF16)


def _augment(x, c_terms, place_ref, n_heads, hd):
    ext = _dot(c_terms, place_ref[...]).astype(BF16)
    placed = jnp.concatenate(
        [x[:, h * hd // LANES * LANES:h * hd // LANES * LANES + LANES] for h in range(n_heads)],
        axis=1)
    lane = lax.broadcasted_iota(jnp.int32, (1, n_heads * LANES), 1)
    head, within = lane // LANES, lane % LANES
    own = within // hd == head % (LANES // hd)
    return jnp.where(own, placed, ext)


def _qkv_t_kernel(h_ref, gkv_ref, gat_ref, wkt_ref, wvt_ref, wf_ref, wq_ref, bf_ref,
                  gk_ref, gq_ref, bd_ref, tri_ref, pk_ref, pq_ref,
                  kt_ref, vt_ref, lf_ref, kaug_ref, vtb_ref, qaug_ref, carry_ref, *,
                  n_heads, q_scale):
    @pl.when(pl.program_id(1) == 0)
    def _():
        carry_ref[...] = jnp.zeros(carry_ref.shape, F32)

    base = _rms_base(h_ref[...])
    a_kv = (base * gkv_ref[...]).astype(BF16)
    a_q = (base * gat_ref[...]).astype(BF16)
    tm = a_kv.shape[0]
    kraw = _dot_nt(wkt_ref[...], a_kv)
    hd = kraw.shape[0] // n_heads
    k3 = kraw.reshape(n_heads, hd, tm)
    ms = jnp.mean(k3 * k3, axis=1, keepdims=True)
    kt = (k3 * lax.rsqrt(ms + EPS) * gk_ref[...][None]).reshape(kraw.shape)
    vt = _dot_nt(wvt_ref[...], a_kv)
    lf = _log_sigmoid(_dot(a_kv, wf_ref[...])[:, :n_heads] + bf_ref[...])
    sums = _dot(tri_ref[...], jnp.concatenate(_split3(lf), axis=1))
    c = carry_ref[...] + sum(sums[:, i * n_heads:(i + 1) * n_heads] for i in range(3))
    carry_ref[...] = c[tm - 1:tm, :]
    q = _head_norm(_dot(a_q, wq_ref[...]), bd_ref, gq_ref[...])
    kt_ref[...] = kt
    vt_ref[...] = vt
    lf_ref[...] = lf
    vtb_ref[...] = vt.astype(BF16)
    c_terms = jnp.concatenate(
        _split3(c * math.log2(math.e)) + (jnp.ones((tm, n_heads), BF16),), axis=1)
    k_aug = _augment(kt.T.astype(BF16), c_terms, pk_ref, n_heads, hd)
    q_aug = _augment((q * q_scale).astype(BF16), c_terms, pq_ref, n_heads, hd)
    for h in range(n_heads):
        kaug_ref[h] = k_aug[:, h * LANES:(h + 1) * LANES]
        qaug_ref[h] = q_aug[:, h * LANES:(h + 1) * LANES]


def _qkv_proj_t(h, b, g_kv, g_attn, w_kvf, b_f, g_k, w_q, g_q, n_heads):
    t, d = h.shape
    s = t // b
    kvw = w_q.shape[1]
    hd = kvw // n_heads
    tm = min(s, 512)
    nt = s // tm
    assert s % tm == 0 and MXU_DIM % hd == 0 and kvw % MXU_DIM == 0
    assert LANES % hd == 0 and hd + 6 <= LANES
    wkt = w_kvf[:, :kvw].T.astype(BF16)
    wvt = w_kvf[:, kvw:2 * kvw].T.astype(BF16)
    wf = jnp.pad(w_kvf[:, 2 * kvw:], ((0, 0), (0, LANES - n_heads))).astype(BF16)
    head = jnp.arange(MXU_DIM) // hd
    bd = ((head[:, None] == head[None, :]).astype(F32) / hd).astype(BF16)
    pos = jnp.arange(tm)
    tri = (pos[:, None] >= pos[None, :]).astype(BF16)
    fix = lambda bi, i: (0, 0)
    feat = pl.BlockSpec((None, kvw, tm), lambda bi, i: (bi, 0, i))
    aug = pl.BlockSpec((None, n_heads, tm, LANES), lambda bi, i: (bi, 0, i, 0))
    return pl.pallas_call(
        functools.partial(_qkv_t_kernel, n_heads=n_heads,
                          q_scale=hd ** -0.5 * math.log2(math.e)),
        grid=(b, nt),
        in_specs=[
            pl.BlockSpec((tm, d), lambda bi, i: (bi * nt + i, 0)),
            pl.BlockSpec((1, d), fix), pl.BlockSpec((1, d), fix),
            pl.BlockSpec((kvw, d), fix), pl.BlockSpec((kvw, d), fix),
            pl.BlockSpec((d, LANES), fix), pl.BlockSpec((d, kvw), fix),
            pl.BlockSpec((1, n_heads), fix), pl.BlockSpec((hd, tm), fix),
            pl.BlockSpec((1, kvw), fix), pl.BlockSpec((MXU_DIM, MXU_DIM), fix),
            pl.BlockSpec((tm, tm), fix),
            pl.BlockSpec((4 * n_heads, n_heads * LANES), fix),
            pl.BlockSpec((4 * n_heads, n_heads * LANES), fix),
        ],
        out_specs=[feat, feat, pl.BlockSpec((tm, n_heads), lambda bi, i: (bi * nt + i, 0)),
                   aug, feat, aug],
        out_shape=[jax.ShapeDtypeStruct((b, kvw, s), F32), jax.ShapeDtypeStruct((b, kvw, s), F32),
                   jax.ShapeDtypeStruct((t, n_heads), F32),
                   jax.ShapeDtypeStruct((b, n_heads, s, LANES), BF16),
                   jax.ShapeDtypeStruct((b, kvw, s), BF16),
                   jax.ShapeDtypeStruct((b, n_heads, s, LANES), BF16)],
        scratch_shapes=[pltpu.VMEM((1, n_heads), F32)],
        compiler_params=_params(("parallel", "arbitrary"), VMEM_LIMIT),
        name="qkv_proj_t",
    )(h, g_kv.reshape(1, d), g_attn.reshape(1, d), wkt, wvt, wf, w_q.astype(BF16),
      b_f.reshape(1, n_heads), jnp.broadcast_to(g_k[:, None], (hd, tm)),
      jnp.tile(g_q, n_heads).reshape(1, kvw), bd, tri,
      _bias_placement(n_heads, hd, False), _bias_placement(n_heads, hd, True))


def _cumsum_kernel(x_ref, o_ref):
    r, l = x_ref.shape
    ii = lax.broadcasted_iota(jnp.int32, (LANES, LANES), 0)
    jj = lax.broadcasted_iota(jnp.int32, (LANES, LANES), 1)
    tri = (ii <= jj).astype(BF16)
    carry = jnp.zeros((r, 1), F32)
    for c in range(0, l, LANES):
        y = _cumsum_tile(x_ref[:, c:c + LANES], tri, carry)
        o_ref[:, c:c + LANES] = y
        carry = y[:, LANES - 1:LANES]


def _cumsum_lanes(x):
    r, l = x.shape
    tr = min(r, 256)
    assert r % tr == 0 and l % LANES == 0
    return pl.pallas_call(
        _cumsum_kernel,
        grid=(r // tr,),
        in_specs=[pl.BlockSpec((tr, l), lambda i: (i, 0))],
        out_specs=pl.BlockSpec((tr, l), lambda i: (i, 0)),
        out_shape=jax.ShapeDtypeStruct((r, l), F32),
        compiler_params=_params(("parallel",)),
        name="logf_cumsum",
    )(x)


def _bdot(a, b, ca, cb):
    return lax.dot_general(a, b, (((ca,), (cb,)), ((0,), (0,))), preferred_element_type=F32)


def _attn_prompt_kernel(q_ref, kaug_ref, vt_ref, o_ref, m_ref, acc_ref, *, tq):
    qi = pl.program_id(1)
    n_heads = q_ref.shape[0]
    kvw = vt_ref.shape[0]
    hd = kvw // n_heads
    q_aug = q_ref[...]
    m_ref[...] = jnp.full(m_ref.shape, NEG_INF, F32)
    acc_ref[...] = jnp.zeros(acc_ref.shape, F32)
    n_den = acc_ref.shape[1] - hd
    den_rows = (lax.broadcasted_iota(jnp.int32, (n_heads, n_den, tq), 1) == 0).astype(BF16)

    def tile(kt, masked):
        start = pl.multiple_of(kt * tq, tq)
        st = _bdot(kaug_ref[:, pl.ds(start, tq), :], q_aug, 2, 2)
        if masked:
            key = lax.broadcasted_iota(jnp.int32, (1, tq, tq), 1)
            qry = lax.broadcasted_iota(jnp.int32, (1, tq, tq), 2)
            st = jnp.where(key <= qry, st, NEG_INF)
        m_prev = m_ref[...]
        m_new = jnp.maximum(m_prev, jnp.max(st, axis=1, keepdims=True))
        alpha = jnp.exp2(m_prev - m_new)
        p = jnp.exp2(st - m_new)
        vt = vt_ref[:, pl.ds(start, tq)].reshape(n_heads, hd, tq)
        vt = jnp.concatenate([vt, den_rows], axis=1)
        acc_ref[...] = alpha * acc_ref[...] + _bdot(vt, p.astype(BF16), 2, 1)
        m_ref[...] = m_new

    def body(kt, carry):
        tile(kt, False)
        return carry

    lax.fori_loop(0, qi, body, 0)
    tile(qi, True)
    ot = (acc_ref[:, :hd, :] / acc_ref[:, hd:hd + 1, :]).reshape(kvw, tq)
    o_ref[...] = ot.T.astype(o_ref.dtype)


def _attn_prompt(q_aug, k_aug, vt):
    b, n_heads, s, _ = q_aug.shape
    kvw = vt.shape[1]
    hd = kvw // n_heads
    tq = min(s, 256)
    assert s % tq == 0
    return pl.pallas_call(
        functools.partial(_attn_prompt_kernel, tq=tq),
        grid=(b, s // tq),
        in_specs=[
            pl.BlockSpec((None, n_heads, tq, LANES), lambda bi, qi: (bi, 0, qi, 0)),
            pl.BlockSpec((None, n_heads, s, LANES), lambda bi, qi: (bi, 0, 0, 0)),
            pl.BlockSpec((None, kvw, s), lambda bi, qi: (bi, 0, 0)),
        ],
        out_specs=pl.BlockSpec((None, tq, kvw), lambda bi, qi: (bi, qi, 0)),
        out_shape=jax.ShapeDtypeStruct((b, s, kvw), BF16),
        scratch_shapes=[pltpu.VMEM((n_heads, 1, tq), F32),
                        pltpu.VMEM((n_heads, hd + BF16_ROWS, tq), F32)],
        compiler_params=_params(("parallel", "arbitrary"), VMEM_LIMIT),
        name="fox_prompt",
    )(q_aug, k_aug, vt)


def _attn_sample_kernel(q_ref, cq_ref, kc_ref, vc_ref, ckc_ref, kn_ref, vn_ref, ckn_ref,
                        o_ref, m_ref, acc_ref, *, n_heads):
    kt = pl.program_id(1)
    last = pl.num_programs(1) - 1
    q = q_ref[...]
    cq = cq_ref[...]
    n, hd = q.shape[1:]
    n_den = acc_ref.shape[2] - hd

    @pl.when(kt == 0)
    def _():
        m_ref[...] = jnp.full(m_ref.shape, NEG_INF, F32)
        acc_ref[...] = jnp.zeros(acc_ref.shape, F32)

    def update(s, v, v_contract):
        m_prev = m_ref[...]
        m_new = jnp.maximum(m_prev, jnp.max(s, axis=2, keepdims=True))
        p = jnp.exp(s - m_new).astype(BF16)
        acc_ref[...] = jnp.exp(m_prev - m_new) * acc_ref[...] + _bdot(p, v, 2, v_contract)
        m_ref[...] = m_new

    kvw, tk = kc_ref.shape
    split = (n_heads, kvw // n_heads, tk)
    kk = kc_ref[...].reshape(split).astype(BF16)
    den = (lax.broadcasted_iota(jnp.int32, (n_heads, n_den, tk), 1) == 0).astype(BF16)
    vv = jnp.concatenate([vc_ref[...].reshape(split).astype(BF16), den], axis=1)
    update(_bdot(q, kk, 2, 1) + (cq - ckc_ref[...]), vv, 2)

    @pl.when(kt == last)
    def _():
        r = lax.broadcasted_iota(jnp.int32, (1, n, n), 1)
        c = lax.broadcasted_iota(jnp.int32, (1, n, n), 2)
        s = _bdot(q, kn_ref[...], 2, 2) + (cq - ckn_ref[...])
        den_new = (lax.broadcasted_iota(jnp.int32, (n_heads, n, n_den), 2) == 0).astype(BF16)
        update(jnp.where(c <= r, s, NEG_INF), jnp.concatenate([vn_ref[...], den_new], axis=2), 1)
        acc = acc_ref[...]
        o_ref[...] = (acc[:, :, :hd] / acc[:, :, hd:hd + 1]).astype(o_ref.dtype)


def _attn_sample(q, k_new, v_new, cache_kt, cache_vt, cq, ck_cache, ck_new):
    b, n_heads, n, hd = q.shape
    kvw, p = cache_kt.shape[1:]
    tk = min(p, 1024)
    assert p % tk == 0 and kvw == n_heads * hd
    fix = lambda bi, kt: (bi, 0, 0, 0)
    rows = pl.BlockSpec((None, n_heads, n, hd), fix)
    cache = pl.BlockSpec((None, kvw, tk), lambda bi, kt: (bi, 0, kt))
    return pl.pallas_call(
        functools.partial(_attn_sample_kernel, n_heads=n_heads),
        grid=(b, p // tk),
        in_specs=[
            rows,
            pl.BlockSpec((None, n_heads, n, 1), fix),
            cache, cache,
            pl.BlockSpec((None, n_heads, 1, tk), lambda bi, kt: (bi, 0, 0, kt)),
            rows, rows,
            pl.BlockSpec((None, n_heads, 1, n), fix),
        ],
        out_specs=rows,
        out_shape=jax.ShapeDtypeStruct((b, n_heads, n, hd), BF16),
        scratch_shapes=[pltpu.VMEM((n_heads, n, 1), F32),
                        pltpu.VMEM((n_heads, n, hd + BF16_ROWS), F32)],
        compiler_params=_params(("parallel", "arbitrary"), VMEM_LIMIT),
        name="fox_sample",
    )(q, cq, cache_kt, cache_vt, ck_cache, k_new, v_new, ck_new)


def _store_token_tiles(ref, x):
    m, d = x.shape
    nsub = d // LANES
    for s in range(nsub):
        ref[pl.ds(s, m, stride=nsub), :] = x[:, s * LANES:(s + 1) * LANES]


def _load_token_tiles(ref, m, nsub, first=0, stride=None):
    stride = nsub if stride is None else stride
    return jnp.concatenate(
        [ref[pl.ds(first + s, m, stride=stride), :] for s in range(nsub)], axis=1)


def _oproj_router_kernel(h_ref, o_ref, wo_ref, g_ref, wr_ref, h2_ref, u_ref, idx_ref, w_ref, *,
                         n_experts):
    h2 = h_ref[...] + _dot(o_ref[...], wo_ref[...])
    u = _rms_base(h2) * g_ref[...]
    h2_ref[...] = h2
    _store_token_tiles(u_ref, u)
    u_hi, u_lo = _split2(u)
    both = _dot(u_hi, wr_ref[...])
    logits = both[:, :LANES] + both[:, LANES:] + _dot(u_lo, wr_ref[:, :LANES])
    lane = lax.broadcasted_iota(jnp.int32, logits.shape, 1).astype(F32)
    logits = jnp.where(lane < n_experts, logits, NEG_INF)
    m1 = jnp.max(logits, axis=1, keepdims=True)
    i1 = jnp.min(jnp.where(logits == m1, lane, float(LANES)), axis=1, keepdims=True)
    rest = jnp.where(lane == i1, NEG_INF, logits)
    m2 = jnp.max(rest, axis=1, keepdims=True)
    i2 = jnp.min(jnp.where(rest == m2, lane, float(LANES)), axis=1, keepdims=True)
    e2 = jnp.exp(m2 - m1)
    w1 = 1.0 / (1.0 + e2)
    w2 = e2 / (1.0 + e2)
    idx_ref[...] = jnp.concatenate([i1, i2], axis=1).astype(jnp.int32)
    w_ref[...] = jnp.concatenate([w1, w2], axis=1)


def _oproj_router(h, o, w_o, g, w_router):
    t, d = h.shape
    kvw = o.shape[1]
    ne = w_router.shape[1]
    tm = min(t, 512)
    nsub = d // LANES
    assert t % tm == 0 and ne <= LANES
    wr = jnp.pad(w_router, ((0, 0), (0, LANES - ne)))
    wr_hi = wr.astype(BF16)
    wr_lo = (wr - wr_hi.astype(F32)).astype(BF16)
    row = lambda i: (i, 0)
    return pl.pallas_call(
        functools.partial(_oproj_router_kernel, n_experts=ne),
        grid=(t // tm,),
        in_specs=[
            pl.BlockSpec((tm, d), row), pl.BlockSpec((tm, kvw), row),
            pl.BlockSpec((kvw, d), lambda i: (0, 0)),
            pl.BlockSpec((1, d), lambda i: (0, 0)),
            pl.BlockSpec((d, 2 * LANES), lambda i: (0, 0)),
        ],
        out_specs=[pl.BlockSpec((tm, d), row), pl.BlockSpec((tm * nsub, LANES), row),
                   pl.BlockSpec((tm, TOP_K), row), pl.BlockSpec((tm, TOP_K), row)],
        out_shape=[jax.ShapeDtypeStruct((t, d), F32),
                   jax.ShapeDtypeStruct((t * nsub, LANES), F32),
                   jax.ShapeDtypeStruct((t, TOP_K), jnp.int32),
                   jax.ShapeDtypeStruct((t, TOP_K), F32)],
        compiler_params=_params(("parallel",)),
        name="oproj_router",
    )(h, o, w_o.astype(BF16), g.reshape(1, d), jnp.concatenate([wr_hi, wr_lo], axis=1))


MOE_ROWS = 512
DMA_UNROLL = 8


def _route_plan(idx, ne, tmg):
    t = idx.shape[0]
    npairs = t * TOP_K
    e_flat = idx.reshape(npairs)
    onehot = (e_flat[:, None] == jnp.arange(ne, dtype=jnp.int32)[None, :]).astype(jnp.int32)
    csum = jnp.cumsum(onehot, axis=0)
    cnt = csum[-1]
    padded = (cnt + tmg - 1) // tmg * tmg
    ends = jnp.cumsum(padded)
    off = ends - padded
    dest = jnp.sum(onehot * (off[None, :] + csum), axis=1) - 1
    total = ends[-1]
    k = jnp.arange(tmg, dtype=jnp.int32)[None, :]
    valid = (k < (padded - cnt)[:, None]).reshape(-1)
    pad_row = ((off + cnt)[:, None] + k).reshape(-1)
    tail_row = total + jnp.cumsum(jnp.logical_not(valid).astype(jnp.int32)) - 1
    fill = jnp.where(valid, pad_row, tail_row)
    n_tiles = total // tmg
    nt_max = (npairs + ne * tmg) // tmg
    tile = jnp.minimum(jnp.arange(nt_max, dtype=jnp.int32), n_tiles - 1)
    tile_expert = jnp.sum((tile[:, None] >= (ends // tmg)[None, :]).astype(jnp.int32), axis=1)
    return dest, fill, tile_expert, n_tiles.reshape(1)


def _step_ranges(steps):
    out, first = [], 0
    for n in steps:
        out.append((first, n))
        first += n
    return out


def _in_range(i, first, count):
    return jnp.logical_and(i >= first, i < first + count)


def _dispatch_kernel(dest_ref, fill_ref, *rest, ch, ranges, nsub):
    u_refs, (xg_hbm, zero_ref, sem) = rest[:len(ranges)], rest[len(ranges):]
    i = pl.program_id(0)

    def issue(src_row, dst_rows):
        def body(j, carry):
            for jj in range(DMA_UNROLL):
                pltpu.make_async_copy(src_row(j, jj), xg_hbm.at[dst_rows[0, j * DMA_UNROLL + jj]],
                                      sem.at[0]).start(priority=jj % 2)
            return carry
        lax.fori_loop(0, ch // DMA_UNROLL, body, 0)

    for u_ref, (first, count) in zip(u_refs, ranges):
        def token_tile(j, jj, u_ref=u_ref):
            tok = j * (DMA_UNROLL // TOP_K) + jj // TOP_K
            return u_ref.at[pl.ds(pl.multiple_of(tok * nsub, nsub), nsub), :]

        @pl.when(_in_range(i, first, count))
        def _(token_tile=token_tile):
            issue(token_tile, dest_ref)

    pair_steps = ranges[-1][0] + ranges[-1][1]

    @pl.when(i >= pair_steps)
    def _():
        zero_ref[...] = jnp.zeros(zero_ref.shape, zero_ref.dtype)
        issue(lambda j, jj: zero_ref, fill_ref)

    pltpu.make_async_copy(xg_hbm.at[pl.ds(0, ch)], xg_hbm.at[pl.ds(0, ch)], sem.at[0]).wait()


def _dispatch(u_groups, dest, fill, n_rows, nsub):
    pairs = [u.shape[0] // nsub * TOP_K for u in u_groups]
    assert sum(pairs) == dest.shape[0] and DMA_UNROLL % TOP_K == 0
    ch = math.gcd(2 * MOE_ROWS, *pairs)
    assert fill.shape[0] % ch == 0 and ch % DMA_UNROLL == 0
    ranges = _step_ranges([p // ch for p in pairs])
    pair_steps = sum(p // ch for p in pairs)
    fill_steps = fill.shape[0] // ch
    smem = pltpu.SMEM
    clamp = lambda i, first, count: jnp.clip(i - first, 0, count - 1)
    return pl.pallas_call(
        functools.partial(_dispatch_kernel, ch=ch, ranges=ranges, nsub=nsub),
        grid=(pair_steps + fill_steps,),
        in_specs=[
            pl.BlockSpec((None, 1, ch), lambda i: (jnp.minimum(i, pair_steps - 1), 0, 0),
                         memory_space=smem),
            pl.BlockSpec((None, 1, ch), lambda i: (jnp.maximum(i - pair_steps, 0), 0, 0),
                         memory_space=smem),
        ] + [pl.BlockSpec((ch // TOP_K * nsub, LANES),
                          lambda i, first=first, count=count: (clamp(i, first, count), 0))
             for first, count in ranges],
        out_specs=pl.BlockSpec(memory_space=pl.ANY),
        out_shape=jax.ShapeDtypeStruct((n_rows, nsub, LANES), F32),
        scratch_shapes=[pltpu.VMEM((nsub, LANES), F32), pltpu.SemaphoreType.DMA((1,))],
        compiler_params=_params(("arbitrary",)),
        name="moe_dispatch",
    )(dest.reshape(pair_steps, 1, ch), fill.reshape(fill_steps, 1, ch), *u_groups)


def _moe_ffn_kernel(te_ref, nt_ref, x_ref, wgu_ref, wd_ref, o_ref, *, f, tmg, nsub):
    del te_ref
    live = pl.program_id(0) < nt_ref[0]

    @pl.when(live)
    def _():
        x = _load_token_tiles(x_ref, tmg, nsub).astype(BF16)
        _store_token_tiles(o_ref, _swiglu(x, wgu_ref, wd_ref, f))

    @pl.when(jnp.logical_not(live))
    def _():
        o_ref[...] = jnp.zeros(o_ref.shape, o_ref.dtype)


def _moe_ffn(xg, tile_expert, n_tiles, w_gu, w_down, tmg):
    ne, f, d = w_down.shape
    nsub = d // LANES
    nt_max = tile_expert.shape[0]
    assert xg.shape[0] == nt_max * tmg * nsub
    rows = lambda i, te, nt: (jnp.minimum(i, nt[0] - 1), 0)
    return pl.pallas_call(
        functools.partial(_moe_ffn_kernel, f=f, tmg=tmg, nsub=nsub),
        grid_spec=pltpu.PrefetchScalarGridSpec(
            num_scalar_prefetch=2,
            grid=(nt_max,),
            in_specs=[
                pl.BlockSpec((tmg * nsub, LANES), rows),
                pl.BlockSpec((None, d, 2 * f), lambda i, te, nt: (te[i], 0, 0)),
                pl.BlockSpec((None, f, d), lambda i, te, nt: (te[i], 0, 0)),
            ],
            out_specs=pl.BlockSpec((tmg * nsub, LANES), lambda i, te, nt: (i, 0)),
        ),
        out_shape=jax.ShapeDtypeStruct(xg.shape, F32),
        compiler_params=_params(("arbitrary",), VMEM_LIMIT),
        name="moe_experts",
    )(tile_expert, n_tiles, xg, w_gu.astype(BF16), w_down.astype(BF16))


def _combine_kernel(dcur_ref, dnxt_ref, w_ref, og_hbm, *rest, tm, nsub, ranges):
    ng = len(ranges)
    h_refs, y_refs, (buf, sem) = rest[:ng], rest[ng:2 * ng], rest[2 * ng:]
    i = pl.program_id(0)
    n = pl.num_programs(0)
    rows = TOP_K * tm

    def issue(d_ref, slot):
        def body(j, carry):
            for jj in range(DMA_UNROLL):
                tok = j * (DMA_UNROLL // TOP_K) + jj // TOP_K
                dst = buf.at[slot, jj % TOP_K, pl.ds(pl.multiple_of(tok * nsub, nsub), nsub), :]
                pltpu.make_async_copy(og_hbm.at[d_ref[0, j * DMA_UNROLL + jj]], dst,
                                      sem.at[slot]).start(priority=jj % 2)
            return carry
        lax.fori_loop(0, rows // DMA_UNROLL, body, 0)

    @pl.when(i == 0)
    def _():
        issue(dcur_ref, 0)

    @pl.when(i + 1 < n)
    def _():
        issue(dnxt_ref, lax.rem(i + 1, 2))

    slot = lax.rem(i, 2)
    pltpu.make_async_copy(buf.at[slot], buf.at[slot], sem.at[slot]).wait()
    w = w_ref[...]
    moe = (w[:, 0:1] * _load_token_tiles(buf.at[slot, 0], tm, nsub)
           + w[:, 1:2] * _load_token_tiles(buf.at[slot, 1], tm, nsub))
    for h_ref, y_ref, (first, count) in zip(h_refs, y_refs, ranges):
        @pl.when(_in_range(i, first, count))
        def _(h_ref=h_ref, y_ref=y_ref):
            y_ref[...] = h_ref[...] + moe


def _combine(h_groups, w, og, dest):
    assert TOP_K == 2
    d = h_groups[0].shape[1]
    nsub = d // LANES
    tm = math.gcd(256, *[h.shape[0] for h in h_groups])
    assert (TOP_K * tm) % DMA_UNROLL == 0
    ranges = _step_ranges([h.shape[0] // tm for h in h_groups])
    n = sum(count for _, count in ranges)
    dest3 = dest.reshape(n, 1, TOP_K * tm)
    smem = pltpu.SMEM
    clamp = lambda i, first, count: jnp.clip(i - first, 0, count - 1)
    rows = [pl.BlockSpec((tm, d), lambda i, first=first, count=count: (clamp(i, first, count), 0))
            for first, count in ranges]
    return pl.pallas_call(
        functools.partial(_combine_kernel, tm=tm, nsub=nsub, ranges=ranges),
        grid=(n,),
        in_specs=[
            pl.BlockSpec((None, 1, TOP_K * tm), lambda i: (i, 0, 0), memory_space=smem),
            pl.BlockSpec((None, 1, TOP_K * tm), lambda i: (jnp.minimum(i + 1, n - 1), 0, 0),
                         memory_space=smem),
            pl.BlockSpec((tm, TOP_K), lambda i: (i, 0)),
            pl.BlockSpec(memory_space=pl.ANY),
        ] + rows,
        out_specs=rows,
        out_shape=[jax.ShapeDtypeStruct(h.shape, F32) for h in h_groups],
        scratch_shapes=[pltpu.VMEM((2, TOP_K, tm * nsub, LANES), F32),
                        pltpu.SemaphoreType.DMA((2,))],
        compiler_params=_params(("arbitrary",)),
        name="moe_combine",
    )(dest3, dest3, w, og, *h_groups)


def _moe(groups, w_gu, w_down):
    ne, _, d = w_down.shape
    nsub = d // LANES
    idx = jnp.concatenate([g[2] for g in groups], axis=0)
    w = jnp.concatenate([g[3] for g in groups], axis=0)
    npairs = idx.shape[0] * TOP_K
    tmg = math.gcd(MOE_ROWS, npairs)
    dest, fill, tile_expert, n_tiles = _route_plan(idx, ne, tmg)
    n_rows = npairs + ne * tmg
    xg = _dispatch([g[1] for g in groups], dest, fill, n_rows, nsub)
    og = _moe_ffn(xg.reshape(n_rows * nsub, LANES), tile_expert, n_tiles, w_gu, w_down, tmg)
    return _combine([g[0] for g in groups], w, og.reshape(n_rows, nsub, LANES), dest)


def _trunk(x, pool_hist, kv_cache, p):
    b, n, d = x.shape
    n_heads = p["b_f"].shape[0]
    start_pos = 0 if kv_cache is None else kv_cache[0].shape[1]

    pool_args = (p["g_pool_norm"][0], p["w_pool"][0], p["pool_scale"][0], start_pos)
    ffn_args = (p["g_ffn_norm"][0], p["w_ffn_gu"][0], p["w_ffn_down"][0])
    if n >= 256:
        h, pool_state = _pool_layer(x, None if pool_hist is None else pool_hist[0], *pool_args,
                                    ffn=ffn_args)
        h = h.reshape(b * n, d)
    else:
        h, pool_state = _pool_layer(x, None if pool_hist is None else pool_hist[0], *pool_args)
        h = _dense_ffn(h.reshape(b * n, d), *ffn_args)

    qkv_args = (p["g_kv_norm"], p["g_attn_norm"][0], p["w_kvf"], p["b_f"], p["g_k"],
                p["w_q"][0], p["g_q"][0], n_heads)
    kvw = p["w_q"].shape[2]
    hd = kvw // n_heads
    if kv_cache is None:
        kt, vt, logf, k_aug, vtb, q_aug = _qkv_proj_t(h, b, *qkv_args)
        k = jnp.transpose(kt.reshape(b, n_heads, hd, n), (0, 3, 1, 2))
        v = jnp.transpose(vt.reshape(b, n_heads, hd, n), (0, 3, 1, 2))
        logf = logf.reshape(b, n, n_heads)
        o = _attn_prompt(q_aug, k_aug, vtb)
    else:
        k, v, logf, kb, vb, qb = _qkv_proj(h, *qkv_args)
        logf = logf.reshape(b, n, n_heads)
        qb, kb, vb = (a.reshape(b, n, kvw) for a in (qb, kb, vb))
        k = k.reshape(b, n, n_heads, hd)
        v = v.reshape(b, n, n_heads, hd)
        cache_k, cache_v, cache_logf = kv_cache
        past = cache_k.shape[1]
        lf_all = jnp.concatenate([cache_logf.astype(F32), logf], axis=1)
        total = past + n
        padded = -(-total // LANES) * LANES
        lf_t = jnp.transpose(lf_all, (0, 2, 1)).reshape(b * n_heads, total)
        c_t = _cumsum_lanes(jnp.pad(lf_t, ((0, 0), (0, padded - total))))
        c_t = c_t.reshape(b, n_heads, padded)
        c_new = c_t[:, :, past:total]
        cache_kt = jnp.transpose(cache_k, (0, 2, 3, 1)).reshape(b, kvw, past)
        cache_vt = jnp.transpose(cache_v, (0, 2, 3, 1)).reshape(b, kvw, past)
        heads = lambda a: jnp.transpose(a.reshape(b, n, n_heads, hd), (0, 2, 1, 3))
        o = _attn_sample(heads(qb), heads(kb), heads(vb), cache_kt, cache_vt, c_new[..., None],
                         c_t[:, :, None, :past], c_new[:, :, None, :])
        o = jnp.transpose(o, (0, 2, 1, 3))

    h2, u_tiles, idx, top_w = _oproj_router(h, o.reshape(b * n, kvw), p["w_o"][0],
                                            p["g_ffn_norm"][1], p["w_router"][0])
    return (h2, u_tiles, idx, top_w), pool_state[None], k, v, logf


def kernel(x_prompt, x_sample, state_pool, cache_k, cache_v, cache_logf, g_pool_norm, w_pool,
           pool_scale, g_kv_norm, w_kvf, b_f, g_k, g_attn_norm, w_q, g_q, w_o, g_ffn_norm,
           w_ffn_gu, w_ffn_down, w_router, w_moe_gu, w_moe_down):
    p = dict(g_pool_norm=g_pool_norm, w_pool=w_pool, pool_scale=pool_scale,
             g_kv_norm=g_kv_norm, w_kvf=w_kvf, b_f=b_f, g_k=g_k, g_attn_norm=g_attn_norm,
             w_q=w_q, g_q=g_q, w_o=w_o, g_ffn_norm=g_ffn_norm, w_ffn_gu=w_ffn_gu,
             w_ffn_down=w_ffn_down, w_router=w_router, w_moe_gu=w_moe_gu,
             w_moe_down=w_moe_down)
    assert w_pool.shape[0] == 1 and w_q.shape[0] == 1 and w_router.shape[0] == 1
    moe_p, pool_p, k_p, v_p, lf_p = _trunk(x_prompt, None, None, p)
    moe_s, pool_s, k_s, v_s, lf_s = _trunk(
        x_sample, state_pool, (cache_k, cache_v, cache_logf), p)
    y_p, y_s = _moe([moe_p, moe_s], w_moe_gu[0], w_moe_down[0])
    return (y_p.reshape(x_prompt.shape), y_s.reshape(x_sample.shape), pool_p, k_p, v_p, lf_p,
            pool_s, k_s, v_s, lf_s)
```

```python
import functools
import math

import jax
import jax.numpy as jnp
import numpy as np
from jax import lax
from jax.experimental import pallas as pl
from jax.experimental.pallas import tpu as pltpu

EPS = 1e-6
POOL_WINDOWS = (2, 4, 8, 16)
HALO = max(POOL_WINDOWS)
POOL_HIST = HALO - 1
TOP_K = 2
LANES = 128
BF16_ROWS = 16
MXU_DIM = 256
VMEM_LIMIT = 56 * 1024 * 1024
NEG_INF = float("-inf")

BF16 = jnp.bfloat16
F32 = jnp.float32


def _params(semantics, vmem=None):
    return pltpu.CompilerParams(dimension_semantics=semantics, vmem_limit_bytes=vmem)


def _dot(a, b):
    return jnp.dot(a, b, preferred_element_type=F32)


def _dot_nt(a, b):
    return lax.dot_general(a, b, (((1,), (1,)), ((), ())), preferred_element_type=F32)


def _rms_base(x):
    return x * lax.rsqrt(jnp.mean(x * x, axis=-1, keepdims=True) + EPS)


def _split2(x):
    hi = x.astype(BF16)
    lo = (x - hi.astype(F32)).astype(BF16)
    return hi, lo


def _split3(x):
    hi = x.astype(BF16)
    r = x - hi.astype(F32)
    mid = r.astype(BF16)
    lo = (r - mid.astype(F32)).astype(BF16)
    return hi, mid, lo


def _ffn_chunks(f):
    out, s = [], 0
    while s < f:
        n = min(4 * MXU_DIM, f - s)
        out.append((s, n))
        s += n
    return out


def _pool_kernel(x_ref, halo_ref, g_ref, w_ref, scale_ref, *rest, tm, start_pos, halo_is_normed,
                 f):
    if f is None:
        h_ref, st_ref = rest
    else:
        gf_ref, wgu_ref, wd_ref, h_ref, st_ref = rest
    i = pl.program_id(1)
    x = x_ref[...]
    g = g_ref[...]
    u = _rms_base(x) * g
    if halo_is_normed:
        uh = halo_ref[...]
    else:
        uh = _rms_base(halo_ref[...]) * g
        uh = jnp.where(i > 0, uh, 0.0)
    ext = jnp.concatenate([uh, u], axis=0)

    row = lax.broadcasted_iota(jnp.int32, (tm, 1), 0)
    avail = start_pos + i * tm + row + 1
    group = x.shape[1] // len(POOL_WINDOWS)
    ys = []
    for gi, w in enumerate(POOL_WINDOWS):
        lo, hi = gi * group, (gi + 1) * group
        s = ext[:, lo:hi]
        step = 1
        while step < w:
            s = s + pltpu.roll(s, step, 0)
            step *= 2
        inv_cnt = 1.0 / jnp.minimum(avail, w).astype(F32)
        pooled = s[HALO:, :] * inv_cnt
        diff = pooled - u[:, lo:hi]
        ys.append(_dot(diff.astype(BF16), w_ref[gi]))
    h = x + jnp.concatenate(ys, axis=1) * scale_ref[...]
    if f is not None:
        h = h + _swiglu((_rms_base(h) * gf_ref[...]).astype(BF16), wgu_ref, wd_ref, f)
    h_ref[...] = h

    @pl.when(i == pl.num_programs(1) - 1)
    def _():
        st_ref[...] = u[tm - HALO:, :]


def _pool_layer(x, hist, g, w_pool, pool_scale, start_pos, ffn=None):
    b, n, d = x.shape
    tm = min(n, 256 if ffn is None else 512)
    assert n % tm == 0 and tm % HALO == 0 and n >= HALO
    nt = n // tm
    if hist is None:
        halo_arr = x
        halo_spec = pl.BlockSpec(
            (None, HALO, d), lambda bi, i: (bi, jnp.maximum(i * (tm // HALO) - 1, 0), 0))
    else:
        assert nt == 1
        halo_arr = jnp.pad(hist, ((0, 0), (HALO - POOL_HIST, 0), (0, 0)))
        halo_spec = pl.BlockSpec((None, HALO, d), lambda bi, i: (bi, 0, 0))
    ng = len(POOL_WINDOWS)
    vec = pl.BlockSpec((1, d), lambda bi, i: (0, 0))
    in_specs = [pl.BlockSpec((None, tm, d), lambda bi, i: (bi, i, 0)), halo_spec, vec,
                pl.BlockSpec((ng, d // ng, d // ng), lambda bi, i: (0, 0, 0)), vec]
    args = [x, halo_arr, g.reshape(1, d), w_pool.astype(BF16), pool_scale.reshape(1, d)]
    f = None
    if ffn is not None:
        g_ffn, w_gu, w_down = ffn
        f = w_down.shape[0]
        in_specs += [vec, pl.BlockSpec((d, 2 * f), lambda bi, i: (0, 0)),
                     pl.BlockSpec((f, d), lambda bi, i: (0, 0))]
        args += [g_ffn.reshape(1, d), w_gu.astype(BF16), w_down.astype(BF16)]
    h, st = pl.pallas_call(
        functools.partial(_pool_kernel, tm=tm, start_pos=start_pos,
                          halo_is_normed=hist is not None, f=f),
        grid=(b, nt),
        in_specs=in_specs,
        out_specs=[
            pl.BlockSpec((None, tm, d), lambda bi, i: (bi, i, 0)),
            pl.BlockSpec((None, HALO, d), lambda bi, i: (bi, 0, 0)),
        ],
        out_shape=[jax.ShapeDtypeStruct((b, n, d), F32),
                   jax.ShapeDtypeStruct((b, HALO, d), F32)],
        compiler_params=_params(("parallel", "arbitrary"), None if ffn is None else VMEM_LIMIT),
        name="pool_mixer" if ffn is None else "pool_ffn",
    )(*args)
    return h, st[:, HALO - POOL_HIST:, :]


def _swiglu(xb, wgu_ref, wd_ref, f):
    acc = None
    for s, n in _ffn_chunks(f):
        gate = _dot(xb, wgu_ref[:, s:s + n])
        up = _dot(xb, wgu_ref[:, f + s:f + s + n])
        act = (gate * jax.nn.sigmoid(gate)) * up
        part = _dot(act.astype(BF16), wd_ref[s:s + n, :])
        acc = part if acc is None else acc + part
    return acc


def _dense_ffn_kernel(h_ref, g_ref, wgu_ref, wd_ref, o_ref, *, f):
    h = h_ref[...]
    u = (_rms_base(h) * g_ref[...]).astype(BF16)
    o_ref[...] = h + _swiglu(u, wgu_ref, wd_ref, f)


def _dense_ffn(h, g, w_gu, w_down):
    t, d = h.shape
    f = w_down.shape[0]
    tm = min(t, 512)
    assert t % tm == 0
    return pl.pallas_call(
        functools.partial(_dense_ffn_kernel, f=f),
        grid=(t // tm,),
        in_specs=[
            pl.BlockSpec((tm, d), lambda i: (i, 0)),
            pl.BlockSpec((1, d), lambda i: (0, 0)),
            pl.BlockSpec((d, 2 * f), lambda i: (0, 0)),
            pl.BlockSpec((f, d), lambda i: (0, 0)),
        ],
        out_specs=pl.BlockSpec((tm, d), lambda i: (i, 0)),
        out_shape=jax.ShapeDtypeStruct((t, d), F32),
        compiler_params=_params(("parallel",), VMEM_LIMIT),
        name="dense_ffn",
    )(h, g.reshape(1, d), w_gu.astype(BF16), w_down.astype(BF16))


def _head_norm(x, bd_ref, g):
    sq_hi, sq_lo = _split2(x * x)
    n = bd_ref.shape[0]
    ms = jnp.concatenate(
        [_dot(sq_hi[:, c:c + n], bd_ref[...]) + _dot(sq_lo[:, c:c + n], bd_ref[...])
         for c in range(0, x.shape[1], n)], axis=1)
    return x * lax.rsqrt(ms + EPS) * g


def _log_sigmoid(z):
    return jnp.minimum(z, 0.0) - jnp.log1p(jnp.exp(-jnp.abs(z)))


def _qkv_kernel(h_ref, gkv_ref, gat_ref, wk_ref, wv_ref, wf_ref, wq_ref, bf_ref,
                gk_ref, gq_ref, bd_ref,
                k_ref, v_ref, lf_ref, kb_ref, vb_ref, qb_ref, *, n_heads, q_scale):
    base = _rms_base(h_ref[...])
    a_kv = (base * gkv_ref[...]).astype(BF16)
    a_q = (base * gat_ref[...]).astype(BF16)
    k = _head_norm(_dot(a_kv, wk_ref[...]), bd_ref, gk_ref[...])
    v = _dot(a_kv, wv_ref[...])
    z = _dot(a_kv, wf_ref[...])[:, :n_heads] + bf_ref[...]
    q = _head_norm(_dot(a_q, wq_ref[...]), bd_ref, gq_ref[...])
    k_ref[...] = k
    v_ref[...] = v
    lf_ref[...] = _log_sigmoid(z)
    kb_ref[...] = k.astype(BF16)
    vb_ref[...] = v.astype(BF16)
    qb_ref[...] = (q * q_scale).astype(BF16)


def _qkv_proj(h, g_kv, g_attn, w_kvf, b_f, g_k, w_q, g_q, n_heads):
    t, d = h.shape
    kvw = w_q.shape[1]
    hd = kvw // n_heads
    tm = min(t, 256)
    assert t % tm == 0 and MXU_DIM % hd == 0 and kvw % MXU_DIM == 0
    wk = w_kvf[:, :kvw].astype(BF16)
    wv = w_kvf[:, kvw:2 * kvw].astype(BF16)
    wf = jnp.pad(w_kvf[:, 2 * kvw:], ((0, 0), (0, LANES - n_heads))).astype(BF16)
    head = jnp.arange(MXU_DIM) // hd
    bd = ((head[:, None] == head[None, :]).astype(F32) / hd).astype(BF16)
    row = lambda i: (i, 0)
    fix = lambda i: (0, 0)
    return pl.pallas_call(
        functools.partial(_qkv_kernel, n_heads=n_heads, q_scale=hd ** -0.5),
        grid=(t // tm,),
        in_specs=[
            pl.BlockSpec((tm, d), row),
            pl.BlockSpec((1, d), fix), pl.BlockSpec((1, d), fix),
            pl.BlockSpec((d, kvw), fix), pl.BlockSpec((d, kvw), fix),
            pl.BlockSpec((d, LANES), fix), pl.BlockSpec((d, kvw), fix),
            pl.BlockSpec((1, n_heads), fix),
            pl.BlockSpec((1, kvw), fix), pl.BlockSpec((1, kvw), fix),
            pl.BlockSpec((MXU_DIM, MXU_DIM), fix),
        ],
        out_specs=[
            pl.BlockSpec((tm, kvw), row), pl.BlockSpec((tm, kvw), row),
            pl.BlockSpec((tm, n_heads), row),
            pl.BlockSpec((tm, kvw), row), pl.BlockSpec((tm, kvw), row),
            pl.BlockSpec((tm, kvw), row),
        ],
        out_shape=[
            jax.ShapeDtypeStruct((t, kvw), F32), jax.ShapeDtypeStruct((t, kvw), F32),
            jax.ShapeDtypeStruct((t, n_heads), F32),
            jax.ShapeDtypeStruct((t, kvw), BF16), jax.ShapeDtypeStruct((t, kvw), BF16),
            jax.ShapeDtypeStruct((t, kvw), BF16),
        ],
        compiler_params=_params(("parallel",), VMEM_LIMIT),
        name="qkv_proj",
    )(h, g_kv.reshape(1, d), g_attn.reshape(1, d), wk, wv, wf, w_q.astype(BF16),
      b_f.reshape(1, n_heads), jnp.tile(g_k, n_heads).reshape(1, kvw),
      jnp.tile(g_q, n_heads).reshape(1, kvw), bd)


def _cumsum_tile(x, tri, carry):
    y = carry
    for part in _split3(x):
        y = y + _dot(part, tri)
    return y


def _bias_placement(n_heads, hd, c_first):
    p = np.zeros((4 * n_heads, n_heads * LANES), np.float32)
    for h in range(n_heads):
        first = h * LANES + (h * hd % LANES + hd) % LANES
        for i in range(3):
            term, one = (first + i, first + 3 + i) if c_first else (first + 3 + i, first + i)
            p[i * n_heads + h, term] = 1.0 if c_first else -1.0
            p[3 * n_heads, one] = 1.0
    return jnp.asarray(p, BF16)


def _augment(x, c_terms, place_ref, n_heads, hd):
    ext = _dot(c_terms, place_ref[...]).astype(BF16)
    placed = jnp.concatenate(
        [x[:, h * hd // LANES * LANES:h * hd // LANES * LANES + LANES] for h in range(n_heads)],
        axis=1)
    lane = lax.broadcasted_iota(jnp.int32, (1, n_heads * LANES), 1)
    head, within = lane // LANES, lane % LANES
    own = within // hd == head % (LANES // hd)
    return jnp.where(own, placed, ext)


def _qkv_t_kernel(h_ref, gkv_ref, gat_ref, wkt_ref, wvt_ref, wf_ref, wq_ref, bf_ref,
                  gk_ref, gq_ref, bd_ref, tri_ref, pk_ref, pq_ref,
                  kt_ref, vt_ref, lf_ref, kaug_ref, vtb_ref, qaug_ref, carry_ref, *,
                  n_heads, q_scale):
    @pl.when(pl.program_id(1) == 0)
    def _():
        carry_ref[...] = jnp.zeros(carry_ref.shape, F32)

    base = _rms_base(h_ref[...])
    a_kv = (base * gkv_ref[...]).astype(BF16)
    a_q = (base * gat_ref[...]).astype(BF16)
    tm = a_kv.shape[0]
    kraw = _dot_nt(wkt_ref[...], a_kv)
    hd = kraw.shape[0] // n_heads
    k3 = kraw.reshape(n_heads, hd, tm)
    ms = jnp.mean(k3 * k3, axis=1, keepdims=True)
    kt = (k3 * lax.rsqrt(ms + EPS) * gk_ref[...][None]).reshape(kraw.shape)
    vt = _dot_nt(wvt_ref[...], a_kv)
    lf = _log_sigmoid(_dot(a_kv, wf_ref[...])[:, :n_heads] + bf_ref[...])
    sums = _dot(tri_ref[...], jnp.concatenate(_split3(lf), axis=1))
    c = carry_ref[...] + sum(sums[:, i * n_heads:(i + 1) * n_heads] for i in range(3))
    carry_ref[...] = c[tm - 1:tm, :]
    q = _head_norm(_dot(a_q, wq_ref[...]), bd_ref, gq_ref[...])
    kt_ref[...] = kt
    vt_ref[...] = vt
    lf_ref[...] = lf
    vtb_ref[...] = vt.astype(BF16)
    c_terms = jnp.concatenate(
        _split3(c * math.log2(math.e)) + (jnp.ones((tm, n_heads), BF16),), axis=1)
    k_aug = _augment(kt.T.astype(BF16), c_terms, pk_ref, n_heads, hd)
    q_aug = _augment((q * q_scale).astype(BF16), c_terms, pq_ref, n_heads, hd)
    for h in range(n_heads):
        kaug_ref[h] = k_aug[:, h * LANES:(h + 1) * LANES]
        qaug_ref[h] = q_aug[:, h * LANES:(h + 1) * LANES]


def _qkv_proj_t(h, b, g_kv, g_attn, w_kvf, b_f, g_k, w_q, g_q, n_heads):
    t, d = h.shape
    s = t // b
    kvw = w_q.shape[1]
    hd = kvw // n_heads
    tm = min(s, 512)
    nt = s // tm
    assert s % tm == 0 and MXU_DIM % hd == 0 and kvw % MXU_DIM == 0
    assert LANES % hd == 0 and hd + 6 <= LANES
    wkt = w_kvf[:, :kvw].T.astype(BF16)
    wvt = w_kvf[:, kvw:2 * kvw].T.astype(BF16)
    wf = jnp.pad(w_kvf[:, 2 * kvw:], ((0, 0), (0, LANES - n_heads))).astype(BF16)
    head = jnp.arange(MXU_DIM) // hd
    bd = ((head[:, None] == head[None, :]).astype(F32) / hd).astype(BF16)
    pos = jnp.arange(tm)
    tri = (pos[:, None] >= pos[None, :]).astype(BF16)
    fix = lambda bi, i: (0, 0)
    feat = pl.BlockSpec((None, kvw, tm), lambda bi, i: (bi, 0, i))
    aug = pl.BlockSpec((None, n_heads, tm, LANES), lambda bi, i: (bi, 0, i, 0))
    return pl.pallas_call(
        functools.partial(_qkv_t_kernel, n_heads=n_heads,
                          q_scale=hd ** -0.5 * math.log2(math.e)),
        grid=(b, nt),
        in_specs=[
            pl.BlockSpec((tm, d), lambda bi, i: (bi * nt + i, 0)),
            pl.BlockSpec((1, d), fix), pl.BlockSpec((1, d), fix),
            pl.BlockSpec((kvw, d), fix), pl.BlockSpec((kvw, d), fix),
            pl.BlockSpec((d, LANES), fix), pl.BlockSpec((d, kvw), fix),
            pl.BlockSpec((1, n_heads), fix), pl.BlockSpec((hd, tm), fix),
            pl.BlockSpec((1, kvw), fix), pl.BlockSpec((MXU_DIM, MXU_DIM), fix),
            pl.BlockSpec((tm, tm), fix),
            pl.BlockSpec((4 * n_heads, n_heads * LANES), fix),
            pl.BlockSpec((4 * n_heads, n_heads * LANES), fix),
        ],
        out_specs=[feat, feat, pl.BlockSpec((tm, n_heads), lambda bi, i: (bi * nt + i, 0)),
                   aug, feat, aug],
        out_shape=[jax.ShapeDtypeStruct((b, kvw, s), F32), jax.ShapeDtypeStruct((b, kvw, s), F32),
                   jax.ShapeDtypeStruct((t, n_heads), F32),
                   jax.ShapeDtypeStruct((b, n_heads, s, LANES), BF16),
                   jax.ShapeDtypeStruct((b, kvw, s), BF16),
                   jax.ShapeDtypeStruct((b, n_heads, s, LANES), BF16)],
        scratch_shapes=[pltpu.VMEM((1, n_heads), F32)],
        compiler_params=_params(("parallel", "arbitrary"), VMEM_LIMIT),
        name="qkv_proj_t",
    )(h, g_kv.reshape(1, d), g_attn.reshape(1, d), wkt, wvt, wf, w_q.astype(BF16),
      b_f.reshape(1, n_heads), jnp.broadcast_to(g_k[:, None], (hd, tm)),
      jnp.tile(g_q, n_heads).reshape(1, kvw), bd, tri,
      _bias_placement(n_heads, hd, False), _bias_placement(n_heads, hd, True))


def _cumsum_kernel(x_ref, o_ref):
    r, l = x_ref.shape
    ii = lax.broadcasted_iota(jnp.int32, (LANES, LANES), 0)
    jj = lax.broadcasted_iota(jnp.int32, (LANES, LANES), 1)
    tri = (ii <= jj).astype(BF16)
    carry = jnp.zeros((r, 1), F32)
    for c in range(0, l, LANES):
        y = _cumsum_tile(x_ref[:, c:c + LANES], tri, carry)
        o_ref[:, c:c + LANES] = y
        carry = y[:, LANES - 1:LANES]


def _cumsum_lanes(x):
    r, l = x.shape
    tr = min(r, 256)
    assert r % tr == 0 and l % LANES == 0
    return pl.pallas_call(
        _cumsum_kernel,
        grid=(r // tr,),
        in_specs=[pl.BlockSpec((tr, l), lambda i: (i, 0))],
        out_specs=pl.BlockSpec((tr, l), lambda i: (i, 0)),
        out_shape=jax.ShapeDtypeStruct((r, l), F32),
        compiler_params=_params(("parallel",)),
        name="logf_cumsum",
    )(x)


def _bdot(a, b, ca, cb):
    return lax.dot_general(a, b, (((ca,), (cb,)), ((0,), (0,))), preferred_element_type=F32)


def _attn_prompt_kernel(q_ref, kaug_ref, vt_ref, o_ref, m_ref, acc_ref, *, tq):
    qi = pl.program_id(1)
    n_heads = q_ref.shape[0]
    kvw = vt_ref.shape[0]
    hd = kvw // n_heads
    q_aug = q_ref[...]
    m_ref[...] = jnp.full(m_ref.shape, NEG_INF, F32)
    acc_ref[...] = jnp.zeros(acc_ref.shape, F32)
    n_den = acc_ref.shape[1] - hd
    den_rows = (lax.broadcasted_iota(jnp.int32, (n_heads, n_den, tq), 1) == 0).astype(BF16)

    def tile(kt, masked):
        start = pl.multiple_of(kt * tq, tq)
        st = _bdot(kaug_ref[:, pl.ds(start, tq), :], q_aug, 2, 2)
        if masked:
            key = lax.broadcasted_iota(jnp.int32, (1, tq, tq), 1)
            qry = lax.broadcasted_iota(jnp.int32, (1, tq, tq), 2)
            st = jnp.where(key <= qry, st, NEG_INF)
        m_prev = m_ref[...]
        m_new = jnp.maximum(m_prev, jnp.max(st, axis=1, keepdims=True))
        alpha = jnp.exp2(m_prev - m_new)
        p = jnp.exp2(st - m_new)
        vt = vt_ref[:, pl.ds(start, tq)].reshape(n_heads, hd, tq)
        vt = jnp.concatenate([vt, den_rows], axis=1)
        acc_ref[...] = alpha * acc_ref[...] + _bdot(vt, p.astype(BF16), 2, 1)
        m_ref[...] = m_new

    def body(kt, carry):
        tile(kt, False)
        return carry

    lax.fori_loop(0, qi, body, 0)
    tile(qi, True)
    ot = (acc_ref[:, :hd, :] / acc_ref[:, hd:hd + 1, :]).reshape(kvw, tq)
    o_ref[...] = ot.T.astype(o_ref.dtype)


def _attn_prompt(q_aug, k_aug, vt):
    b, n_heads, s, _ = q_aug.shape
    kvw = vt.shape[1]
    hd = kvw // n_heads
    tq = min(s, 256)
    assert s % tq == 0
    return pl.pallas_call(
        functools.partial(_attn_prompt_kernel, tq=tq),
        grid=(b, s // tq),
        in_specs=[
            pl.BlockSpec((None, n_heads, tq, LANES), lambda bi, qi: (bi, 0, qi, 0)),
            pl.BlockSpec((None, n_heads, s, LANES), lambda bi, qi: (bi, 0, 0, 0)),
            pl.BlockSpec((None, kvw, s), lambda bi, qi: (bi, 0, 0)),
        ],
        out_specs=pl.BlockSpec((None, tq, kvw), lambda bi, qi: (bi, qi, 0)),
        out_shape=jax.ShapeDtypeStruct((b, s, kvw), BF16),
        scratch_shapes=[pltpu.VMEM((n_heads, 1, tq), F32),
                        pltpu.VMEM((n_heads, hd + BF16_ROWS, tq), F32)],
        compiler_params=_params(("parallel", "arbitrary"), VMEM_LIMIT),
        name="fox_prompt",
    )(q_aug, k_aug, vt)


def _attn_sample_kernel(q_ref, cq_ref, kc_ref, vc_ref, ckc_ref, kn_ref, vn_ref, ckn_ref,
                        o_ref, m_ref, acc_ref, *, n_heads):
    kt = pl.program_id(1)
    last = pl.num_programs(1) - 1
    q = q_ref[...]
    cq = cq_ref[...]
    n, hd = q.shape[1:]
    n_den = acc_ref.shape[2] - hd

    @pl.when(kt == 0)
    def _():
        m_ref[...] = jnp.full(m_ref.shape, NEG_INF, F32)
        acc_ref[...] = jnp.zeros(acc_ref.shape, F32)

    def update(s, v, v_contract):
        m_prev = m_ref[...]
        m_new = jnp.maximum(m_prev, jnp.max(s, axis=2, keepdims=True))
        p = jnp.exp(s - m_new).astype(BF16)
        acc_ref[...] = jnp.exp(m_prev - m_new) * acc_ref[...] + _bdot(p, v, 2, v_contract)
        m_ref[...] = m_new

    kvw, tk = kc_ref.shape
    split = (n_heads, kvw // n_heads, tk)
    kk = kc_ref[...].reshape(split).astype(BF16)
    den = (lax.broadcasted_iota(jnp.int32, (n_heads, n_den, tk), 1) == 0).astype(BF16)
    vv = jnp.concatenate([vc_ref[...].reshape(split).astype(BF16), den], axis=1)
    update(_bdot(q, kk, 2, 1) + (cq - ckc_ref[...]), vv, 2)

    @pl.when(kt == last)
    def _():
        r = lax.broadcasted_iota(jnp.int32, (1, n, n), 1)
        c = lax.broadcasted_iota(jnp.int32, (1, n, n), 2)
        s = _bdot(q, kn_ref[...], 2, 2) + (cq - ckn_ref[...])
        den_new = (lax.broadcasted_iota(jnp.int32, (n_heads, n, n_den), 2) == 0).astype(BF16)
        update(jnp.where(c <= r, s, NEG_INF), jnp.concatenate([vn_ref[...], den_new], axis=2), 1)
        acc = acc_ref[...]
        o_ref[...] = (acc[:, :, :hd] / acc[:, :, hd:hd + 1]).astype(o_ref.dtype)


def _attn_sample(q, k_new, v_new, cache_kt, cache_vt, cq, ck_cache, ck_new):
    b, n_heads, n, hd = q.shape
    kvw, p = cache_kt.shape[1:]
    tk = min(p, 1024)
    assert p % tk == 0 and kvw == n_heads * hd and ck_cache.shape[-1] >= p
    fix = lambda bi, kt: (bi, 0, 0, 0)
    rows = pl.BlockSpec((None, n_heads, n, hd), fix)
    cache = pl.BlockSpec((None, kvw, tk), lambda bi, kt: (bi, 0, kt))
    return pl.pallas_call(
        functools.partial(_attn_sample_kernel, n_heads=n_heads),
        grid=(b, p // tk),
        in_specs=[
            rows,
            pl.BlockSpec((None, n_heads, n, 1), fix),
            cache, cache,
            pl.BlockSpec((None, n_heads, 1, tk), lambda bi, kt: (bi, 0, 0, kt)),
            rows, rows,
            pl.BlockSpec((None, n_heads, 1, n), fix),
        ],
        out_specs=rows,
        out_shape=jax.ShapeDtypeStruct((b, n_heads, n, hd), BF16),
        scratch_shapes=[pltpu.VMEM((n_heads, n, 1), F32),
                        pltpu.VMEM((n_heads, n, hd + BF16_ROWS), F32)],
        compiler_params=_params(("parallel", "arbitrary"), VMEM_LIMIT),
        name="fox_sample",
    )(q, cq, cache_kt, cache_vt, ck_cache, k_new, v_new, ck_new)


def _store_token_tiles(ref, x):
    m, d = x.shape
    nsub = d // LANES
    for s in range(nsub):
        ref[pl.ds(s, m, stride=nsub), :] = x[:, s * LANES:(s + 1) * LANES]


def _load_token_tiles(ref, m, nsub, first=0, stride=None):
    stride = nsub if stride is None else stride
    return jnp.concatenate(
        [ref[pl.ds(first + s, m, stride=stride), :] for s in range(nsub)], axis=1)


def _oproj_router_kernel(h_ref, o_ref, wo_ref, g_ref, wr_ref, h2_ref, u_ref, idx_ref, w_ref, *,
                         n_experts):
    h2 = h_ref[...] + _dot(o_ref[...], wo_ref[...])
    u = _rms_base(h2) * g_ref[...]
    h2_ref[...] = h2
    _store_token_tiles(u_ref, u)
    u_hi, u_lo = _split2(u)
    both = _dot(u_hi, wr_ref[...])
    logits = both[:, :LANES] + both[:, LANES:] + _dot(u_lo, wr_ref[:, :LANES])
    lane = lax.broadcasted_iota(jnp.int32, logits.shape, 1).astype(F32)
    logits = jnp.where(lane < n_experts, logits, NEG_INF)
    m1 = jnp.max(logits, axis=1, keepdims=True)
    i1 = jnp.min(jnp.where(logits == m1, lane, float(LANES)), axis=1, keepdims=True)
    rest = jnp.where(lane == i1, NEG_INF, logits)
    m2 = jnp.max(rest, axis=1, keepdims=True)
    i2 = jnp.min(jnp.where(rest == m2, lane, float(LANES)), axis=1, keepdims=True)
    e2 = jnp.exp(m2 - m1)
    w1 = 1.0 / (1.0 + e2)
    w2 = e2 / (1.0 + e2)
    idx_ref[...] = jnp.concatenate([i1, i2], axis=1).astype(jnp.int32)
    w_ref[...] = jnp.concatenate([w1, w2], axis=1)


def _oproj_router(h, o, w_o, g, w_router):
    t, d = h.shape
    kvw = o.shape[1]
    ne = w_router.shape[1]
    tm = min(t, 512)
    nsub = d // LANES
    assert t % tm == 0 and ne <= LANES
    wr = jnp.pad(w_router, ((0, 0), (0, LANES - ne)))
    wr_hi = wr.astype(BF16)
    wr_lo = (wr - wr_hi.astype(F32)).astype(BF16)
    row = lambda i: (i, 0)
    return pl.pallas_call(
        functools.partial(_oproj_router_kernel, n_experts=ne),
        grid=(t // tm,),
        in_specs=[
            pl.BlockSpec((tm, d), row), pl.BlockSpec((tm, kvw), row),
            pl.BlockSpec((kvw, d), lambda i: (0, 0)),
            pl.BlockSpec((1, d), lambda i: (0, 0)),
            pl.BlockSpec((d, 2 * LANES), lambda i: (0, 0)),
        ],
        out_specs=[pl.BlockSpec((tm, d), row), pl.BlockSpec((tm * nsub, LANES), row),
                   pl.BlockSpec((tm, TOP_K), row), pl.BlockSpec((tm, TOP_K), row)],
        out_shape=[jax.ShapeDtypeStruct((t, d), F32),
                   jax.ShapeDtypeStruct((t * nsub, LANES), F32),
                   jax.ShapeDtypeStruct((t, TOP_K), jnp.int32),
                   jax.ShapeDtypeStruct((t, TOP_K), F32)],
        compiler_params=_params(("parallel",)),
        name="oproj_router",
    )(h, o, w_o.astype(BF16), g.reshape(1, d), jnp.concatenate([wr_hi, wr_lo], axis=1))


MOE_ROWS = 512
DMA_UNROLL = 16


def _route_plan(idx, ne, tmg):
    t = idx.shape[0]
    npairs = t * TOP_K
    e_flat = idx.reshape(npairs)
    onehot = (e_flat[:, None] == jnp.arange(ne, dtype=jnp.int32)[None, :]).astype(jnp.int32)
    csum = jnp.cumsum(onehot, axis=0)
    cnt = csum[-1]
    padded = (cnt + tmg - 1) // tmg * tmg
    ends = jnp.cumsum(padded)
    off = ends - padded
    dest = jnp.sum(onehot * (off[None, :] + csum), axis=1) - 1
    total = ends[-1]
    k = jnp.arange(tmg, dtype=jnp.int32)[None, :]
    valid = (k < (padded - cnt)[:, None]).reshape(-1)
    pad_row = ((off + cnt)[:, None] + k).reshape(-1)
    tail_row = total + jnp.cumsum(jnp.logical_not(valid).astype(jnp.int32)) - 1
    fill = jnp.where(valid, pad_row, tail_row)
    n_tiles = total // tmg
    nt_max = (npairs + ne * tmg) // tmg
    tile = jnp.minimum(jnp.arange(nt_max, dtype=jnp.int32), n_tiles - 1)
    tile_expert = jnp.sum((tile[:, None] >= (ends // tmg)[None, :]).astype(jnp.int32), axis=1)
    return dest, fill, tile_expert, n_tiles.reshape(1)


def _step_ranges(steps):
    out, first = [], 0
    for n in steps:
        out.append((first, n))
        first += n
    return out


def _in_range(i, first, count):
    return jnp.logical_and(i >= first, i < first + count)


def _dispatch_kernel(dest_ref, fill_ref, *rest, ch, ranges, nsub):
    u_refs, (xg_hbm, zero_ref, sem) = rest[:len(ranges)], rest[len(ranges):]
    i = pl.program_id(0)

    def issue(src_row, dst_rows):
        def body(j, carry):
            for jj in range(DMA_UNROLL):
                pltpu.make_async_copy(src_row(j, jj), xg_hbm.at[dst_rows[0, j * DMA_UNROLL + jj]],
                                      sem.at[0]).start(priority=jj % 2)
            return carry
        lax.fori_loop(0, ch // DMA_UNROLL, body, 0)

    for u_ref, (first, count) in zip(u_refs, ranges):
        def token_tile(j, jj, u_ref=u_ref):
            tok = j * (DMA_UNROLL // TOP_K) + jj // TOP_K
            return u_ref.at[pl.ds(pl.multiple_of(tok * nsub, nsub), nsub), :]

        @pl.when(_in_range(i, first, count))
        def _(token_tile=token_tile):
            issue(token_tile, dest_ref)

    pair_steps = ranges[-1][0] + ranges[-1][1]

    @pl.when(i >= pair_steps)
    def _():
        zero_ref[...] = jnp.zeros(zero_ref.shape, zero_ref.dtype)
        issue(lambda j, jj: zero_ref, fill_ref)

    pltpu.make_async_copy(xg_hbm.at[pl.ds(0, ch)], xg_hbm.at[pl.ds(0, ch)], sem.at[0]).wait()


def _dispatch(u_groups, dest, fill, n_rows, nsub):
    pairs = [u.shape[0] // nsub * TOP_K for u in u_groups]
    assert sum(pairs) == dest.shape[0] and DMA_UNROLL % TOP_K == 0
    ch = math.gcd(2 * MOE_ROWS, *pairs)
    assert fill.shape[0] % ch == 0 and ch % DMA_UNROLL == 0
    ranges = _step_ranges([p // ch for p in pairs])
    pair_steps = sum(p // ch for p in pairs)
    fill_steps = fill.shape[0] // ch
    smem = pltpu.SMEM
    clamp = lambda i, first, count: jnp.clip(i - first, 0, count - 1)
    return pl.pallas_call(
        functools.partial(_dispatch_kernel, ch=ch, ranges=ranges, nsub=nsub),
        grid=(pair_steps + fill_steps,),
        in_specs=[
            pl.BlockSpec((None, 1, ch), lambda i: (jnp.minimum(i, pair_steps - 1), 0, 0),
                         memory_space=smem),
            pl.BlockSpec((None, 1, ch), lambda i: (jnp.maximum(i - pair_steps, 0), 0, 0),
                         memory_space=smem),
        ] + [pl.BlockSpec((ch // TOP_K * nsub, LANES),
                          lambda i, first=first, count=count: (clamp(i, first, count), 0))
             for first, count in ranges],
        out_specs=pl.BlockSpec(memory_space=pl.ANY),
        out_shape=jax.ShapeDtypeStruct((n_rows, nsub, LANES), F32),
        scratch_shapes=[pltpu.VMEM((nsub, LANES), F32), pltpu.SemaphoreType.DMA((1,))],
        compiler_params=_params(("arbitrary",)),
        name="moe_dispatch",
    )(dest.reshape(pair_steps, 1, ch), fill.reshape(fill_steps, 1, ch), *u_groups)


def _moe_ffn_kernel(te_ref, nt_ref, x_ref, wgu_ref, wd_ref, o_ref, *, f, tmg, nsub):
    del te_ref
    live = pl.program_id(0) < nt_ref[0]

    @pl.when(live)
    def _():
        x = _load_token_tiles(x_ref, tmg, nsub).astype(BF16)
        _store_token_tiles(o_ref, _swiglu(x, wgu_ref, wd_ref, f))

    @pl.when(jnp.logical_not(live))
    def _():
        o_ref[...] = jnp.zeros(o_ref.shape, o_ref.dtype)


def _moe_ffn(xg, tile_expert, n_tiles, w_gu, w_down, tmg):
    ne, f, d = w_down.shape
    nsub = d // LANES
    nt_max = tile_expert.shape[0]
    assert xg.shape[0] == nt_max * tmg * nsub
    rows = lambda i, te, nt: (jnp.minimum(i, nt[0] - 1), 0)
    return pl.pallas_call(
        functools.partial(_moe_ffn_kernel, f=f, tmg=tmg, nsub=nsub),
        grid_spec=pltpu.PrefetchScalarGridSpec(
            num_scalar_prefetch=2,
            grid=(nt_max,),
            in_specs=[
                pl.BlockSpec((tmg * nsub, LANES), rows),
                pl.BlockSpec((None, d, 2 * f), lambda i, te, nt: (te[i], 0, 0)),
                pl.BlockSpec((None, f, d), lambda i, te, nt: (te[i], 0, 0)),
            ],
            out_specs=pl.BlockSpec((tmg * nsub, LANES), lambda i, te, nt: (i, 0)),
        ),
        out_shape=jax.ShapeDtypeStruct(xg.shape, F32),
        compiler_params=_params(("arbitrary",), VMEM_LIMIT),
        name="moe_experts",
    )(tile_expert, n_tiles, xg, w_gu.astype(BF16), w_down.astype(BF16))


def _combine_kernel(dcur_ref, dnxt_ref, w_ref, og_hbm, *rest, tm, nsub, ranges):
    ng = len(ranges)
    h_refs, y_refs, (buf, sem) = rest[:ng], rest[ng:2 * ng], rest[2 * ng:]
    i = pl.program_id(0)
    n = pl.num_programs(0)
    rows = TOP_K * tm

    def issue(d_ref, slot):
        def body(j, carry):
            for jj in range(DMA_UNROLL):
                tok = j * (DMA_UNROLL // TOP_K) + jj // TOP_K
                dst = buf.at[slot, jj % TOP_K, pl.ds(pl.multiple_of(tok * nsub, nsub), nsub), :]
                pltpu.make_async_copy(og_hbm.at[d_ref[0, j * DMA_UNROLL + jj]], dst,
                                      sem.at[slot]).start(priority=jj % 2)
            return carry
        lax.fori_loop(0, rows // DMA_UNROLL, body, 0)

    @pl.when(i == 0)
    def _():
        issue(dcur_ref, 0)

    @pl.when(i + 1 < n)
    def _():
        issue(dnxt_ref, lax.rem(i + 1, 2))

    slot = lax.rem(i, 2)
    pltpu.make_async_copy(buf.at[slot], buf.at[slot], sem.at[slot]).wait()
    w = w_ref[...]
    moe = (w[:, 0:1] * _load_token_tiles(buf.at[slot, 0], tm, nsub)
           + w[:, 1:2] * _load_token_tiles(buf.at[slot, 1], tm, nsub))
    for h_ref, y_ref, (first, count) in zip(h_refs, y_refs, ranges):
        @pl.when(_in_range(i, first, count))
        def _(h_ref=h_ref, y_ref=y_ref):
            y_ref[...] = h_ref[...] + moe


def _combine(h_groups, w, og, dest):
    assert TOP_K == 2
    d = h_groups[0].shape[1]
    nsub = d // LANES
    tm = math.gcd(256, *[h.shape[0] for h in h_groups])
    assert (TOP_K * tm) % DMA_UNROLL == 0
    ranges = _step_ranges([h.shape[0] // tm for h in h_groups])
    n = sum(count for _, count in ranges)
    dest3 = dest.reshape(n, 1, TOP_K * tm)
    smem = pltpu.SMEM
    clamp = lambda i, first, count: jnp.clip(i - first, 0, count - 1)
    rows = [pl.BlockSpec((tm, d), lambda i, first=first, count=count: (clamp(i, first, count), 0))
            for first, count in ranges]
    return pl.pallas_call(
        functools.partial(_combine_kernel, tm=tm, nsub=nsub, ranges=ranges),
        grid=(n,),
        in_specs=[
            pl.BlockSpec((None, 1, TOP_K * tm), lambda i: (i, 0, 0), memory_space=smem),
            pl.BlockSpec((None, 1, TOP_K * tm), lambda i: (jnp.minimum(i + 1, n - 1), 0, 0),
                         memory_space=smem),
            pl.BlockSpec((tm, TOP_K), lambda i: (i, 0)),
            pl.BlockSpec(memory_space=pl.ANY),
        ] + rows,
        out_specs=rows,
        out_shape=[jax.ShapeDtypeStruct(h.shape, F32) for h in h_groups],
        scratch_shapes=[pltpu.VMEM((2, TOP_K, tm * nsub, LANES), F32),
                        pltpu.SemaphoreType.DMA((2,))],
        compiler_params=_params(("arbitrary",)),
        name="moe_combine",
    )(dest3, dest3, w, og, *h_groups)


def _moe(groups, w_gu, w_down):
    ne, _, d = w_down.shape
    nsub = d // LANES
    idx = jnp.concatenate([g[2] for g in groups], axis=0)
    w = jnp.concatenate([g[3] for g in groups], axis=0)
    npairs = idx.shape[0] * TOP_K
    tmg = math.gcd(MOE_ROWS, npairs)
    dest, fill, tile_expert, n_tiles = _route_plan(idx, ne, tmg)
    n_rows = npairs + ne * tmg
    xg = _dispatch([g[1] for g in groups], dest, fill, n_rows, nsub)
    og = _moe_ffn(xg.reshape(n_rows * nsub, LANES), tile_expert, n_tiles, w_gu, w_down, tmg)
    return _combine([g[0] for g in groups], w, og.reshape(n_rows, nsub, LANES), dest)


def _trunk(x, pool_hist, kv_cache, p):
    b, n, d = x.shape
    n_heads = p["b_f"].shape[0]
    start_pos = 0 if kv_cache is None else kv_cache[0].shape[1]

    pool_args = (p["g_pool_norm"][0], p["w_pool"][0], p["pool_scale"][0], start_pos)
    ffn_args = (p["g_ffn_norm"][0], p["w_ffn_gu"][0], p["w_ffn_down"][0])
    if n >= 256:
        h, pool_state = _pool_layer(x, None if pool_hist is None else pool_hist[0], *pool_args,
                                    ffn=ffn_args)
        h = h.reshape(b * n, d)
    else:
        h, pool_state = _pool_layer(x, None if pool_hist is None else pool_hist[0], *pool_args)
        h = _dense_ffn(h.reshape(b * n, d), *ffn_args)

    qkv_args = (p["g_kv_norm"], p["g_attn_norm"][0], p["w_kvf"], p["b_f"], p["g_k"],
                p["w_q"][0], p["g_q"][0], n_heads)
    kvw = p["w_q"].shape[2]
    hd = kvw // n_heads
    if kv_cache is None:
        kt, vt, logf, k_aug, vtb, q_aug = _qkv_proj_t(h, b, *qkv_args)
        k = jnp.transpose(kt.reshape(b, n_heads, hd, n), (0, 3, 1, 2))
        v = jnp.transpose(vt.reshape(b, n_heads, hd, n), (0, 3, 1, 2))
        logf = logf.reshape(b, n, n_heads)
        o = _attn_prompt(q_aug, k_aug, vtb)
    else:
        k, v, logf, kb, vb, qb = _qkv_proj(h, *qkv_args)
        logf = logf.reshape(b, n, n_heads)
        qb, kb, vb = (a.reshape(b, n, kvw) for a in (qb, kb, vb))
        k = k.reshape(b, n, n_heads, hd)
        v = v.reshape(b, n, n_heads, hd)
        cache_k, cache_v, cache_logf = kv_cache
        past = cache_k.shape[1]
        total = past + n
        padded = -(-total // LANES) * LANES
        lf_t = jnp.concatenate(
            [jnp.transpose(cache_logf.astype(F32), (0, 2, 1)), jnp.transpose(logf, (0, 2, 1)),
             jnp.zeros((b, n_heads, padded - total), F32)], axis=2)
        c_t = _cumsum_lanes(lf_t.reshape(b * n_heads, padded)).reshape(b, n_heads, 1, padded)
        c_new = c_t[:, :, :, past:total]
        cache_kt = jnp.transpose(cache_k, (0, 2, 3, 1)).reshape(b, kvw, past)
        cache_vt = jnp.transpose(cache_v, (0, 2, 3, 1)).reshape(b, kvw, past)
        heads = lambda a: jnp.transpose(a.reshape(b, n, n_heads, hd), (0, 2, 1, 3))
        o = _attn_sample(heads(qb), heads(kb), heads(vb), cache_kt, cache_vt,
                         jnp.swapaxes(c_new, 2, 3), c_t, c_new)
        o = jnp.transpose(o, (0, 2, 1, 3))

    h2, u_tiles, idx, top_w = _oproj_router(h, o.reshape(b * n, kvw), p["w_o"][0],
                                            p["g_ffn_norm"][1], p["w_router"][0])
    return (h2, u_tiles, idx, top_w), pool_state[None], k, v, logf


def kernel(x_prompt, x_sample, state_pool, cache_k, cache_v, cache_logf, g_pool_norm, w_pool,
           pool_scale, g_kv_norm, w_kvf, b_f, g_k, g_attn_norm, w_q, g_q, w_o, g_ffn_norm,
           w_ffn_gu, w_ffn_down, w_router, w_moe_gu, w_moe_down):
    p = dict(g_pool_norm=g_pool_norm, w_pool=w_pool, pool_scale=pool_scale,
             g_kv_norm=g_kv_norm, w_kvf=w_kvf, b_f=b_f, g_k=g_k, g_attn_norm=g_attn_norm,
             w_q=w_q, g_q=g_q, w_o=w_o, g_ffn_norm=g_ffn_norm, w_ffn_gu=w_ffn_gu,
             w_ffn_down=w_ffn_down, w_router=w_router, w_moe_gu=w_moe_gu,
             w_moe_down=w_moe_down)
    assert w_pool.shape[0] == 1 and w_q.shape[0] == 1 and w_router.shape[0] == 1
    moe_p, pool_p, k_p, v_p, lf_p = _trunk(x_prompt, None, None, p)
    moe_s, pool_s, k_s, v_s, lf_s = _trunk(
        x_sample, state_pool, (cache_k, cache_v, cache_logf), p)
    y_p, y_s = _moe([moe_p, moe_s], w_moe_gu[0], w_moe_down[0])
    return (y_p.reshape(x_prompt.shape), y_s.reshape(x_sample.shape), pool_p, k_p, v_p, lf_p,
            pool_s, k_s, v_s, lf_s)
```

```python
import functools
import math

import jax
import jax.numpy as jnp
import numpy as np
from jax import lax
from jax.experimental import pallas as pl
from jax.experimental.pallas import tpu as pltpu

EPS = 1e-6
POOL_WINDOWS = (2, 4, 8, 16)
HALO = max(POOL_WINDOWS)
POOL_HIST = HALO - 1
TOP_K = 2
LANES = 128
BF16_ROWS = 16
MXU_DIM = 256
VMEM_LIMIT = 56 * 1024 * 1024
NEG_INF = float("-inf")

BF16 = jnp.bfloat16
F32 = jnp.float32


def _params(semantics, vmem=None):
    return pltpu.CompilerParams(dimension_semantics=semantics, vmem_limit_bytes=vmem)


def _dot(a, b):
    return jnp.dot(a, b, preferred_element_type=F32)


def _dot_nt(a, b):
    return lax.dot_general(a, b, (((1,), (1,)), ((), ())), preferred_element_type=F32)


def _rms_base(x):
    return x * lax.rsqrt(jnp.mean(x * x, axis=-1, keepdims=True) + EPS)


def _split2(x):
    hi = x.astype(BF16)
    lo = (x - hi.astype(F32)).astype(BF16)
    return hi, lo


def _split3(x):
    hi = x.astype(BF16)
    r = x - hi.astype(F32)
    mid = r.astype(BF16)
    lo = (r - mid.astype(F32)).astype(BF16)
    return hi, mid, lo


def _ffn_chunks(f):
    out, s = [], 0
    while s < f:
        n = min(4 * MXU_DIM, f - s)
        out.append((s, n))
        s += n
    return out


def _pool_kernel(x_ref, halo_ref, g_ref, w_ref, scale_ref, *rest, tm, start_pos, halo_is_normed,
                 f):
    if f is None:
        h_ref, st_ref = rest
    else:
        gf_ref, wgu_ref, wd_ref, h_ref, st_ref = rest
    i = pl.program_id(1)
    x = x_ref[...]
    g = g_ref[...]
    u = _rms_base(x) * g
    if halo_is_normed:
        uh = halo_ref[...]
    else:
        uh = _rms_base(halo_ref[...]) * g
        uh = jnp.where(i > 0, uh, 0.0)
    ext = jnp.concatenate([uh, u], axis=0)

    row = lax.broadcasted_iota(jnp.int32, (tm, 1), 0)
    avail = start_pos + i * tm + row + 1
    group = x.shape[1] // len(POOL_WINDOWS)
    ys = []
    for gi, w in enumerate(POOL_WINDOWS):
        lo, hi = gi * group, (gi + 1) * group
        s = ext[:, lo:hi]
        step = 1
        while step < w:
            s = s + pltpu.roll(s, step, 0)
            step *= 2
        inv_cnt = 1.0 / jnp.minimum(avail, w).astype(F32)
        pooled = s[HALO:, :] * inv_cnt
        diff = pooled - u[:, lo:hi]
        ys.append(_dot(diff.astype(BF16), w_ref[gi]))
    h = x + jnp.concatenate(ys, axis=1) * scale_ref[...]
    if f is not None:
        h = h + _swiglu((_rms_base(h) * gf_ref[...]).astype(BF16), wgu_ref, wd_ref, f)
    h_ref[...] = h

    @pl.when(i == pl.num_programs(1) - 1)
    def _():
        st_ref[...] = u[tm - HALO:, :]


def _pool_layer(x, hist, g, w_pool, pool_scale, start_pos, ffn=None):
    b, n, d = x.shape
    tm = min(n, 256 if ffn is None else 512)
    assert n % tm == 0 and tm % HALO == 0 and n >= HALO
    nt = n // tm
    if hist is None:
        halo_arr = x
        halo_spec = pl.BlockSpec(
            (None, HALO, d), lambda bi, i: (bi, jnp.maximum(i * (tm // HALO) - 1, 0), 0))
    else:
        assert nt == 1
        halo_arr = jnp.pad(hist, ((0, 0), (HALO - POOL_HIST, 0), (0, 0)))
        halo_spec = pl.BlockSpec((None, HALO, d), lambda bi, i: (bi, 0, 0))
    ng = len(POOL_WINDOWS)
    vec = pl.BlockSpec((1, d), lambda bi, i: (0, 0))
    in_specs = [pl.BlockSpec((None, tm, d), lambda bi, i: (bi, i, 0)), halo_spec, vec,
                pl.BlockSpec((ng, d // ng, d // ng), lambda bi, i: (0, 0, 0)), vec]
    args = [x, halo_arr, g.reshape(1, d), w_pool.astype(BF16), pool_scale.reshape(1, d)]
    f = None
    if ffn is not None:
        g_ffn, w_gu, w_down = ffn
        f = w_down.shape[0]
        in_specs += [vec, pl.BlockSpec((d, 2 * f), lambda bi, i: (0, 0)),
                     pl.BlockSpec((f, d), lambda bi, i: (0, 0))]
        args += [g_ffn.reshape(1, d), w_gu.astype(BF16), w_down.astype(BF16)]
    h, st = pl.pallas_call(
        functools.partial(_pool_kernel, tm=tm, start_pos=start_pos,
                          halo_is_normed=hist is not None, f=f),
        grid=(b, nt),
        in_specs=in_specs,
        out_specs=[
            pl.BlockSpec((None, tm, d), lambda bi, i: (bi, i, 0)),
            pl.BlockSpec((None, HALO, d), lambda bi, i: (bi, 0, 0)),
        ],
        out_shape=[jax.ShapeDtypeStruct((b, n, d), F32),
                   jax.ShapeDtypeStruct((b, HALO, d), F32)],
        compiler_params=_params(("parallel", "arbitrary"), None if ffn is None else VMEM_LIMIT),
        name="pool_mixer" if ffn is None else "pool_ffn",
    )(*args)
    return h, st[:, HALO - POOL_HIST:, :]


def _swiglu(xb, wgu_ref, wd_ref, f):
    acc = None
    for s, n in _ffn_chunks(f):
        gate = _dot(xb, wgu_ref[:, s:s + n])
        up = _dot(xb, wgu_ref[:, f + s:f + s + n])
        act = (gate * jax.nn.sigmoid(gate)) * up
        part = _dot(act.astype(BF16), wd_ref[s:s + n, :])
        acc = part if acc is None else acc + part
    return acc


def _dense_ffn_kernel(h_ref, g_ref, wgu_ref, wd_ref, o_ref, *, f):
    h = h_ref[...]
    u = (_rms_base(h) * g_ref[...]).astype(BF16)
    o_ref[...] = h + _swiglu(u, wgu_ref, wd_ref, f)


def _dense_ffn(h, g, w_gu, w_down):
    t, d = h.shape
    f = w_down.shape[0]
    tm = min(t, 512)
    assert t % tm == 0
    return pl.pallas_call(
        functools.partial(_dense_ffn_kernel, f=f),
        grid=(t // tm,),
        in_specs=[
            pl.BlockSpec((tm, d), lambda i: (i, 0)),
            pl.BlockSpec((1, d), lambda i: (0, 0)),
            pl.BlockSpec((d, 2 * f), lambda i: (0, 0)),
            pl.BlockSpec((f, d), lambda i: (0, 0)),
        ],
        out_specs=pl.BlockSpec((tm, d), lambda i: (i, 0)),
        out_shape=jax.ShapeDtypeStruct((t, d), F32),
        compiler_params=_params(("parallel",), VMEM_LIMIT),
        name="dense_ffn",
    )(h, g.reshape(1, d), w_gu.astype(BF16), w_down.astype(BF16))


def _head_norm(x, bd_ref, g):
    sq_hi, sq_lo = _split2(x * x)
    n = bd_ref.shape[0]
    ms = jnp.concatenate(
        [_dot(sq_hi[:, c:c + n], bd_ref[...]) + _dot(sq_lo[:, c:c + n], bd_ref[...])
         for c in range(0, x.shape[1], n)], axis=1)
    return x * lax.rsqrt(ms + EPS) * g


def _log_sigmoid(z):
    return jnp.minimum(z, 0.0) - jnp.log1p(jnp.exp(-jnp.abs(z)))


def _qkv_kernel(h_ref, gkv_ref, gat_ref, wk_ref, wv_ref, wf_ref, wq_ref, bf_ref,
                gk_ref, gq_ref, bd_ref,
                k_ref, v_ref, lf_ref, kb_ref, vb_ref, qb_ref, *, n_heads, q_scale):
    base = _rms_base(h_ref[...])
    a_kv = (base * gkv_ref[...]).astype(BF16)
    a_q = (base * gat_ref[...]).astype(BF16)
    k = _head_norm(_dot(a_kv, wk_ref[...]), bd_ref, gk_ref[...])
    v = _dot(a_kv, wv_ref[...])
    z = _dot(a_kv, wf_ref[...])[:, :n_heads] + bf_ref[...]
    q = _head_norm(_dot(a_q, wq_ref[...]), bd_ref, gq_ref[...])
    k_ref[...] = k
    v_ref[...] = v
    lf_ref[...] = _log_sigmoid(z)
    kb_ref[...] = k.astype(BF16)
    vb_ref[...] = v.astype(BF16)
    qb_ref[...] = (q * q_scale).astype(BF16)


def _qkv_proj(h, g_kv, g_attn, w_kvf, b_f, g_k, w_q, g_q, n_heads):
    t, d = h.shape
    kvw = w_q.shape[1]
    hd = kvw // n_heads
    tm = min(t, 256)
    assert t % tm == 0 and MXU_DIM % hd == 0 and kvw % MXU_DIM == 0
    wk = w_kvf[:, :kvw].astype(BF16)
    wv = w_kvf[:, kvw:2 * kvw].astype(BF16)
    wf = jnp.pad(w_kvf[:, 2 * kvw:], ((0, 0), (0, LANES - n_heads))).astype(BF16)
    head = jnp.arange(MXU_DIM) // hd
    bd = ((head[:, None] == head[None, :]).astype(F32) / hd).astype(BF16)
    row = lambda i: (i, 0)
    fix = lambda i: (0, 0)
    return pl.pallas_call(
        functools.partial(_qkv_kernel, n_heads=n_heads, q_scale=hd ** -0.5),
        grid=(t // tm,),
        in_specs=[
            pl.BlockSpec((tm, d), row),
            pl.BlockSpec((1, d), fix), pl.BlockSpec((1, d), fix),
            pl.BlockSpec((d, kvw), fix), pl.BlockSpec((d, kvw), fix),
            pl.BlockSpec((d, LANES), fix), pl.BlockSpec((d, kvw), fix),
            pl.BlockSpec((1, n_heads), fix),
            pl.BlockSpec((1, kvw), fix), pl.BlockSpec((1, kvw), fix),
            pl.BlockSpec((MXU_DIM, MXU_DIM), fix),
        ],
        out_specs=[
            pl.BlockSpec((tm, kvw), row), pl.BlockSpec((tm, kvw), row),
            pl.BlockSpec((tm, n_heads), row),
            pl.BlockSpec((tm, kvw), row), pl.BlockSpec((tm, kvw), row),
            pl.BlockSpec((tm, kvw), row),
        ],
        out_shape=[
            jax.ShapeDtypeStruct((t, kvw), F32), jax.ShapeDtypeStruct((t, kvw), F32),
            jax.ShapeDtypeStruct((t, n_heads), F32),
            jax.ShapeDtypeStruct((t, kvw), BF16), jax.ShapeDtypeStruct((t, kvw), BF16),
            jax.ShapeDtypeStruct((t, kvw), BF16),
        ],
        compiler_params=_params(("parallel",), VMEM_LIMIT),
        name="qkv_proj",
    )(h, g_kv.reshape(1, d), g_attn.reshape(1, d), wk, wv, wf, w_q.astype(BF16),
      b_f.reshape(1, n_heads), jnp.tile(g_k, n_heads).reshape(1, kvw),
      jnp.tile(g_q, n_heads).reshape(1, kvw), bd)


def _cumsum_tile(x, tri, carry):
    y = carry
    for part in _split3(x):
        y = y + _dot(part, tri)
    return y


def _bias_placement(n_heads, hd, c_first):
    p = np.zeros((4 * n_heads, n_heads * LANES), np.float32)
    for h in range(n_heads):
        first = h * LANES + (h * hd % LANES + hd) % LANES
        for i in range(3):
            term, one = (first + i, first + 3 + i) if c_first else (first + 3 + i, first + i)
            p[i * n_heads + h, term] = 1.0 if c_first else -1.0
            p[3 * n_heads, one] = 1.0
    return jnp.asarray(p, BF16)


def _augment(x, c_terms, place_ref, n_heads, hd):
    ext = _dot(c_terms, place_ref[...]).astype(BF16)
    placed = jnp.concatenate(
        [x[:, h * hd // LANES * LANES:h * hd // LANES * LANES + LANES] for h in range(n_heads)],
        axis=1)
    lane = lax.broadcasted_iota(jnp.int32, (1, n_heads * LANES), 1)
    head, within = lane // LANES, lane % LANES
    own = within // hd == head % (LANES // hd)
    return jnp.where(own, placed, ext)


def _qkv_t_kernel(h_ref, gkv_ref, gat_ref, wkt_ref, wvt_ref, wf_ref, wq_ref, bf_ref,
                  gk_ref, gq_ref, bd_ref, tri_ref, pk_ref, pq_ref,
                  kt_ref, vt_ref, lf_ref, kaug_ref, vtb_ref, qaug_ref, carry_ref, *,
                  n_heads, q_scale):
    @pl.when(pl.program_id(1) == 0)
    def _():
        carry_ref[...] = jnp.zeros(carry_ref.shape, F32)

    base = _rms_base(h_ref[...])
    a_kv = (base * gkv_ref[...]).astype(BF16)
    a_q = (base * gat_ref[...]).astype(BF16)
    tm = a_kv.shape[0]
    kraw = _dot_nt(wkt_ref[...], a_kv)
    hd = kraw.shape[0] // n_heads
    k3 = kraw.reshape(n_heads, hd, tm)
    ms = jnp.mean(k3 * k3, axis=1, keepdims=True)
    kt = (k3 * lax.rsqrt(ms + EPS) * gk_ref[...][None]).reshape(kraw.shape)
    vt = _dot_nt(wvt_ref[...], a_kv)
    lf = _log_sigmoid(_dot(a_kv, wf_ref[...])[:, :n_heads] + bf_ref[...])
    sums = _dot(tri_ref[...], jnp.concatenate(_split3(lf), axis=1))
    c = carry_ref[...] + sum(sums[:, i * n_heads:(i + 1) * n_heads] for i in range(3))
    carry_ref[...] = c[tm - 1:tm, :]
    q = _head_norm(_dot(a_q, wq_ref[...]), bd_ref, gq_ref[...])
    kt_ref[...] = kt
    vt_ref[...] = vt
    lf_ref[...] = lf
    vtb_ref[...] = vt.astype(BF16)
    c_terms = jnp.concatenate(
        _split3(c * math.log2(math.e)) + (jnp.ones((tm, n_heads), BF16),), axis=1)
    k_aug = _augment(kt.T.astype(BF16), c_terms, pk_ref, n_heads, hd)
    q_aug = _augment((q * q_scale).astype(BF16), c_terms, pq_ref, n_heads, hd)
    for h in range(n_heads):
        kaug_ref[h] = k_aug[:, h * LANES:(h + 1) * LANES]
        qaug_ref[h] = q_aug[:, h * LANES:(h + 1) * LANES]


def _qkv_proj_t(h, b, g_kv, g_attn, w_kvf, b_f, g_k, w_q, g_q, n_heads):
    t, d = h.shape
    s = t // b
    kvw = w_q.shape[1]
    hd = kvw // n_heads
    tm = min(s, 512)
    nt = s // tm
    assert s % tm == 0 and MXU_DIM % hd == 0 and kvw % MXU_DIM == 0
    assert LANES % hd == 0 and hd + 6 <= LANES
    wkt = w_kvf[:, :kvw].T.astype(BF16)
    wvt = w_kvf[:, kvw:2 * kvw].T.astype(BF16)
    wf = jnp.pad(w_kvf[:, 2 * kvw:], ((0, 0), (0, LANES - n_heads))).astype(BF16)
    head = jnp.arange(MXU_DIM) // hd
    bd = ((head[:, None] == head[None, :]).astype(F32) / hd).astype(BF16)
    pos = jnp.arange(tm)
    tri = (pos[:, None] >= pos[None, :]).astype(BF16)
    fix = lambda bi, i: (0, 0)
    feat = pl.BlockSpec((None, kvw, tm), lambda bi, i: (bi, 0, i))
    aug = pl.BlockSpec((None, n_heads, tm, LANES), lambda bi, i: (bi, 0, i, 0))
    return pl.pallas_call(
        functools.partial(_qkv_t_kernel, n_heads=n_heads,
                          q_scale=hd ** -0.5 * math.log2(math.e)),
        grid=(b, nt),
        in_specs=[
            pl.BlockSpec((tm, d), lambda bi, i: (bi * nt + i, 0)),
            pl.BlockSpec((1, d), fix), pl.BlockSpec((1, d), fix),
            pl.BlockSpec((kvw, d), fix), pl.BlockSpec((kvw, d), fix),
            pl.BlockSpec((d, LANES), fix), pl.BlockSpec((d, kvw), fix),
            pl.BlockSpec((1, n_heads), fix), pl.BlockSpec((hd, tm), fix),
            pl.BlockSpec((1, kvw), fix), pl.BlockSpec((MXU_DIM, MXU_DIM), fix),
            pl.BlockSpec((tm, tm), fix),
            pl.BlockSpec((4 * n_heads, n_heads * LANES), fix),
            pl.BlockSpec((4 * n_heads, n_heads * LANES), fix),
        ],
        out_specs=[feat, feat, pl.BlockSpec((tm, n_heads), lambda bi, i: (bi * nt + i, 0)),
                   aug, feat, aug],
        out_shape=[jax.ShapeDtypeStruct((b, kvw, s), F32), jax.ShapeDtypeStruct((b, kvw, s), F32),
                   jax.ShapeDtypeStruct((t, n_heads), F32),
                   jax.ShapeDtypeStruct((b, n_heads, s, LANES), BF16),
                   jax.ShapeDtypeStruct((b, kvw, s), BF16),
                   jax.ShapeDtypeStruct((b, n_heads, s, LANES), BF16)],
        scratch_shapes=[pltpu.VMEM((1, n_heads), F32)],
        compiler_params=_params(("parallel", "arbitrary"), VMEM_LIMIT),
        name="qkv_proj_t",
    )(h, g_kv.reshape(1, d), g_attn.reshape(1, d), wkt, wvt, wf, w_q.astype(BF16),
      b_f.reshape(1, n_heads), jnp.broadcast_to(g_k[:, None], (hd, tm)),
      jnp.tile(g_q, n_heads).reshape(1, kvw), bd, tri,
      _bias_placement(n_heads, hd, False), _bias_placement(n_heads, hd, True))


def _cumsum_kernel(x_ref, o_ref):
    r, l = x_ref.shape
    ii = lax.broadcasted_iota(jnp.int32, (LANES, LANES), 0)
    jj = lax.broadcasted_iota(jnp.int32, (LANES, LANES), 1)
    tri = (ii <= jj).astype(BF16)
    carry = jnp.zeros((r, 1), F32)
    for c in range(0, l, LANES):
        y = _cumsum_tile(x_ref[:, c:c + LANES], tri, carry)
        o_ref[:, c:c + LANES] = y
        carry = y[:, LANES - 1:LANES]


def _cumsum_lanes(x):
    r, l = x.shape
    tr = min(r, 256)
    assert r % tr == 0 and l % LANES == 0
    return pl.pallas_call(
        _cumsum_kernel,
        grid=(r // tr,),
        in_specs=[pl.BlockSpec((tr, l), lambda i: (i, 0))],
        out_specs=pl.BlockSpec((tr, l), lambda i: (i, 0)),
        out_shape=jax.ShapeDtypeStruct((r, l), F32),
        compiler_params=_params(("parallel",)),
        name="logf_cumsum",
    )(x)


def _bdot(a, b, ca, cb):
    return lax.dot_general(a, b, (((ca,), (cb,)), ((0,), (0,))), preferred_element_type=F32)


def _attn_prompt_kernel(q_ref, kaug_ref, vt_ref, o_ref, m_ref, acc_ref, *, tq):
    qi = pl.program_id(1)
    n_heads = q_ref.shape[0]
    kvw = vt_ref.shape[0]
    hd = kvw // n_heads
    q_aug = q_ref[...]
    m_ref[...] = jnp.full(m_ref.shape, NEG_INF, F32)
    acc_ref[...] = jnp.zeros(acc_ref.shape, F32)
    n_den = acc_ref.shape[1] - hd
    den_rows = (lax.broadcasted_iota(jnp.int32, (n_heads, n_den, tq), 1) == 0).astype(BF16)

    def tile(kt, masked):
        start = pl.multiple_of(kt * tq, tq)
        st = _bdot(kaug_ref[:, pl.ds(start, tq), :], q_aug, 2, 2)
        if masked:
            key = lax.broadcasted_iota(jnp.int32, (1, tq, tq), 1)
            qry = lax.broadcasted_iota(jnp.int32, (1, tq, tq), 2)
            st = jnp.where(key <= qry, st, NEG_INF)
        m_prev = m_ref[...]
        m_new = jnp.maximum(m_prev, jnp.max(st, axis=1, keepdims=True))
        alpha = jnp.exp2(m_prev - m_new)
        p = jnp.exp2(st - m_new)
        vt = vt_ref[:, pl.ds(start, tq)].reshape(n_heads, hd, tq)
        vt = jnp.concatenate([vt, den_rows], axis=1)
        acc_ref[...] = alpha * acc_ref[...] + _bdot(vt, p.astype(BF16), 2, 1)
        m_ref[...] = m_new

    def body(kt, carry):
        tile(kt, False)
        return carry

    lax.fori_loop(0, qi, body, 0)
    tile(qi, True)
    ot = (acc_ref[:, :hd, :] / acc_ref[:, hd:hd + 1, :]).reshape(kvw, tq)
    o_ref[...] = ot.T.astype(o_ref.dtype)


def _attn_prompt(q_aug, k_aug, vt):
    b, n_heads, s, _ = q_aug.shape
    kvw = vt.shape[1]
    hd = kvw // n_heads
    tq = min(s, 256)
    assert s % tq == 0
    return pl.pallas_call(
        functools.partial(_attn_prompt_kernel, tq=tq),
        grid=(b, s // tq),
        in_specs=[
            pl.BlockSpec((None, n_heads, tq, LANES), lambda bi, qi: (bi, 0, qi, 0)),
            pl.BlockSpec((None, n_heads, s, LANES), lambda bi, qi: (bi, 0, 0, 0)),
            pl.BlockSpec((None, kvw, s), lambda bi, qi: (bi, 0, 0)),
        ],
        out_specs=pl.BlockSpec((None, tq, kvw), lambda bi, qi: (bi, qi, 0)),
        out_shape=jax.ShapeDtypeStruct((b, s, kvw), BF16),
        scratch_shapes=[pltpu.VMEM((n_heads, 1, tq), F32),
                        pltpu.VMEM((n_heads, hd + BF16_ROWS, tq), F32)],
        compiler_params=_params(("parallel", "arbitrary"), VMEM_LIMIT),
        name="fox_prompt",
    )(q_aug, k_aug, vt)


def _attn_sample_kernel(q_ref, cq_ref, kc_ref, vc_ref, ckc_ref, kn_ref, vn_ref, ckn_ref,
                        o_ref, m_ref, acc_ref, *, n_heads):
    kt = pl.program_id(1)
    last = pl.num_programs(1) - 1
    q = q_ref[...]
    cq = cq_ref[...]
    n, hd = q.shape[1:]
    n_den = acc_ref.shape[2] - hd

    @pl.when(kt == 0)
    def _():
        m_ref[...] = jnp.full(m_ref.shape, NEG_INF, F32)
        acc_ref[...] = jnp.zeros(acc_ref.shape, F32)

    def update(s, v, v_contract):
        m_prev = m_ref[...]
        m_new = jnp.maximum(m_prev, jnp.max(s, axis=2, keepdims=True))
        p = jnp.exp(s - m_new).astype(BF16)
        acc_ref[...] = jnp.exp(m_prev - m_new) * acc_ref[...] + _bdot(p, v, 2, v_contract)
        m_ref[...] = m_new

    kvw, tk = kc_ref.shape
    split = (n_heads, kvw // n_heads, tk)
    kk = kc_ref[...].reshape(split).astype(BF16)
    den = (lax.broadcasted_iota(jnp.int32, (n_heads, n_den, tk), 1) == 0).astype(BF16)
    vv = jnp.concatenate([vc_ref[...].reshape(split).astype(BF16), den], axis=1)
    update(_bdot(q, kk, 2, 1) + (cq - ckc_ref[...]), vv, 2)

    @pl.when(kt == last)
    def _():
        r = lax.broadcasted_iota(jnp.int32, (1, n, n), 1)
        c = lax.broadcasted_iota(jnp.int32, (1, n, n), 2)
        s = _bdot(q, kn_ref[...], 2, 2) + (cq - ckn_ref[...])
        den_new = (lax.broadcasted_iota(jnp.int32, (n_heads, n, n_den), 2) == 0).astype(BF16)
        update(jnp.where(c <= r, s, NEG_INF), jnp.concatenate([vn_ref[...], den_new], axis=2), 1)
        acc = acc_ref[...]
        o_ref[...] = (acc[:, :, :hd] / acc[:, :, hd:hd + 1]).astype(o_ref.dtype)


def _attn_sample(q, k_new, v_new, cache_kt, cache_vt, cq, ck_cache, ck_new):
    b, n_heads, n, hd = q.shape
    kvw, p = cache_kt.shape[1:]
    tk = min(p, 2048)
    assert p % tk == 0 and kvw == n_heads * hd and ck_cache.shape[-1] >= p
    fix = lambda bi, kt: (bi, 0, 0, 0)
    rows = pl.BlockSpec((None, n_heads, n, hd), fix)
    cache = pl.BlockSpec((None, kvw, tk), lambda bi, kt: (bi, 0, kt))
    return pl.pallas_call(
        functools.partial(_attn_sample_kernel, n_heads=n_heads),
        grid=(b, p // tk),
        in_specs=[
            rows,
            pl.BlockSpec((None, n_heads, n, 1), fix),
            cache, cache,
            pl.BlockSpec((None, n_heads, 1, tk), lambda bi, kt: (bi, 0, 0, kt)),
            rows, rows,
            pl.BlockSpec((None, n_heads, 1, n), fix),
        ],
        out_specs=rows,
        out_shape=jax.ShapeDtypeStruct((b, n_heads, n, hd), BF16),
        scratch_shapes=[pltpu.VMEM((n_heads, n, 1), F32),
                        pltpu.VMEM((n_heads, n, hd + BF16_ROWS), F32)],
        compiler_params=_params(("parallel", "arbitrary"), VMEM_LIMIT),
        name="fox_sample",
    )(q, cq, cache_kt, cache_vt, ck_cache, k_new, v_new, ck_new)


def _store_token_tiles(ref, x):
    m, d = x.shape
    nsub = d // LANES
    for s in range(nsub):
        ref[pl.ds(s, m, stride=nsub), :] = x[:, s * LANES:(s + 1) * LANES]


def _load_token_tiles(ref, m, nsub, first=0, stride=None):
    stride = nsub if stride is None else stride
    return jnp.concatenate(
        [ref[pl.ds(first + s, m, stride=stride), :] for s in range(nsub)], axis=1)


def _oproj_router_kernel(h_ref, o_ref, wo_ref, g_ref, wr_ref, h2_ref, u_ref, idx_ref, w_ref, *,
                         n_experts):
    h2 = h_ref[...] + _dot(o_ref[...], wo_ref[...])
    u = _rms_base(h2) * g_ref[...]
    h2_ref[...] = h2
    _store_token_tiles(u_ref, u)
    u_hi, u_lo = _split2(u)
    both = _dot(u_hi, wr_ref[...])
    logits = both[:, :LANES] + both[:, LANES:] + _dot(u_lo, wr_ref[:, :LANES])
    lane = lax.broadcasted_iota(jnp.int32, logits.shape, 1).astype(F32)
    logits = jnp.where(lane < n_experts, logits, NEG_INF)
    m1 = jnp.max(logits, axis=1, keepdims=True)
    i1 = jnp.min(jnp.where(logits == m1, lane, float(LANES)), axis=1, keepdims=True)
    rest = jnp.where(lane == i1, NEG_INF, logits)
    m2 = jnp.max(rest, axis=1, keepdims=True)
    i2 = jnp.min(jnp.where(rest == m2, lane, float(LANES)), axis=1, keepdims=True)
    e2 = jnp.exp(m2 - m1)
    w1 = 1.0 / (1.0 + e2)
    w2 = e2 / (1.0 + e2)
    idx_ref[...] = jnp.concatenate([i1, i2], axis=1).astype(jnp.int32)
    w_ref[...] = jnp.concatenate([w1, w2], axis=1)


def _oproj_router(h, o, w_o, g, w_router):
    t, d = h.shape
    kvw = o.shape[1]
    ne = w_router.shape[1]
    tm = min(t, 512)
    nsub = d // LANES
    assert t % tm == 0 and ne <= LANES
    wr = jnp.pad(w_router, ((0, 0), (0, LANES - ne)))
    wr_hi = wr.astype(BF16)
    wr_lo = (wr - wr_hi.astype(F32)).astype(BF16)
    row = lambda i: (i, 0)
    return pl.pallas_call(
        functools.partial(_oproj_router_kernel, n_experts=ne),
        grid=(t // tm,),
        in_specs=[
            pl.BlockSpec((tm, d), row), pl.BlockSpec((tm, kvw), row),
            pl.BlockSpec((kvw, d), lambda i: (0, 0)),
            pl.BlockSpec((1, d), lambda i: (0, 0)),
            pl.BlockSpec((d, 2 * LANES), lambda i: (0, 0)),
        ],
        out_specs=[pl.BlockSpec((tm, d), row), pl.BlockSpec((tm * nsub, LANES), row),
                   pl.BlockSpec((tm, TOP_K), row), pl.BlockSpec((tm, TOP_K), row)],
        out_shape=[jax.ShapeDtypeStruct((t, d), F32),
                   jax.ShapeDtypeStruct((t * nsub, LANES), F32),
                   jax.ShapeDtypeStruct((t, TOP_K), jnp.int32),
                   jax.ShapeDtypeStruct((t, TOP_K), F32)],
        compiler_params=_params(("parallel",)),
        name="oproj_router",
    )(h, o, w_o.astype(BF16), g.reshape(1, d), jnp.concatenate([wr_hi, wr_lo], axis=1))


MOE_ROWS = 512
DMA_UNROLL = 16


def _route_plan(idx, ne, tmg):
    t = idx.shape[0]
    npairs = t * TOP_K
    e_flat = idx.reshape(npairs)
    onehot = (e_flat[:, None] == jnp.arange(ne, dtype=jnp.int32)[None, :]).astype(jnp.int32)
    csum = jnp.cumsum(onehot, axis=0)
    cnt = csum[-1]
    padded = (cnt + tmg - 1) // tmg * tmg
    ends = jnp.cumsum(padded)
    off = ends - padded
    dest = jnp.sum(onehot * (off[None, :] + csum), axis=1) - 1
    total = ends[-1]
    k = jnp.arange(tmg, dtype=jnp.int32)[None, :]
    valid = (k < (padded - cnt)[:, None]).reshape(-1)
    pad_row = ((off + cnt)[:, None] + k).reshape(-1)
    tail_row = total + jnp.cumsum(jnp.logical_not(valid).astype(jnp.int32)) - 1
    fill = jnp.where(valid, pad_row, tail_row)
    n_tiles = total // tmg
    nt_max = (npairs + ne * tmg) // tmg
    tile = jnp.minimum(jnp.arange(nt_max, dtype=jnp.int32), n_tiles - 1)
    tile_expert = jnp.sum((tile[:, None] >= (ends // tmg)[None, :]).astype(jnp.int32), axis=1)
    return dest, fill, tile_expert, n_tiles.reshape(1)


def _step_ranges(steps):
    out, first = [], 0
    for n in steps:
        out.append((first, n))
        first += n
    return out


def _in_range(i, first, count):
    return jnp.logical_and(i >= first, i < first + count)


def _dispatch_kernel(dest_ref, fill_ref, *rest, ch, ranges, nsub):
    u_refs, (xg_hbm, zero_ref, sem) = rest[:len(ranges)], rest[len(ranges):]
    i = pl.program_id(0)

    def issue(src_row, dst_rows):
        def body(j, carry):
            for jj in range(DMA_UNROLL):
                pltpu.make_async_copy(src_row(j, jj), xg_hbm.at[dst_rows[0, j * DMA_UNROLL + jj]],
                                      sem.at[0]).start(priority=jj % 2)
            return carry
        lax.fori_loop(0, ch // DMA_UNROLL, body, 0)

    for u_ref, (first, count) in zip(u_refs, ranges):
        def token_tile(j, jj, u_ref=u_ref):
            tok = j * (DMA_UNROLL // TOP_K) + jj // TOP_K
            return u_ref.at[pl.ds(pl.multiple_of(tok * nsub, nsub), nsub), :]

        @pl.when(_in_range(i, first, count))
        def _(token_tile=token_tile):
            issue(token_tile, dest_ref)

    pair_steps = ranges[-1][0] + ranges[-1][1]

    @pl.when(i >= pair_steps)
    def _():
        zero_ref[...] = jnp.zeros(zero_ref.shape, zero_ref.dtype)
        issue(lambda j, jj: zero_ref, fill_ref)

    pltpu.make_async_copy(xg_hbm.at[pl.ds(0, ch)], xg_hbm.at[pl.ds(0, ch)], sem.at[0]).wait()


def _dispatch(u_groups, dest, fill, n_rows, nsub):
    pairs = [u.shape[0] // nsub * TOP_K for u in u_groups]
    assert sum(pairs) == dest.shape[0] and DMA_UNROLL % TOP_K == 0
    ch = math.gcd(2 * MOE_ROWS, *pairs)
    assert fill.shape[0] % ch == 0 and ch % DMA_UNROLL == 0
    ranges = _step_ranges([p // ch for p in pairs])
    pair_steps = sum(p // ch for p in pairs)
    fill_steps = fill.shape[0] // ch
    smem = pltpu.SMEM
    clamp = lambda i, first, count: jnp.clip(i - first, 0, count - 1)
    return pl.pallas_call(
        functools.partial(_dispatch_kernel, ch=ch, ranges=ranges, nsub=nsub),
        grid=(pair_steps + fill_steps,),
        in_specs=[
            pl.BlockSpec((None, 1, ch), lambda i: (jnp.minimum(i, pair_steps - 1), 0, 0),
                         memory_space=smem),
            pl.BlockSpec((None, 1, ch), lambda i: (jnp.maximum(i - pair_steps, 0), 0, 0),
                         memory_space=smem),
        ] + [pl.BlockSpec((ch // TOP_K * nsub, LANES),
                          lambda i, first=first, count=count: (clamp(i, first, count), 0))
             for first, count in ranges],
        out_specs=pl.BlockSpec(memory_space=pl.ANY),
        out_shape=jax.ShapeDtypeStruct((n_rows, nsub, LANES), F32),
        scratch_shapes=[pltpu.VMEM((nsub, LANES), F32), pltpu.SemaphoreType.DMA((1,))],
        compiler_params=_params(("arbitrary",)),
        name="moe_dispatch",
    )(dest.reshape(pair_steps, 1, ch), fill.reshape(fill_steps, 1, ch), *u_groups)


def _moe_ffn_kernel(te_ref, nt_ref, x_ref, wgu_ref, wd_ref, o_ref, *, f, tmg, nsub):
    del te_ref
    live = pl.program_id(0) < nt_ref[0]

    @pl.when(live)
    def _():
        x = _load_token_tiles(x_ref, tmg, nsub).astype(BF16)
        _store_token_tiles(o_ref, _swiglu(x, wgu_ref, wd_ref, f))

    @pl.when(jnp.logical_not(live))
    def _():
        o_ref[...] = jnp.zeros(o_ref.shape, o_ref.dtype)


def _moe_ffn(xg, tile_expert, n_tiles, w_gu, w_down, tmg):
    ne, f, d = w_down.shape
    nsub = d // LANES
    nt_max = tile_expert.shape[0]
    assert xg.shape[0] == nt_max * tmg * nsub
    rows = lambda i, te, nt: (jnp.minimum(i, nt[0] - 1), 0)
    return pl.pallas_call(
        functools.partial(_moe_ffn_kernel, f=f, tmg=tmg, nsub=nsub),
        grid_spec=pltpu.PrefetchScalarGridSpec(
            num_scalar_prefetch=2,
            grid=(nt_max,),
            in_specs=[
                pl.BlockSpec((tmg * nsub, LANES), rows),
                pl.BlockSpec((None, d, 2 * f), lambda i, te, nt: (te[i], 0, 0)),
                pl.BlockSpec((None, f, d), lambda i, te, nt: (te[i], 0, 0)),
            ],
            out_specs=pl.BlockSpec((tmg * nsub, LANES), lambda i, te, nt: (i, 0)),
        ),
        out_shape=jax.ShapeDtypeStruct(xg.shape, F32),
        compiler_params=_params(("arbitrary",), VMEM_LIMIT),
        name="moe_experts",
    )(tile_expert, n_tiles, xg, w_gu.astype(BF16), w_down.astype(BF16))


def _combine_kernel(dcur_ref, dnxt_ref, w_ref, og_hbm, *rest, tm, nsub, ranges):
    ng = len(ranges)
    h_refs, y_refs, (buf, sem) = rest[:ng], rest[ng:2 * ng], rest[2 * ng:]
    i = pl.program_id(0)
    n = pl.num_programs(0)
    rows = TOP_K * tm

    def issue(d_ref, slot):
        def body(j, carry):
            for jj in range(DMA_UNROLL):
                tok = j * (DMA_UNROLL // TOP_K) + jj // TOP_K
                dst = buf.at[slot, jj % TOP_K, pl.ds(pl.multiple_of(tok * nsub, nsub), nsub), :]
                pltpu.make_async_copy(og_hbm.at[d_ref[0, j * DMA_UNROLL + jj]], dst,
                                      sem.at[slot]).start(priority=jj % 2)
            return carry
        lax.fori_loop(0, rows // DMA_UNROLL, body, 0)

    @pl.when(i == 0)
    def _():
        issue(dcur_ref, 0)

    @pl.when(i + 1 < n)
    def _():
        issue(dnxt_ref, lax.rem(i + 1, 2))

    slot = lax.rem(i, 2)
    pltpu.make_async_copy(buf.at[slot], buf.at[slot], sem.at[slot]).wait()
    w = w_ref[...]
    moe = (w[:, 0:1] * _load_token_tiles(buf.at[slot, 0], tm, nsub)
           + w[:, 1:2] * _load_token_tiles(buf.at[slot, 1], tm, nsub))
    for h_ref, y_ref, (first, count) in zip(h_refs, y_refs, ranges):
        @pl.when(_in_range(i, first, count))
        def _(h_ref=h_ref, y_ref=y_ref):
            y_ref[...] = h_ref[...] + moe


def _combine(h_groups, w, og, dest):
    assert TOP_K == 2
    d = h_groups[0].shape[1]
    nsub = d // LANES
    tm = math.gcd(256, *[h.shape[0] for h in h_groups])
    assert (TOP_K * tm) % DMA_UNROLL == 0
    ranges = _step_ranges([h.shape[0] // tm for h in h_groups])
    n = sum(count for _, count in ranges)
    dest3 = dest.reshape(n, 1, TOP_K * tm)
    smem = pltpu.SMEM
    clamp = lambda i, first, count: jnp.clip(i - first, 0, count - 1)
    rows = [pl.BlockSpec((tm, d), lambda i, first=first, count=count: (clamp(i, first, count), 0))
            for first, count in ranges]
    return pl.pallas_call(
        functools.partial(_combine_kernel, tm=tm, nsub=nsub, ranges=ranges),
        grid=(n,),
        in_specs=[
            pl.BlockSpec((None, 1, TOP_K * tm), lambda i: (i, 0, 0), memory_space=smem),
            pl.BlockSpec((None, 1, TOP_K * tm), lambda i: (jnp.minimum(i + 1, n - 1), 0, 0),
                         memory_space=smem),
            pl.BlockSpec((tm, TOP_K), lambda i: (i, 0)),
            pl.BlockSpec(memory_space=pl.ANY),
        ] + rows,
        out_specs=rows,
        out_shape=[jax.ShapeDtypeStruct(h.shape, F32) for h in h_groups],
        scratch_shapes=[pltpu.VMEM((2, TOP_K, tm * nsub, LANES), F32),
                        pltpu.SemaphoreType.DMA((2,))],
        compiler_params=_params(("arbitrary",)),
        name="moe_combine",
    )(dest3, dest3, w, og, *h_groups)


def _moe(groups, w_gu, w_down):
    ne, _, d = w_down.shape
    nsub = d // LANES
    idx = jnp.concatenate([g[2] for g in groups], axis=0)
    w = jnp.concatenate([g[3] for g in groups], axis=0)
    npairs = idx.shape[0] * TOP_K
    tmg = math.gcd(MOE_ROWS, npairs)
    dest, fill, tile_expert, n_tiles = _route_plan(idx, ne, tmg)
    n_rows = npairs + ne * tmg
    xg = _dispatch([g[1] for g in groups], dest, fill, n_rows, nsub)
    og = _moe_ffn(xg.reshape(n_rows * nsub, LANES), tile_expert, n_tiles, w_gu, w_down, tmg)
    return _combine([g[0] for g in groups], w, og.reshape(n_rows, nsub, LANES), dest)


def _trunk(x, pool_hist, kv_cache, p):
    b, n, d = x.shape
    n_heads = p["b_f"].shape[0]
    start_pos = 0 if kv_cache is None else kv_cache[0].shape[1]

    pool_args = (p["g_pool_norm"][0], p["w_pool"][0], p["pool_scale"][0], start_pos)
    ffn_args = (p["g_ffn_norm"][0], p["w_ffn_gu"][0], p["w_ffn_down"][0])
    if n >= 256:
        h, pool_state = _pool_layer(x, None if pool_hist is None else pool_hist[0], *pool_args,
                                    ffn=ffn_args)
        h = h.reshape(b * n, d)
    else:
        h, pool_state = _pool_layer(x, None if pool_hist is None else pool_hist[0], *pool_args)
        h = _dense_ffn(h.reshape(b * n, d), *ffn_args)

    qkv_args = (p["g_kv_norm"], p["g_attn_norm"][0], p["w_kvf"], p["b_f"], p["g_k"],
                p["w_q"][0], p["g_q"][0], n_heads)
    kvw = p["w_q"].shape[2]
    hd = kvw // n_heads
    if kv_cache is None:
        kt, vt, logf, k_aug, vtb, q_aug = _qkv_proj_t(h, b, *qkv_args)
        k = jnp.transpose(kt.reshape(b, n_heads, hd, n), (0, 3, 1, 2))
        v = jnp.transpose(vt.reshape(b, n_heads, hd, n), (0, 3, 1, 2))
        logf = logf.reshape(b, n, n_heads)
        o = _attn_prompt(q_aug, k_aug, vtb)
    else:
        k, v, logf, kb, vb, qb = _qkv_proj(h, *qkv_args)
        logf = logf.reshape(b, n, n_heads)
        qb, kb, vb = (a.reshape(b, n, kvw) for a in (qb, kb, vb))
        k = k.reshape(b, n, n_heads, hd)
        v = v.reshape(b, n, n_heads, hd)
        cache_k, cache_v, cache_logf = kv_cache
        past = cache_k.shape[1]
        total = past + n
        padded = -(-total // LANES) * LANES
        lf_t = jnp.concatenate(
            [jnp.transpose(cache_logf.astype(F32), (0, 2, 1)), jnp.transpose(logf, (0, 2, 1)),
             jnp.zeros((b, n_heads, padded - total), F32)], axis=2)
        c_t = _cumsum_lanes(lf_t.reshape(b * n_heads, padded)).reshape(b, n_heads, 1, padded)
        c_new = c_t[:, :, :, past:total]
        cache_kt = jnp.transpose(cache_k, (0, 2, 3, 1)).reshape(b, kvw, past)
        cache_vt = jnp.transpose(cache_v, (0, 2, 3, 1)).reshape(b, kvw, past)
        heads = lambda a: jnp.transpose(a.reshape(b, n, n_heads, hd), (0, 2, 1, 3))
        o = _attn_sample(heads(qb), heads(kb), heads(vb), cache_kt, cache_vt,
                         jnp.swapaxes(c_new, 2, 3), c_t, c_new)
        o = jnp.transpose(o, (0, 2, 1, 3))

    h2, u_tiles, idx, top_w = _oproj_router(h, o.reshape(b * n, kvw), p["w_o"][0],
                                            p["g_ffn_norm"][1], p["w_router"][0])
    return (h2, u_tiles, idx, top_w), pool_state[None], k, v, logf


def kernel(x_prompt, x_sample, state_pool, cache_k, cache_v, cache_logf, g_pool_norm, w_pool,
           pool_scale, g_kv_norm, w_kvf, b_f, g_k, g_attn_norm, w_q, g_q, w_o, g_ffn_norm,
           w_ffn_gu, w_ffn_down, w_router, w_moe_gu, w_moe_down):
    p = dict(g_pool_norm=g_pool_norm, w_pool=w_pool, pool_scale=pool_scale,
             g_kv_norm=g_kv_norm, w_kvf=w_kvf, b_f=b_f, g_k=g_k, g_attn_norm=g_attn_norm,
             w_q=w_q, g_q=g_q, w_o=w_o, g_ffn_norm=g_ffn_norm, w_ffn_gu=w_ffn_gu,
             w_ffn_down=w_ffn_down, w_router=w_router, w_moe_gu=w_moe_gu,
             w_moe_down=w_moe_down)
    assert w_pool.shape[0] == 1 and w_q.shape[0] == 1 and w_router.shape[0] == 1
    moe_p, pool_p, k_p, v_p, lf_p = _trunk(x_prompt, None, None, p)
    moe_s, pool_s, k_s, v_s, lf_s = _trunk(
        x_sample, state_pool, (cache_k, cache_v, cache_logf), p)
    y_p, y_s = _moe([moe_p, moe_s], w_moe_gu[0], w_moe_down[0])
    return (y_p.reshape(x_prompt.shape), y_s.reshape(x_sample.shape), pool_p, k_p, v_p, lf_p,
            pool_s, k_s, v_s, lf_s)
```
